```python
import jax, jax.numpy as jnp
from jax import lax
import numpy as np

D_MODEL = 1024
BATCH = 16
SEQ = 2048
DEPTH = 1

CHUNK = 64
LEFT_CHUNKS = 8
BAND = (LEFT_CHUNKS + 1) * CHUNK
Q_BLOCK = 128
MLA_HEADS = 8
MLA_Q_RANK = 256
MLA_KV_RANK = 128
MLA_NOPE = 64
MLA_ROPE = 32
MLA_QK = MLA_NOPE + MLA_ROPE
MLA_V = 64
ROPE_THETA = 10000.0
CA_HEADS = 8
CA_HEAD_DIM = 64
REL_CLIP = 128
D_MIX = MLA_HEADS * MLA_V + CA_HEADS * CA_HEAD_DIM
D_IN = MLA_Q_RANK + MLA_KV_RANK + MLA_ROPE + 3 * CA_HEADS * CA_HEAD_DIM
D_FF = 2816
CONV_WIDTH = 3
N_MOD = 6
EPS = 1e-6
NEG_INF = -1e30

kernel_name = "hybrid_mla_chunkband_convffn_adaln"


def rmsnorm(x, g):
    xf = x.astype(jnp.float32)
    y = xf * lax.rsqrt(jnp.mean(xf * xf, axis=-1, keepdims=True) + EPS)
    return (y * g.astype(jnp.float32)).astype(x.dtype)


def modulate(h, shift, scale):
    return h * (1 + scale[:, None, :]) + shift[:, None, :]


def rope(x, positions):
    half = x.shape[-1] // 2
    inv = jnp.power(ROPE_THETA, -jnp.arange(half, dtype=jnp.float32) / half)
    ang = positions.astype(jnp.float32)[..., None] * inv
    cos = jnp.cos(ang)[:, :, None, :]
    sin = jnp.sin(ang)[:, :, None, :]
    xf = x.astype(jnp.float32)
    x1, x2 = xf[..., :half], xf[..., half:]
    out = jnp.concatenate([x1 * cos - x2 * sin, x2 * cos + x1 * sin], axis=-1)
    return out.astype(x.dtype)


def mla_attention(q, k, v):
    B, S, H, Dq = q.shape
    nqb = S // Q_BLOCK
    scale = Dq ** -0.5
    key_chunk = jnp.arange(S) // CHUNK
    qb = q.reshape(B, nqb, Q_BLOCK, H, Dq).swapaxes(0, 1)

    def block(args):
        qi, i = args
        s = jnp.einsum('bqhd,bkhd->bhqk', qi, k,
                       preferred_element_type=jnp.float32) * scale
        q_chunk = (i * Q_BLOCK + jnp.arange(Q_BLOCK)) // CHUNK
        mask = key_chunk[None, :] <= q_chunk[:, None]
        s = jnp.where(mask[None, None], s, NEG_INF)
        p = jax.nn.softmax(s, axis=-1).astype(v.dtype)
        return jnp.einsum('bhqk,bkhd->bqhd', p, v)

    o = lax.map(block, (qb, jnp.arange(nqb)))
    return o.swapaxes(0, 1).reshape(B, S, H * v.shape[-1])


def chunk_attention(q, k, v, rel_bias):
    B, S, H, D = q.shape
    nc = S // CHUNK
    pad = LEFT_CHUNKS * CHUNK
    scale = D ** -0.5
    k_pad = jnp.pad(k, ((0, 0), (pad, 0), (0, 0), (0, 0)))
    v_pad = jnp.pad(v, ((0, 0), (pad, 0), (0, 0), (0, 0)))
    qi = jnp.arange(CHUNK)
    kj = jnp.arange(BAND)
    rel = (pad + qi[:, None]) - kj[None, :]
    bias = rel_bias[:, jnp.clip(rel, -REL_CLIP, REL_CLIP) + REL_CLIP].astype(jnp.float32)
    qc = q.reshape(B, nc, CHUNK, H, D).swapaxes(0, 1)

    def one_chunk(args):
        qch, ci = args
        kb = lax.dynamic_slice_in_dim(k_pad, ci * CHUNK, BAND, axis=1)
        vb = lax.dynamic_slice_in_dim(v_pad, ci * CHUNK, BAND, axis=1)
        s = jnp.einsum('bqhd,bkhd->bhqk', qch, kb,
                       preferred_element_type=jnp.float32) * scale + bias[None]
        valid = kj >= (LEFT_CHUNKS - ci) * CHUNK
        s = jnp.where(valid[None, None, None, :], s, NEG_INF)
        p = jax.nn.softmax(s, axis=-1).astype(vb.dtype)
        return jnp.einsum('bhqk,bkhd->bqhd', p, vb)

    o = lax.map(one_chunk, (qc, jnp.arange(nc)))
    return o.swapaxes(0, 1).reshape(B, S, H * D)


def causal_dwconv(u, w, b):
    S = u.shape[1]
    up = jnp.pad(u, ((0, 0), (CONV_WIDTH - 1, 0), (0, 0)))
    out = up[:, 0:S] * w[0]
    for j in range(1, CONV_WIDTH):
        out = out + up[:, j:j + S] * w[j]
    return out + b


def setup_inputs(seed: int = 0) -> dict:
    key = jax.random.key(seed)
    ks = jax.random.split(key, 24)
    f32 = jnp.float32

    def nrm(k, shape, fan_in):
        return jax.random.normal(k, shape, f32) * (fan_in ** -0.5)

    def gain(k, shape):
        return 1.0 + 0.05 * jax.random.normal(k, shape, f32)

    L = DEPTH
    x = jax.random.normal(ks[0], (BATCH, SEQ, D_MODEL), f32)
    c = jax.random.normal(ks[1], (BATCH, D_MODEL), f32)
    offsets = jax.random.randint(ks[2], (BATCH, 1), 0, 4096, dtype=jnp.int32)
    positions = (offsets + jnp.arange(SEQ, dtype=jnp.int32)[None, :]).astype(jnp.int32)
    return {
        "x": x,
        "c": c,
        "positions": positions,
        "w_ada": nrm(ks[3], (L, D_MODEL, N_MOD * D_MODEL), D_MODEL),
        "b_ada": 0.02 * jax.random.normal(ks[4], (L, N_MOD * D_MODEL), f32),
        "g_attn_norm": gain(ks[5], (L, D_MODEL)),
        "w_in": nrm(ks[6], (L, D_MODEL, D_IN), D_MODEL),
        "g_q_latent": gain(ks[7], (L, MLA_Q_RANK)),
        "g_kv_latent": gain(ks[8], (L, MLA_KV_RANK)),
        "w_q_up": nrm(ks[9], (L, MLA_Q_RANK, MLA_HEADS * MLA_QK), MLA_Q_RANK),
        "w_kv_up": nrm(ks[10], (L, MLA_KV_RANK, MLA_HEADS * (MLA_NOPE + MLA_V)), MLA_KV_RANK),
        "g_mla_q": gain(ks[11], (L, MLA_QK)),
        "g_mla_k": gain(ks[12], (L, MLA_QK)),
        "g_ca_q": gain(ks[13], (L, CA_HEAD_DIM)),
        "g_ca_k": gain(ks[14], (L, CA_HEAD_DIM)),
        "rel_bias": 0.5 * jax.random.normal(ks[15], (L, CA_HEADS, 2 * REL_CLIP + 1), f32),
        "w_out": nrm(ks[16], (L, D_MIX, D_MODEL), D_MIX),
        "g_mlp_norm": gain(ks[17], (L, D_MODEL)),
        "w_up": nrm(ks[18], (L, D_MODEL, 2 * D_FF), D_MODEL),
        "conv_w": nrm(ks[19], (L, CONV_WIDTH, 2 * D_FF), CONV_WIDTH),
        "conv_b": 0.02 * jax.random.normal(ks[20], (L, 2 * D_FF), f32),
        "w_down": nrm(ks[21], (L, D_FF, D_MODEL), D_FF),
    }


def reference(x, c, positions, w_ada, b_ada, g_attn_norm, w_in, g_q_latent, g_kv_latent,
              w_q_up, w_kv_up, g_mla_q, g_mla_k, g_ca_q, g_ca_k, rel_bias, w_out,
              g_mlp_norm, w_up, conv_w, conv_b, w_down):
    B, S, _ = x.shape
    split_pts = [MLA_Q_RANK, MLA_Q_RANK + MLA_KV_RANK, MLA_Q_RANK + MLA_KV_RANK + MLA_ROPE]
    for l in range(DEPTH):
        mod = jnp.dot(jax.nn.silu(c), w_ada[l]) + b_ada[l]
        sh_a, sc_a, g_a, sh_m, sc_m, g_m = jnp.split(mod, N_MOD, axis=-1)

        h = modulate(rmsnorm(x, g_attn_norm[l]), sh_a, sc_a)
        proj = jnp.dot(h, w_in[l])
        q_lat, kv_lat, k_rope, ca_qkv = jnp.split(proj, split_pts, axis=-1)

        q = jnp.dot(rmsnorm(q_lat, g_q_latent[l]), w_q_up[l]).reshape(B, S, MLA_HEADS, MLA_QK)
        kv = jnp.dot(rmsnorm(kv_lat, g_kv_latent[l]), w_kv_up[l]).reshape(B, S, MLA_HEADS, MLA_NOPE + MLA_V)
        k_nope, v = kv[..., :MLA_NOPE], kv[..., MLA_NOPE:]
        k = jnp.concatenate(
            [k_nope, jnp.broadcast_to(k_rope[:, :, None, :], (B, S, MLA_HEADS, MLA_ROPE))], axis=-1)
        q = rmsnorm(q, g_mla_q[l])
        k = rmsnorm(k, g_mla_k[l])
        q = jnp.concatenate([q[..., :MLA_NOPE], rope(q[..., MLA_NOPE:], positions)], axis=-1)
        k = jnp.concatenate([k[..., :MLA_NOPE], rope(k[..., MLA_NOPE:], positions)], axis=-1)
        o_mla = mla_attention(q, k, v)

        ca = ca_qkv.reshape(B, S, 3, CA_HEADS, CA_HEAD_DIM)
        cq = rmsnorm(ca[:, :, 0], g_ca_q[l])
        ck = rmsnorm(ca[:, :, 1], g_ca_k[l])
        cv = ca[:, :, 2]
        o_ca = chunk_attention(cq, ck, cv, rel_bias[l])

        mixed = jnp.dot(jnp.concatenate([o_mla, o_ca], axis=-1), w_out[l])
        x = x + g_a[:, None, :] * mixed

        h = modulate(rmsnorm(x, g_mlp_norm[l]), sh_m, sc_m)
        u = causal_dwconv(jnp.dot(h, w_up[l]), conv_w[l], conv_b[l])
        gate, val = jnp.split(u, 2, axis=-1)
        x = x + g_m[:, None, :] * jnp.dot(jax.nn.silu(gate) * val, w_down[l])
    return x
```

```python
import functools

import jax
import jax.numpy as jnp
from jax import lax
from jax.experimental import pallas as pl
from jax.experimental.pallas import tpu as pltpu

D_MODEL = 1024
CHUNK = 64
LEFT_CHUNKS = 8
MLA_HEADS = 8
MLA_Q_RANK = 256
MLA_KV_RANK = 128
MLA_NOPE = 64
MLA_ROPE = 32
MLA_QK = MLA_NOPE + MLA_ROPE
MLA_V = 64
ROPE_THETA = 10000.0
CA_HEADS = 8
CA_HEAD_DIM = 64
REL_CLIP = 128
D_FF = 2816
N_MOD = 6
EPS = 1e-6
NEG_INF = -1e30

LANES = 128
BF16_SUBLANES = 16
VMEM_LIMIT_BYTES = 56 * 1024 * 1024

PROJ_TM = 512
MLA_TQ = 256
CA_TQ = 128
CA_KB = 5
FIN_TM = 512
FIN_HALO = BF16_SUBLANES
FF_CHUNK = 256
N_FF_CHUNKS = D_FF // FF_CHUNK

_C_QLAT = 0
_C_KVLAT = _C_QLAT + MLA_Q_RANK
_C_KA = _C_KVLAT + MLA_KV_RANK
_C_KB = _C_KA + LANES
_C_CQ = _C_KB + LANES
_C_CK = _C_CQ + CA_HEADS * CA_HEAD_DIM
_C_CV = _C_CK + CA_HEADS * CA_HEAD_DIM
D_IN_EXT = _C_CV + CA_HEADS * CA_HEAD_DIM

F32 = jnp.float32
BF16 = jnp.bfloat16


def _dot(a, b):
    return jnp.dot(a, b, preferred_element_type=F32)


def _dot_nt(a, b):
    return lax.dot_general(a, b, (((1,), (1,)), ((), ())), preferred_element_type=F32)


def _compiler_params(semantics):
    return pltpu.CompilerParams(dimension_semantics=semantics, vmem_limit_bytes=VMEM_LIMIT_BYTES)


def _ada_kernel(c_ref, w_ref, b_ref, o_ref):
    c = c_ref[...]
    s = c / (1.0 + jnp.exp(-c))
    o_ref[...] = _dot(s.astype(BF16), w_ref[...].astype(BF16)) + b_ref[...]


def _ada(c, w_ada, b_ada):
    bsz = c.shape[0]
    n_out = w_ada.shape[1]
    tn = D_MODEL
    return pl.pallas_call(
        _ada_kernel,
        grid=(n_out // tn,),
        in_specs=[
            pl.BlockSpec((bsz, D_MODEL), lambda j: (0, 0)),
            pl.BlockSpec((D_MODEL, tn), lambda j: (0, j)),
            pl.BlockSpec((1, tn), lambda j: (0, j)),
        ],
        out_specs=pl.BlockSpec((bsz, tn), lambda j: (0, j)),
        out_shape=jax.ShapeDtypeStruct((bsz, n_out), F32),
        compiler_params=_compiler_params(("arbitrary",)),
        name="ada",
    )(c, w_ada, b_ada.reshape(1, n_out))


def _rope_kernel(pos_ref, inv_ref, cos_ref, sin_ref):
    ang = pos_ref[...].astype(F32) * inv_ref[...]
    cos_ref[...] = jnp.cos(ang)
    sin_ref[...] = jnp.sin(ang)


def _rope_tables(positions):
    half = MLA_ROPE // 2
    n_tok = positions.size
    rows = n_tok * half // LANES
    inv = jnp.power(ROPE_THETA, -jnp.arange(half, dtype=F32) / half)
    inv_t = jnp.tile(inv, LANES // half).reshape(1, LANES)
    pos_rep = jnp.repeat(positions.reshape(n_tok, 1), half, axis=1).reshape(rows, LANES)
    tr = 512
    cos, sin = pl.pallas_call(
        _rope_kernel,
        grid=(rows // tr,),
        in_specs=[pl.BlockSpec((tr, LANES), lambda i: (i, 0)), pl.BlockSpec((1, LANES), lambda i: (0, 0))],
        out_specs=[pl.BlockSpec((tr, LANES), lambda i: (i, 0))] * 2,
        out_shape=[jax.ShapeDtypeStruct((rows, LANES), F32)] * 2,
        compiler_params=_compiler_params(("arbitrary",)),
        name="rope",
    )(pos_rep, inv_t)
    return cos.reshape(n_tok, half), sin.reshape(n_tok, half)


def _bias_kernel(y_ref, o_ref):
    for t in range(3):
        row = y_ref[0, t:t + 1, :]
        full = jnp.broadcast_to(row, (LANES, 2 * LANES))
        rolled = pltpu.roll(full, 0, 1, stride=1, stride_axis=0)
        o_ref[0, t] = rolled[:, :LANES]


def _bias_tiles(rel_bias):
    n_heads = rel_bias.shape[0]
    t = jnp.arange(2 * LANES)
    d = jnp.where(t < LANES, t, t - 2 * LANES)
    idx0 = jnp.clip(-d, -REL_CLIP, REL_CLIP) + REL_CLIP
    idx1 = jnp.clip(LANES - d, -REL_CLIP, REL_CLIP) + REL_CLIP
    idx2 = jnp.full_like(t, 2 * REL_CLIP)
    y = jnp.stack([rel_bias[:, idx0], rel_bias[:, idx1], rel_bias[:, idx2]], axis=1)
    return pl.pallas_call(
        _bias_kernel,
        grid=(n_heads,),
        in_specs=[pl.BlockSpec((1, 3, 2 * LANES), lambda h: (h, 0, 0))],
        out_specs=pl.BlockSpec((1, 3, LANES, LANES), lambda h: (h, 0, 0, 0)),
        out_shape=jax.ShapeDtypeStruct((n_heads, 3, LANES, LANES), F32),
        compiler_params=_compiler_params(("arbitrary",)),
        name="bias",
    )(y)


def _proj_kernel(x_ref, mod_ref, tc_ref, ts_ref, gattn_ref, gql_ref, gkvl_ref, gv_ref,
                 win_ref, wq_ref, wkv_ref,
                 q_ref, k_ref, v_ref, cq_ref, ck_ref, cv_ref):
    x = x_ref[0]
    mod = mod_ref[0]
    sh, sc = mod[0:1, :], mod[1:2, :]
    y = x * lax.rsqrt(jnp.mean(x * x, axis=-1, keepdims=True) + EPS) * gattn_ref[...]
    h = (y * (1.0 + sc) + sh).astype(BF16)
    proj = _dot(h, win_ref[...])

    lane = lax.broadcasted_iota(jnp.int32, (1, LANES), 1)
    gv = gv_ref[...]
    g_q, g_kn, g_ka, g_kb, g_cq, g_ck = (gv[r:r + 1, :] for r in range(6))

    tc = tc_ref[0]
    ts = ts_ref[0]
    tq = jnp.where(lane < MLA_QK, tc, ts)

    ql = proj[:, _C_QLAT:_C_QLAT + MLA_Q_RANK]
    qn = (ql * lax.rsqrt(jnp.mean(ql * ql, axis=-1, keepdims=True) + EPS) * gql_ref[...]).astype(BF16)
    qp = _dot(qn, wq_ref[...])
    qmul = g_q * tq
    for hd in range(MLA_HEADS):
        qh = qp[:, hd * LANES:(hd + 1) * LANES]
        ss = jnp.sum(jnp.where(lane < MLA_QK, qh * qh, 0.0), axis=-1, keepdims=True)
        r = lax.rsqrt(ss * (1.0 / MLA_QK) + EPS)
        q_ref[0, :, hd * LANES:(hd + 1) * LANES] = (qh * r * qmul).astype(BF16)

    kvl = proj[:, _C_KVLAT:_C_KVLAT + MLA_KV_RANK]
    kvn = (kvl * lax.rsqrt(jnp.mean(kvl * kvl, axis=-1, keepdims=True) + EPS) * gkvl_ref[...]).astype(BF16)
    kvp = _dot(kvn, wkv_ref[...])
    ka = proj[:, _C_KA:_C_KA + LANES]
    kb = proj[:, _C_KB:_C_KB + LANES]
    krot = ka * g_ka * tc + kb * g_kb * ts
    ss_rope = jnp.sum(jnp.where(lane < MLA_QK, ka * ka, 0.0), axis=-1, keepdims=True)
    for hd in range(MLA_HEADS):
        kh = kvp[:, hd * LANES:(hd + 1) * LANES]
        ss = jnp.sum(kh * kh, axis=-1, keepdims=True) + ss_rope
        r = lax.rsqrt(ss * (1.0 / MLA_QK) + EPS)
        k_ref[0, :, hd * LANES:(hd + 1) * LANES] = ((kh * g_kn + krot) * r).astype(BF16)
    v_ref[0] = kvp[:, MLA_HEADS * LANES:].astype(BF16)

    lo = lane < CA_HEAD_DIM
    for src, gain, dst in ((_C_CQ, g_cq, cq_ref), (_C_CK, g_ck, ck_ref)):
        for p in range(CA_HEADS // 2):
            xx = proj[:, src + p * LANES:src + (p + 1) * LANES]
            x2 = xx * xx
            s_all = jnp.sum(x2, axis=-1, keepdims=True)
            s_lo = jnp.sum(jnp.where(lo, x2, 0.0), axis=-1, keepdims=True)
            r_lo = lax.rsqrt(s_lo * (1.0 / CA_HEAD_DIM) + EPS)
            r_hi = lax.rsqrt((s_all - s_lo) * (1.0 / CA_HEAD_DIM) + EPS)
            dst[0, :, p * LANES:(p + 1) * LANES] = (xx * jnp.where(lo, r_lo, r_hi) * gain).astype(BF16)
    cv_ref[0] = proj[:, _C_CV:].astype(BF16)


def _proj(x, mod, tc, ts, g_attn, g_ql, g_kvl, gvec, w_in_ext, w_q_ext, w_kv_ext):
    bsz, seq, _ = x.shape
    tm = PROJ_TM
    tok = lambda b, i: (b, i, 0)
    const2 = lambda b, i: (0, 0)
    wide = MLA_HEADS * LANES
    narrow = CA_HEADS * CA_HEAD_DIM
    out_shapes = [jax.ShapeDtypeStruct((bsz, seq, w), BF16) for w in (wide, wide, narrow, narrow, narrow, narrow)]
    out_specs = [pl.BlockSpec((1, tm, w), tok) for w in (wide, wide, narrow, narrow, narrow, narrow)]
    return pl.pallas_call(
        _proj_kernel,
        grid=(bsz, seq // tm),
        in_specs=[
            pl.BlockSpec((1, tm, D_MODEL), tok),
            pl.BlockSpec((1, N_MOD, D_MODEL), lambda b, i: (b, 0, 0)),
            pl.BlockSpec((1, tm, LANES), tok),
            pl.BlockSpec((1, tm, LANES), tok),
            pl.BlockSpec((1, D_MODEL), const2),
            pl.BlockSpec((1, MLA_Q_RANK), const2),
            pl.BlockSpec((1, MLA_KV_RANK), const2),
            pl.BlockSpec((8, LANES), const2),
            pl.BlockSpec((D_MODEL, D_IN_EXT), const2),
            pl.BlockSpec((MLA_Q_RANK, wide), const2),
            pl.BlockSpec((MLA_KV_RANK, wide + MLA_HEADS * MLA_V), const2),
        ],
        out_specs=out_specs,
        out_shape=out_shapes,
        compiler_params=_compiler_params(("parallel", "parallel")),
        name="proj",
    )(x, mod, tc, ts, g_attn, g_ql, g_kvl, gvec, w_in_ext, w_q_ext, w_kv_ext)


def _mla_kernel(q_ref, k_ref, v_ref, o_ref):
    i = pl.program_id(2)
    t = MLA_TQ
    lane = lax.broadcasted_iota(jnp.int32, (1, LANES), 1)
    row_chunk = lax.broadcasted_iota(jnp.int32, (t, 1), 0) // CHUNK
    col_chunk = lax.broadcasted_iota(jnp.int32, (1, t), 1) // CHUNK
    diag_mask = col_chunk <= row_chunk

    def update(carry, s, vb):
        m, l, acc = carry
        m_new = jnp.maximum(m, jnp.max(s, axis=-1, keepdims=True))
        alpha = jnp.exp(m - m_new)
        p = jnp.exp(s - m_new)
        l = alpha * l + jnp.sum(p, axis=-1, keepdims=True)
        acc = alpha * acc + _dot(p.astype(BF16), vb)
        return m_new, l, acc

    outs = []
    for hh in range(2):
        q = q_ref[0, :, hh * LANES:(hh + 1) * LANES]

        def body(j, carry, q=q, hh=hh):
            start = pl.multiple_of(j * t, t)
            kb = k_ref[0, pl.ds(start, t), hh * LANES:(hh + 1) * LANES]
            vb = v_ref[0, pl.ds(start, t), :]
            return update(carry, _dot_nt(q, kb), vb)

        init = (jnp.full((t, 1), NEG_INF, F32), jnp.zeros((t, 1), F32), jnp.zeros((t, LANES), F32))
        carry = lax.fori_loop(0, i, body, init)
        start = pl.multiple_of(i * t, t)
        kb = k_ref[0, pl.ds(start, t), hh * LANES:(hh + 1) * LANES]
        vb = v_ref[0, pl.ds(start, t), :]
        s = jnp.where(diag_mask, _dot_nt(q, kb), NEG_INF)
        _, l, acc = update(carry, s, vb)
        outs.append(acc / l)
    o_ref[0] = jnp.where(lane < MLA_V, outs[0], outs[1]).astype(BF16)


def _mla(q, k, v):
    bsz, seq, _ = q.shape
    t = MLA_TQ
    pairs = MLA_HEADS // 2
    return pl.pallas_call(
        _mla_kernel,
        grid=(bsz, pairs, seq // t),
        in_specs=[
            pl.BlockSpec((1, t, 2 * LANES), lambda b, p, i: (b, i, p)),
            pl.BlockSpec((1, seq, 2 * LANES), lambda b, p, i: (b, 0, p)),
            pl.BlockSpec((1, seq, LANES), lambda b, p, i: (b, 0, p)),
        ],
        out_specs=pl.BlockSpec((1, t, LANES), lambda b, p, i: (b, i, p)),
        out_shape=jax.ShapeDtypeStruct((bsz, seq, MLA_HEADS * MLA_V), BF16),
        compiler_params=_compiler_params(("parallel", "parallel", "arbitrary")),
        name="mla",
    )(q, k, v)


def _ca_kernel(q_ref, k_ref, v_ref, bt_ref, o_ref):
    i = pl.program_id(2)
    t = CA_TQ
    lane = lax.broadcasted_iota(jnp.int32, (1, LANES), 1)
    q_hi = lax.broadcasted_iota(jnp.int32, (t, 1), 0) >= CHUNK
    k_hi = lax.broadcasted_iota(jnp.int32, (1, t), 1) >= CHUNK
    oldest_ok = jnp.logical_or(jnp.logical_not(q_hi), k_hi)
    newest_ok = jnp.logical_or(q_hi, jnp.logical_not(k_hi))
    q = q_ref[0]
    tile_of_block = (2, 2, 2, 1, 0)

    outs = []
    for hh in range(2):
        head_lanes = (lane < CA_HEAD_DIM) if hh == 0 else (lane >= CA_HEAD_DIM)
        qm = jnp.where(head_lanes, q, jnp.zeros_like(q))
        s_blocks, v_blocks = [], []
        for j in range(CA_KB):
            kblk = i - (CA_KB - 1) + j
            start = pl.multiple_of(jnp.maximum(kblk, 0) * t, t)
            kb = k_ref[0, pl.ds(start, t), :]
            v_blocks.append(v_ref[0, pl.ds(start, t), :])
            s = _dot_nt(qm, kb) + bt_ref[hh, tile_of_block[j]]
            valid = (jnp.zeros((t, 1), jnp.int32) + kblk) >= 0
            if j == 0:
                valid = jnp.logical_and(valid, oldest_ok)
            elif j == CA_KB - 1:
                valid = jnp.logical_and(valid, newest_ok)
            s_blocks.append(jnp.where(valid, s, NEG_INF))
        m = s_blocks[0].max(axis=-1, keepdims=True)
        for s in s_blocks[1:]:
            m = jnp.maximum(m, s.max(axis=-1, keepdims=True))
        l = jnp.zeros((t, 1), F32)
        acc = jnp.zeros((t, LANES), F32)
        for s, vb in zip(s_blocks, v_blocks):
            p = jnp.exp(s - m)
            l = l + jnp.sum(p, axis=-1, keepdims=True)
            acc = acc + _dot(p.astype(BF16), vb)
        outs.append(acc / l)
    o_ref[0] = jnp.where(lane < CA_HEAD_DIM, outs[0], outs[1]).astype(BF16)


def _ca(cq, ck, cv, bias_tiles):
    bsz, seq, _ = cq.shape
    t = CA_TQ
    pairs = CA_HEADS // 2
    return pl.pallas_call(
        _ca_kernel,
        grid=(bsz, pairs, seq // t),
        in_specs=[
            pl.BlockSpec((1, t, LANES), lambda b, p, i: (b, i, p)),
            pl.BlockSpec((1, seq, LANES), lambda b, p, i: (b, 0, p)),
            pl.BlockSpec((1, seq, LANES), lambda b, p, i: (b, 0, p)),
            pl.BlockSpec((2, 3, LANES, LANES), lambda b, p, i: (p, 0, 0, 0)),
        ],
        out_specs=pl.BlockSpec((1, t, LANES), lambda b, p, i: (b, i, p)),
        out_shape=jax.ShapeDtypeStruct((bsz, seq, CA_HEADS * CA_HEAD_DIM), BF16),
        compiler_params=_compiler_params(("parallel", "parallel", "arbitrary")),
        name="ca",
    )(cq, ck, cv, bias_tiles)


def _final_kernel(x_ref, xh_ref, om_ref, omh_ref, oc_ref, och_ref, mod_ref, gmlp_ref,
                  wo_ref, wug_ref, wuv_ref, cwg_ref, cwv_ref, cbg_ref, cbv_ref, wd_ref,
                  out_ref, h2_sc, acc_sc):
    i = pl.program_id(1)
    tm = FIN_TM
    halo = FIN_HALO
    mod = mod_ref[0]
    g_a, sh_m, sc_m, g_m = mod[2:3, :], mod[3:4, :], mod[4:5, :], mod[5:6, :]

    x_ext = jnp.concatenate([xh_ref[0], x_ref[0]], axis=0)
    o_ext = jnp.concatenate(
        [jnp.concatenate([omh_ref[0], och_ref[0]], axis=-1), jnp.concatenate([om_ref[0], oc_ref[0]], axis=-1)], axis=0)
    x1 = x_ext + g_a * _dot(o_ext, wo_ref[...])
    y = x1 * lax.rsqrt(jnp.mean(x1 * x1, axis=-1, keepdims=True) + EPS) * gmlp_ref[...]
    h2_sc[...] = (y * (1.0 + sc_m) + sh_m).astype(BF16)
    acc_sc[...] = jnp.zeros_like(acc_sc)

    row = lax.broadcasted_iota(jnp.int32, (halo + tm, 1), 0)
    keep = (row + jnp.where(i > 0, halo, 0)) >= halo

    def conv(u, cw, cb):
        u = jnp.where(keep, u, 0.0)
        return (u[halo - 2:halo - 2 + tm] * cw[0:1, :] + u[halo - 1:halo - 1 + tm] * cw[1:2, :]
                + u[halo:] * cw[2:3, :] + cb)

    def body(c, carry):
        h2 = h2_sc[...]
        gate = conv(_dot(h2, wug_ref[c]), cwg_ref[c], cbg_ref[c])
        val = conv(_dot(h2, wuv_ref[c]), cwv_ref[c], cbv_ref[c])
        act = (gate / (1.0 + jnp.exp(-gate)) * val).astype(BF16)
        acc_sc[...] += _dot(act, wd_ref[c])
        return carry

    lax.fori_loop(0, N_FF_CHUNKS, body, 0)
    out_ref[0] = x1[halo:] + g_m * acc_sc[...]


def _final(x, o_mla, o_ca, mod, g_mlp, w_out, wug, wuv, cwg, cwv, cbg, cbv, wd):
    bsz, seq, _ = x.shape
    tm, halo = FIN_TM, FIN_HALO
    tok = lambda b, i: (b, i, 0)
    prev = lambda b, i: (b, jnp.maximum(i * (tm // halo) - 1, 0), 0)
    const2 = lambda b, i: (0, 0)
    const3 = lambda b, i: (0, 0, 0)
    d_mix = o_mla.shape[-1]
    resident = dict(pipeline_mode=pl.Buffered(1))
    return pl.pallas_call(
        _final_kernel,
        grid=(bsz, seq // tm),
        in_specs=[
            pl.BlockSpec((1, tm, D_MODEL), tok),
            pl.BlockSpec((1, halo, D_MODEL), prev),
            pl.BlockSpec((1, tm, d_mix), tok),
            pl.BlockSpec((1, halo, d_mix), prev),
            pl.BlockSpec((1, tm, d_mix), tok),
            pl.BlockSpec((1, halo, d_mix), prev),
            pl.BlockSpec((1, N_MOD, D_MODEL), lambda b, i: (b, 0, 0)),
            pl.BlockSpec((1, D_MODEL), const2),
            pl.BlockSpec((2 * d_mix, D_MODEL), const2, **resident),
            pl.BlockSpec((N_FF_CHUNKS, D_MODEL, FF_CHUNK), const3, **resident),
            pl.BlockSpec((N_FF_CHUNKS, D_MODEL, FF_CHUNK), const3, **resident),
            pl.BlockSpec((N_FF_CHUNKS, 3, FF_CHUNK), const3),
            pl.BlockSpec((N_FF_CHUNKS, 3, FF_CHUNK), const3),
            pl.BlockSpec((N_FF_CHUNKS, 1, FF_CHUNK), const3),
            pl.BlockSpec((N_FF_CHUNKS, 1, FF_CHUNK), const3),
            pl.BlockSpec((N_FF_CHUNKS, FF_CHUNK, D_MODEL), const3, **resident),
        ],
        out_specs=pl.BlockSpec((1, tm, D_MODEL), tok),
        out_shape=jax.ShapeDtypeStruct((bsz, seq, D_MODEL), F32),
        scratch_shapes=[pltpu.VMEM((halo + tm, D_MODEL), BF16), pltpu.VMEM((tm, D_MODEL), F32)],
        compiler_params=_compiler_params(("parallel", "arbitrary")),
        name="final",
    )(x, x, o_mla, o_mla, o_ca, o_ca, mod, g_mlp, w_out, wug, wuv, cwg, cwv, cbg, cbv, wd)


def _prep_layer(w_in, w_q_up, w_kv_up, g_mla_q, g_mla_k, g_ca_q, g_ca_k, w_up, conv_w, conv_b, w_down):
    half = MLA_ROPE // 2
    c0 = MLA_Q_RANK + MLA_KV_RANK
    k1 = w_in[:, c0:c0 + half]
    k2 = w_in[:, c0 + half:c0 + MLA_ROPE]
    z64 = jnp.zeros((D_MODEL, MLA_NOPE), w_in.dtype)
    w_in_ext = jnp.concatenate(
        [w_in[:, :c0], z64, k1, k2, k1, k2, z64, k2, k1, k2, k1, w_in[:, c0 + MLA_ROPE:]], axis=1).astype(BF16)

    wq = w_q_up.reshape(MLA_Q_RANK, MLA_HEADS, MLA_QK)
    x1 = wq[..., MLA_NOPE:MLA_NOPE + half]
    x2 = wq[..., MLA_NOPE + half:]
    w_q_ext = jnp.concatenate([wq, x2, x1], axis=-1).reshape(MLA_Q_RANK, MLA_HEADS * LANES).astype(BF16)

    wkv = w_kv_up.reshape(MLA_KV_RANK, MLA_HEADS, MLA_NOPE + MLA_V)
    wk = jnp.concatenate([wkv[..., :MLA_NOPE], jnp.zeros((MLA_KV_RANK, MLA_HEADS, LANES - MLA_NOPE), wkv.dtype)], axis=-1)
    w_kv_ext = jnp.concatenate(
        [wk.reshape(MLA_KV_RANK, MLA_HEADS * LANES), wkv[..., MLA_NOPE:].reshape(MLA_KV_RANK, MLA_HEADS * MLA_V)],
        axis=1).astype(BF16)

    gq1, gq2 = g_mla_q[MLA_NOPE:MLA_NOPE + half], g_mla_q[MLA_NOPE + half:]
    gk1, gk2 = g_mla_k[MLA_NOPE:MLA_NOPE + half], g_mla_k[MLA_NOPE + half:]
    z = jnp.zeros((MLA_NOPE,), F32)
    rows = [
        jnp.concatenate([g_mla_q, gq2, gq1]) * (MLA_QK ** -0.5),
        jnp.concatenate([g_mla_k[:MLA_NOPE], z]),
        jnp.concatenate([z, gk1, gk2, gk1, gk2]),
        jnp.concatenate([z, gk2, gk1, gk2, gk1]),
        jnp.concatenate([g_ca_q, g_ca_q]) * (CA_HEAD_DIM ** -0.5),
        jnp.concatenate([g_ca_k, g_ca_k]),
        jnp.zeros((LANES,), F32),
        jnp.zeros((LANES,), F32),
    ]
    gvec = jnp.stack(rows)

    def chunk_cols(w):
        return w.reshape(w.shape[0], N_FF_CHUNKS, FF_CHUNK).transpose(1, 0, 2)

    wug = chunk_cols(w_up[:, :D_FF]).astype(BF16)
    wuv = chunk_cols(w_up[:, D_FF:]).astype(BF16)
    cwg = chunk_cols(conv_w[:, :D_FF])
    cwv = chunk_cols(conv_w[:, D_FF:])
    cbg = conv_b[:D_FF].reshape(N_FF_CHUNKS, 1, FF_CHUNK)
    cbv = conv_b[D_FF:].reshape(N_FF_CHUNKS, 1, FF_CHUNK)
    wd = w_down.reshape(N_FF_CHUNKS, FF_CHUNK, D_MODEL).astype(BF16)
    return w_in_ext, w_q_ext, w_kv_ext, gvec, wug, wuv, cwg, cwv, cbg, cbv, wd


def kernel(x, c, positions, w_ada, b_ada, g_attn_norm, w_in, g_q_latent, g_kv_latent, w_q_up, w_kv_up, g_mla_q, g_mla_k, g_ca_q, g_ca_k, rel_bias, w_out, g_mlp_norm, w_up, conv_w, conv_b, w_down):
    bsz, seq, _ = x.shape
    depth = w_ada.shape[0]

    cos, sin = _rope_tables(positions)
    ones = jnp.ones((bsz * seq, MLA_NOPE), F32)
    tc = jnp.concatenate([ones, cos, cos, cos, cos], axis=1).reshape(bsz, seq, LANES)
    ts = jnp.concatenate([jnp.zeros_like(ones), -sin, sin, -sin, sin], axis=1).reshape(bsz, seq, LANES)

    for l in range(depth):
        (w_in_ext, w_q_ext, w_kv_ext, gvec, wug, wuv, cwg, cwv, cbg, cbv, wd) = _prep_layer(
            w_in[l], w_q_up[l], w_kv_up[l], g_mla_q[l], g_mla_k[l], g_ca_q[l], g_ca_k[l],
            w_up[l], conv_w[l], conv_b[l], w_down[l])
        mod = _ada(c, w_ada[l], b_ada[l]).reshape(bsz, N_MOD, D_MODEL)
        bias_tiles = _bias_tiles(rel_bias[l])
        q, k, v, cq, ck, cv = _proj(
            x, mod, tc, ts, g_attn_norm[l].reshape(1, -1), g_q_latent[l].reshape(1, -1),
            g_kv_latent[l].reshape(1, -1), gvec, w_in_ext, w_q_ext, w_kv_ext)
        o_mla = _mla(q, k, v)
        o_ca = _ca(cq, ck, cv, bias_tiles)
        x = _final(x, o_mla, o_ca, mod, g_mlp_norm[l].reshape(1, -1), w_out[l].astype(BF16),
                   wug, wuv, cwg, cwv, cbg, cbv, wd)
    return x
```

```python
import functools

import jax
import jax.numpy as jnp
from jax import lax
from jax.experimental import pallas as pl
from jax.experimental.pallas import tpu as pltpu

D_MODEL = 1024
CHUNK = 64
LEFT_CHUNKS = 8
MLA_HEADS = 8
MLA_Q_RANK = 256
MLA_KV_RANK = 128
MLA_NOPE = 64
MLA_ROPE = 32
MLA_QK = MLA_NOPE + MLA_ROPE
MLA_V = 64
ROPE_THETA = 10000.0
CA_HEADS = 8
CA_HEAD_DIM = 64
REL_CLIP = 128
D_FF = 2816
N_MOD = 6
EPS = 1e-6
NEG_INF = -1e30

LANES = 128
BF16_SUBLANES = 16
VMEM_LIMIT_BYTES = 56 * 1024 * 1024

PROJ_TM = 512
MLA_TQ = 256
CA_TQ = 256
CA_WIN = CA_TQ + LEFT_CHUNKS * CHUNK
CA_ROLL_W = 1024
FIN_TM = 512
FIN_HALO = BF16_SUBLANES
FF_CHUNK = 256
N_FF_CHUNKS = D_FF // FF_CHUNK

_C_QLAT = 0
_C_KVLAT = _C_QLAT + MLA_Q_RANK
_C_KA = _C_KVLAT + MLA_KV_RANK
_C_KB = _C_KA + LANES
_C_CQ = _C_KB + LANES
_C_CK = _C_CQ + CA_HEADS * CA_HEAD_DIM
_C_CV = _C_CK + CA_HEADS * CA_HEAD_DIM
D_IN_EXT = _C_CV + CA_HEADS * CA_HEAD_DIM

F32 = jnp.float32
BF16 = jnp.bfloat16


def _dot(a, b):
    return jnp.dot(a, b, preferred_element_type=F32)


def _dot_nt(a, b):
    return lax.dot_general(a, b, (((1,), (1,)), ((), ())), preferred_element_type=F32)


def _compiler_params(semantics):
    return pltpu.CompilerParams(dimension_semantics=semantics, vmem_limit_bytes=VMEM_LIMIT_BYTES)


def _ada_kernel(c_ref, w_ref, b_ref, o_ref):
    c = c_ref[...]
    s = c / (1.0 + jnp.exp(-c))
    o_ref[...] = _dot(s.astype(BF16), w_ref[...].astype(BF16)) + b_ref[...]


def _ada(c, w_ada, b_ada):
    bsz = c.shape[0]
    n_out = w_ada.shape[1]
    tn = D_MODEL
    return pl.pallas_call(
        _ada_kernel,
        grid=(n_out // tn,),
        in_specs=[
            pl.BlockSpec((bsz, D_MODEL), lambda j: (0, 0)),
            pl.BlockSpec((D_MODEL, tn), lambda j: (0, j)),
            pl.BlockSpec((1, tn), lambda j: (0, j)),
        ],
        out_specs=pl.BlockSpec((bsz, tn), lambda j: (0, j)),
        out_shape=jax.ShapeDtypeStruct((bsz, n_out), F32),
        compiler_params=_compiler_params(("arbitrary",)),
        name="ada",
    )(c, w_ada, b_ada.reshape(1, n_out))


def _rope_kernel(pos_ref, inv_ref, cos_ref, sin_ref):
    ang = pos_ref[...].astype(F32) * inv_ref[...]
    cos_ref[...] = jnp.cos(ang)
    sin_ref[...] = jnp.sin(ang)


def _rope_tables(positions):
    half = MLA_ROPE // 2
    n_tok = positions.size
    rows = n_tok * half // LANES
    inv = jnp.power(ROPE_THETA, -jnp.arange(half, dtype=F32) / half)
    inv_t = jnp.tile(inv, LANES // half).reshape(1, LANES)
    pos_rep = jnp.repeat(positions.reshape(n_tok, 1), half, axis=1).reshape(rows, LANES)
    tr = 512
    cos, sin = pl.pallas_call(
        _rope_kernel,
        grid=(rows // tr,),
        in_specs=[pl.BlockSpec((tr, LANES), lambda i: (i, 0)), pl.BlockSpec((1, LANES), lambda i: (0, 0))],
        out_specs=[pl.BlockSpec((tr, LANES), lambda i: (i, 0))] * 2,
        out_shape=[jax.ShapeDtypeStruct((rows, LANES), F32)] * 2,
        compiler_params=_compiler_params(("arbitrary",)),
        name="rope",
    )(pos_rep, inv_t)
    return cos.reshape(n_tok, half), sin.reshape(n_tok, half)


def _bias_kernel(y_ref, o_ref):
    tq, win = CA_TQ, CA_WIN
    full = jnp.broadcast_to(y_ref[0], (tq, CA_ROLL_W))
    rolled = pltpu.roll(full, 0, 1, stride=1, stride_axis=0)
    q_chunk = lax.broadcasted_iota(jnp.int32, (tq, 1), 0) // CHUNK
    k_chunk = lax.broadcasted_iota(jnp.int32, (1, win), 1) // CHUNK
    valid = jnp.logical_and(k_chunk >= q_chunk, k_chunk <= q_chunk + LEFT_CHUNKS)
    o_ref[0] = jnp.where(valid, rolled[:, :win], NEG_INF)


def _bias_tiles(rel_bias):
    n_heads = rel_bias.shape[0]
    t = jnp.arange(CA_ROLL_W)
    d = jnp.where(t < CA_WIN, t, t - CA_ROLL_W)
    idx = jnp.clip(LEFT_CHUNKS * CHUNK - d, -REL_CLIP, REL_CLIP) + REL_CLIP
    y = rel_bias[:, idx].reshape(n_heads, 1, CA_ROLL_W)
    return pl.pallas_call(
        _bias_kernel,
        grid=(n_heads,),
        in_specs=[pl.BlockSpec((1, 1, CA_ROLL_W), lambda h: (h, 0, 0))],
        out_specs=pl.BlockSpec((1, CA_TQ, CA_WIN), lambda h: (h, 0, 0)),
        out_shape=jax.ShapeDtypeStruct((n_heads, CA_TQ, CA_WIN), F32),
        compiler_params=_compiler_params(("arbitrary",)),
        name="bias",
    )(y)


def _proj_kernel(x_ref, mod_ref, tc_ref, ts_ref, gattn_ref, gql_ref, gkvl_ref, gv_ref,
                 win_ref, wq_ref, wkv_ref,
                 q_ref, k_ref, v_ref, cq_ref, ck_ref, cv_ref):
    x = x_ref[0]
    mod = mod_ref[0]
    sh, sc = mod[0:1, :], mod[1:2, :]
    y = x * lax.rsqrt(jnp.mean(x * x, axis=-1, keepdims=True) + EPS) * gattn_ref[...]
    h = (y * (1.0 + sc) + sh).astype(BF16)
    proj = _dot(h, win_ref[...])

    lane = lax.broadcasted_iota(jnp.int32, (1, LANES), 1)
    gv = gv_ref[...]
    g_q, g_kn, g_ka, g_kb, g_cq, g_ck = (gv[r:r + 1, :] for r in range(6))

    tc = tc_ref[0]
    ts = ts_ref[0]
    tq = jnp.where(lane < MLA_QK, tc, ts)

    ql = proj[:, _C_QLAT:_C_QLAT + MLA_Q_RANK]
    qn = (ql * lax.rsqrt(jnp.mean(ql * ql, axis=-1, keepdims=True) + EPS) * gql_ref[...]).astype(BF16)
    qp = _dot(qn, wq_ref[...])
    qmul = g_q * tq
    for hd in range(MLA_HEADS):
        qh = qp[:, hd * LANES:(hd + 1) * LANES]
        ss = jnp.sum(jnp.where(lane < MLA_QK, qh * qh, 0.0), axis=-1, keepdims=True)
        r = lax.rsqrt(ss * (1.0 / MLA_QK) + EPS)
        q_ref[0, :, hd * LANES:(hd + 1) * LANES] = (qh * r * qmul).astype(BF16)

    kvl = proj[:, _C_KVLAT:_C_KVLAT + MLA_KV_RANK]
    kvn = (kvl * lax.rsqrt(jnp.mean(kvl * kvl, axis=-1, keepdims=True) + EPS) * gkvl_ref[...]).astype(BF16)
    kvp = _dot(kvn, wkv_ref[...])
    ka = proj[:, _C_KA:_C_KA + LANES]
    kb = proj[:, _C_KB:_C_KB + LANES]
    krot = ka * g_ka * tc + kb * g_kb * ts
    ss_rope = jnp.sum(jnp.where(lane < MLA_QK, ka * ka, 0.0), axis=-1, keepdims=True)
    for hd in range(MLA_HEADS):
        kh = kvp[:, hd * LANES:(hd + 1) * LANES]
        ss = jnp.sum(kh * kh, axis=-1, keepdims=True) + ss_rope
        r = lax.rsqrt(ss * (1.0 / MLA_QK) + EPS)
        k_ref[0, :, hd * LANES:(hd + 1) * LANES] = ((kh * g_kn + krot) * r).astype(BF16)
    v_ref[0] = kvp[:, MLA_HEADS * LANES:].astype(BF16)

    lo = lane < CA_HEAD_DIM
    for src, gain, dst in ((_C_CQ, g_cq, cq_ref), (_C_CK, g_ck, ck_ref)):
        for p in range(CA_HEADS // 2):
            xx = proj[:, src + p * LANES:src + (p + 1) * LANES]
            x2 = xx * xx
            s_all = jnp.sum(x2, axis=-1, keepdims=True)
            s_lo = jnp.sum(jnp.where(lo, x2, 0.0), axis=-1, keepdims=True)
            r_lo = lax.rsqrt(s_lo * (1.0 / CA_HEAD_DIM) + EPS)
            r_hi = lax.rsqrt((s_all - s_lo) * (1.0 / CA_HEAD_DIM) + EPS)
            dst[0, :, p * LANES:(p + 1) * LANES] = (xx * jnp.where(lo, r_lo, r_hi) * gain).astype(BF16)
    cv_ref[0] = proj[:, _C_CV:].astype(BF16)


def _proj(x, mod, tc, ts, g_attn, g_ql, g_kvl, gvec, w_in_ext, w_q_ext, w_kv_ext):
    bsz, seq, _ = x.shape
    tm = PROJ_TM
    tok = lambda b, i: (b, i, 0)
    const2 = lambda b, i: (0, 0)
    wide = MLA_HEADS * LANES
    narrow = CA_HEADS * CA_HEAD_DIM
    out_shapes = [jax.ShapeDtypeStruct((bsz, seq, w), BF16) for w in (wide, wide, narrow, narrow, narrow, narrow)]
    out_specs = [pl.BlockSpec((1, tm, w), tok) for w in (wide, wide, narrow, narrow, narrow, narrow)]
    return pl.pallas_call(
        _proj_kernel,
        grid=(bsz, seq // tm),
        in_specs=[
            pl.BlockSpec((1, tm, D_MODEL), tok),
            pl.BlockSpec((1, N_MOD, D_MODEL), lambda b, i: (b, 0, 0)),
            pl.BlockSpec((1, tm, LANES), tok),
            pl.BlockSpec((1, tm, LANES), tok),
            pl.BlockSpec((1, D_MODEL), const2),
            pl.BlockSpec((1, MLA_Q_RANK), const2),
            pl.BlockSpec((1, MLA_KV_RANK), const2),
            pl.BlockSpec((8, LANES), const2),
            pl.BlockSpec((D_MODEL, D_IN_EXT), const2),
            pl.BlockSpec((MLA_Q_RANK, wide), const2),
            pl.BlockSpec((MLA_KV_RANK, wide + MLA_HEADS * MLA_V), const2),
        ],
        out_specs=out_specs,
        out_shape=out_shapes,
        compiler_params=_compiler_params(("parallel", "parallel")),
        name="proj",
    )(x, mod, tc, ts, g_attn, g_ql, g_kvl, gvec, w_in_ext, w_q_ext, w_kv_ext)


def _softmax_pv(score_parts, value_parts):
    m = score_parts[0].max(axis=-1, keepdims=True)
    for s in score_parts[1:]:
        m = jnp.maximum(m, s.max(axis=-1, keepdims=True))
    l = None
    acc = None
    for s, vb in zip(score_parts, value_parts):
        p = jnp.exp(s - m)
        ps = jnp.sum(p, axis=-1, keepdims=True)
        pv = _dot(p.astype(BF16), vb)
        l = ps if l is None else l + ps
        acc = pv if acc is None else acc + pv
    return acc / l


def _mla_kernel(q_ref, k_ref, v_ref, o_ref):
    t = MLA_TQ
    seq = q_ref.shape[1]
    lane = lax.broadcasted_iota(jnp.int32, (1, LANES), 1)
    row_chunk = lax.broadcasted_iota(jnp.int32, (t, 1), 0) // CHUNK
    col_chunk = lax.broadcasted_iota(jnp.int32, (1, t), 1) // CHUNK
    diag_mask = col_chunk <= row_chunk

    for i in range(seq // t):
        lo = i * t
        outs = []
        for hh in range(2):
            hs = slice(hh * LANES, (hh + 1) * LANES)
            q = q_ref[0, lo:lo + t, hs]
            s_diag = jnp.where(diag_mask, _dot_nt(q, k_ref[0, lo:lo + t, hs]), NEG_INF)
            parts, vals = [s_diag], [v_ref[0, lo:lo + t, :]]
            if i > 0:
                parts.insert(0, _dot_nt(q, k_ref[0, 0:lo, hs]))
                vals.insert(0, v_ref[0, 0:lo, :])
            outs.append(_softmax_pv(parts, vals))
        o_ref[0, lo:lo + t, :] = jnp.where(lane < MLA_V, outs[0], outs[1]).astype(BF16)


def _mla(q, k, v):
    bsz, seq, _ = q.shape
    pairs = MLA_HEADS // 2
    return pl.pallas_call(
        _mla_kernel,
        grid=(bsz, pairs),
        in_specs=[
            pl.BlockSpec((1, seq, 2 * LANES), lambda b, p: (b, 0, p)),
            pl.BlockSpec((1, seq, 2 * LANES), lambda b, p: (b, 0, p)),
            pl.BlockSpec((1, seq, LANES), lambda b, p: (b, 0, p)),
        ],
        out_specs=pl.BlockSpec((1, seq, LANES), lambda b, p: (b, 0, p)),
        out_shape=jax.ShapeDtypeStruct((bsz, seq, MLA_HEADS * MLA_V), BF16),
        compiler_params=_compiler_params(("parallel", "parallel")),
        name="mla",
    )(q, k, v)


def _ca_kernel(q_ref, k_ref, v_ref, bm_ref, o_ref):
    tq, win = CA_TQ, CA_WIN
    seq = q_ref.shape[1]
    lane = lax.broadcasted_iota(jnp.int32, (1, LANES), 1)
    for i in range(seq // tq):
        lo = i * tq
        k_lo = max(lo + tq - win, 0)
        c_lo = k_lo - (lo + tq - win)
        q = q_ref[0, lo:lo + tq, :]
        kwin = k_ref[0, k_lo:lo + tq, :]
        vwin = v_ref[0, k_lo:lo + tq, :]
        outs = []
        for hh in range(2):
            head_lanes = (lane < CA_HEAD_DIM) if hh == 0 else (lane >= CA_HEAD_DIM)
            qm = jnp.where(head_lanes, q, jnp.zeros_like(q))
            s = _dot_nt(qm, kwin) + bm_ref[hh, :, c_lo:]
            outs.append(_softmax_pv([s], [vwin]))
        o_ref[0, lo:lo + tq, :] = jnp.where(lane < CA_HEAD_DIM, outs[0], outs[1]).astype(BF16)


def _ca(cq, ck, cv, bias_mask):
    bsz, seq, _ = cq.shape
    pairs = CA_HEADS // 2
    tok = lambda b, p: (b, 0, p)
    return pl.pallas_call(
        _ca_kernel,
        grid=(bsz, pairs),
        in_specs=[
            pl.BlockSpec((1, seq, LANES), tok),
            pl.BlockSpec((1, seq, LANES), tok),
            pl.BlockSpec((1, seq, LANES), tok),
            pl.BlockSpec((2, CA_TQ, CA_WIN), lambda b, p: (p, 0, 0)),
        ],
        out_specs=pl.BlockSpec((1, seq, LANES), tok),
        out_shape=jax.ShapeDtypeStruct((bsz, seq, CA_HEADS * CA_HEAD_DIM), BF16),
        compiler_params=_compiler_params(("parallel", "parallel")),
        name="ca",
    )(cq, ck, cv, bias_mask)


def _final_kernel(x_ref, xh_ref, om_ref, omh_ref, oc_ref, och_ref, mod_ref, gmlp_ref,
                  wo_ref, wug_ref, wuv_ref, cwg_ref, cwv_ref, cbg_ref, cbv_ref, wd_ref,
                  out_ref, h2_sc, acc_sc):
    i = pl.program_id(1)
    tm = FIN_TM
    halo = FIN_HALO
    mod = mod_ref[0]
    g_a, sh_m, sc_m, g_m = mod[2:3, :], mod[3:4, :], mod[4:5, :], mod[5:6, :]

    x_ext = jnp.concatenate([xh_ref[0], x_ref[0]], axis=0)
    o_ext = jnp.concatenate(
        [jnp.concatenate([omh_ref[0], och_ref[0]], axis=-1), jnp.concatenate([om_ref[0], oc_ref[0]], axis=-1)], axis=0)
    x1 = x_ext + g_a * _dot(o_ext, wo_ref[...])
    y = x1 * lax.rsqrt(jnp.mean(x1 * x1, axis=-1, keepdims=True) + EPS) * gmlp_ref[...]
    h2_sc[...] = (y * (1.0 + sc_m) + sh_m).astype(BF16)
    acc_sc[...] = jnp.zeros_like(acc_sc)

    halo_keep = i > 0

    def conv(u, cw, cb):
        u = jnp.concatenate([jnp.where(halo_keep, u[:halo], 0.0), u[halo:]], axis=0)
        return (u[halo - 2:halo - 2 + tm] * cw[0:1, :] + u[halo - 1:halo - 1 + tm] * cw[1:2, :]
                + u[halo:] * cw[2:3, :] + cb)

    h2 = h2_sc[...]
    for c in range(N_FF_CHUNKS):
        gate = conv(_dot(h2, wug_ref[c]), cwg_ref[c], cbg_ref[c])
        val = conv(_dot(h2, wuv_ref[c]), cwv_ref[c], cbv_ref[c])
        act = (gate / (1.0 + jnp.exp(-gate)) * val).astype(BF16)
        acc_sc[...] += _dot(act, wd_ref[c])
    out_ref[0] = x1[halo:] + g_m * acc_sc[...]


def _final(x, o_mla, o_ca, mod, g_mlp, w_out, wug, wuv, cwg, cwv, cbg, cbv, wd):
    bsz, seq, _ = x.shape
    tm, halo = FIN_TM, FIN_HALO
    tok = lambda b, i: (b, i, 0)
    prev = lambda b, i: (b, jnp.maximum(i * (tm // halo) - 1, 0), 0)
    const2 = lambda b, i: (0, 0)
    const3 = lambda b, i: (0, 0, 0)
    d_mix = o_mla.shape[-1]
    resident = dict(pipeline_mode=pl.Buffered(1))
    return pl.pallas_call(
        _final_kernel,
        grid=(bsz, seq // tm),
        in_specs=[
            pl.BlockSpec((1, tm, D_MODEL), tok),
            pl.BlockSpec((1, halo, D_MODEL), prev),
            pl.BlockSpec((1, tm, d_mix), tok),
            pl.BlockSpec((1, halo, d_mix), prev),
            pl.BlockSpec((1, tm, d_mix), tok),
            pl.BlockSpec((1, halo, d_mix), prev),
            pl.BlockSpec((1, N_MOD, D_MODEL), lambda b, i: (b, 0, 0)),
            pl.BlockSpec((1, D_MODEL), const2),
            pl.BlockSpec((2 * d_mix, D_MODEL), const2, **resident),
            pl.BlockSpec((N_FF_CHUNKS, D_MODEL, FF_CHUNK), const3, **resident),
            pl.BlockSpec((N_FF_CHUNKS, D_MODEL, FF_CHUNK), const3, **resident),
            pl.BlockSpec((N_FF_CHUNKS, 3, FF_CHUNK), const3),
            pl.BlockSpec((N_FF_CHUNKS, 3, FF_CHUNK), const3),
            pl.BlockSpec((N_FF_CHUNKS, 1, FF_CHUNK), const3),
            pl.BlockSpec((N_FF_CHUNKS, 1, FF_CHUNK), const3),
            pl.BlockSpec((N_FF_CHUNKS, FF_CHUNK, D_MODEL), const3, **resident),
        ],
        out_specs=pl.BlockSpec((1, tm, D_MODEL), tok),
        out_shape=jax.ShapeDtypeStruct((bsz, seq, D_MODEL), F32),
        scratch_shapes=[pltpu.VMEM((halo + tm, D_MODEL), BF16), pltpu.VMEM((tm, D_MODEL), F32)],
        compiler_params=_compiler_params(("parallel", "arbitrary")),
        name="final",
    )(x, x, o_mla, o_mla, o_ca, o_ca, mod, g_mlp, w_out, wug, wuv, cwg, cwv, cbg, cbv, wd)


def _prep_layer(w_in, w_q_up, w_kv_up, g_mla_q, g_mla_k, g_ca_q, g_ca_k, w_up, conv_w, conv_b, w_down):
    half = MLA_ROPE // 2
    c0 = MLA_Q_RANK + MLA_KV_RANK
    k1 = w_in[:, c0:c0 + half]
    k2 = w_in[:, c0 + half:c0 + MLA_ROPE]
    z64 = jnp.zeros((D_MODEL, MLA_NOPE), w_in.dtype)
    w_in_ext = jnp.concatenate(
        [w_in[:, :c0], z64, k1, k2, k1, k2, z64, k2, k1, k2, k1, w_in[:, c0 + MLA_ROPE:]], axis=1).astype(BF16)

    wq = w_q_up.reshape(MLA_Q_RANK, MLA_HEADS, MLA_QK)
    x1 = wq[..., MLA_NOPE:MLA_NOPE + half]
    x2 = wq[..., MLA_NOPE + half:]
    w_q_ext = jnp.concatenate([wq, x2, x1], axis=-1).reshape(MLA_Q_RANK, MLA_HEADS * LANES).astype(BF16)

    wkv = w_kv_up.reshape(MLA_KV_RANK, MLA_HEADS, MLA_NOPE + MLA_V)
    wk = jnp.concatenate([wkv[..., :MLA_NOPE], jnp.zeros((MLA_KV_RANK, MLA_HEADS, LANES - MLA_NOPE), wkv.dtype)], axis=-1)
    w_kv_ext = jnp.concatenate(
        [wk.reshape(MLA_KV_RANK, MLA_HEADS * LANES), wkv[..., MLA_NOPE:].reshape(MLA_KV_RANK, MLA_HEADS * MLA_V)],
        axis=1).astype(BF16)

    gq1, gq2 = g_mla_q[MLA_NOPE:MLA_NOPE + half], g_mla_q[MLA_NOPE + half:]
    gk1, gk2 = g_mla_k[MLA_NOPE:MLA_NOPE + half], g_mla_k[MLA_NOPE + half:]
    z = jnp.zeros((MLA_NOPE,), F32)
    rows = [
        jnp.concatenate([g_mla_q, gq2, gq1]) * (MLA_QK ** -0.5),
        jnp.concatenate([g_mla_k[:MLA_NOPE], z]),
        jnp.concatenate([z, gk1, gk2, gk1, gk2]),
        jnp.concatenate([z, gk2, gk1, gk2, gk1]),
        jnp.concatenate([g_ca_q, g_ca_q]) * (CA_HEAD_DIM ** -0.5),
        jnp.concatenate([g_ca_k, g_ca_k]),
        jnp.zeros((LANES,), F32),
        jnp.zeros((LANES,), F32),
    ]
    gvec = jnp.stack(rows)

    def chunk_cols(w):
        return w.reshape(w.shape[0], N_FF_CHUNKS, FF_CHUNK).transpose(1, 0, 2)

    wug = chunk_cols(w_up[:, :D_FF]).astype(BF16)
    wuv = chunk_cols(w_up[:, D_FF:]).astype(BF16)
    cwg = chunk_cols(conv_w[:, :D_FF])
    cwv = chunk_cols(conv_w[:, D_FF:])
    cbg = conv_b[:D_FF].reshape(N_FF_CHUNKS, 1, FF_CHUNK)
    cbv = conv_b[D_FF:].reshape(N_FF_CHUNKS, 1, FF_CHUNK)
    wd = w_down.reshape(N_FF_CHUNKS, FF_CHUNK, D_MODEL).astype(BF16)
    return w_in_ext, w_q_ext, w_kv_ext, gvec, wug, wuv, cwg, cwv, cbg, cbv, wd


def kernel(x, c, positions, w_ada, b_ada, g_attn_norm, w_in, g_q_latent, g_kv_latent, w_q_up, w_kv_up, g_mla_q, g_mla_k, g_ca_q, g_ca_k, rel_bias, w_out, g_mlp_norm, w_up, conv_w, conv_b, w_down):
    bsz, seq, _ = x.shape
    depth = w_ada.shape[0]

    cos, sin = _rope_tables(positions)
    ones = jnp.ones((bsz * seq, MLA_NOPE), F32)
    tc = jnp.concatenate([ones, cos, cos, cos, cos], axis=1).reshape(bsz, seq, LANES)
    ts = jnp.concatenate([jnp.zeros_like(ones), -sin, sin, -sin, sin], axis=1).reshape(bsz, seq, LANES)

    for l in range(depth):
        (w_in_ext, w_q_ext, w_kv_ext, gvec, wug, wuv, cwg, cwv, cbg, cbv, wd) = _prep_layer(
            w_in[l], w_q_up[l], w_kv_up[l], g_mla_q[l], g_mla_k[l], g_ca_q[l], g_ca_k[l],
            w_up[l], conv_w[l], conv_b[l], w_down[l])
        mod = _ada(c, w_ada[l], b_ada[l]).reshape(bsz, N_MOD, D_MODEL)
        bias_mask = _bias_tiles(rel_bias[l])
        q, k, v, cq, ck, cv = _proj(
            x, mod, tc, ts, g_attn_norm[l].reshape(1, -1), g_q_latent[l].reshape(1, -1),
            g_kv_latent[l].reshape(1, -1), gvec, w_in_ext, w_q_ext, w_kv_ext)
        o_mla = _mla(q, k, v)
        o_ca = _ca(cq, ck, cv, bias_mask)
        x = _final(x, o_mla, o_ca, mod, g_mlp_norm[l].reshape(1, -1), w_out[l].astype(BF16),
                   wug, wuv, cwg, cwv, cbg, cbv, wd)
    return x
```

```python
import math

import numpy as np
import jax
import jax.numpy as jnp
from jax import lax
from jax.experimental import pallas as pl
from jax.experimental.pallas import tpu as pltpu

D_MODEL = 1024
CHUNK = 64
LEFT_CHUNKS = 8
MLA_HEADS = 8
MLA_Q_RANK = 256
MLA_KV_RANK = 128
MLA_NOPE = 64
MLA_ROPE = 32
MLA_QK = MLA_NOPE + MLA_ROPE
MLA_V = 64
ROPE_THETA = 10000.0
CA_HEADS = 8
CA_HEAD_DIM = 64
REL_CLIP = 128
D_FF = 2816
N_MOD = 6
EPS = 1e-6
NEG_INF = -1e30
LOG2E = math.log2(math.e)

LANES = 128
BF16_SUBLANES = 16
VMEM_LIMIT_BYTES = 56 * 1024 * 1024

PROJ_TM = 512
MLA_TQ = 256
CA_TQ = 256
CA_WIN = CA_TQ + LEFT_CHUNKS * CHUNK
CA_ROLL_W = 1024
FIN_TM = 512
FIN_HALO = BF16_SUBLANES
FIN_ROW_BLOCK = 128
FF_CHUNK = 256
N_FF_CHUNKS = D_FF // FF_CHUNK

_C_QLAT = 0
_C_KVLAT = _C_QLAT + MLA_Q_RANK
_C_KA = _C_KVLAT + MLA_KV_RANK
_C_KB = _C_KA + LANES
_C_CQ = _C_KB + LANES
_C_CK = _C_CQ + CA_HEADS * CA_HEAD_DIM
_C_CV = _C_CK + CA_HEADS * CA_HEAD_DIM
D_IN_EXT = _C_CV + CA_HEADS * CA_HEAD_DIM

F32 = jnp.float32
BF16 = jnp.bfloat16


def _dot(a, b):
    return jnp.dot(a, b, preferred_element_type=F32)


def _dot_nt(a, b):
    return lax.dot_general(a, b, (((1,), (1,)), ((), ())), preferred_element_type=F32)


def _compiler_params(semantics):
    return pltpu.CompilerParams(dimension_semantics=semantics, vmem_limit_bytes=VMEM_LIMIT_BYTES)


def _ada_kernel(c_ref, w_ref, b_ref, o_ref):
    c = c_ref[...]
    s = c / (1.0 + jnp.exp(-c))
    o_ref[...] = _dot(s.astype(BF16), w_ref[...].astype(BF16)) + b_ref[...]


def _ada(c, w_ada, b_ada):
    bsz = c.shape[0]
    n_out = w_ada.shape[1]
    tn = D_MODEL
    return pl.pallas_call(
        _ada_kernel,
        grid=(n_out // tn,),
        in_specs=[
            pl.BlockSpec((bsz, D_MODEL), lambda j: (0, 0)),
            pl.BlockSpec((D_MODEL, tn), lambda j: (0, j)),
            pl.BlockSpec((1, tn), lambda j: (0, j)),
        ],
        out_specs=pl.BlockSpec((bsz, tn), lambda j: (0, j)),
        out_shape=jax.ShapeDtypeStruct((bsz, n_out), F32),
        compiler_params=_compiler_params(("arbitrary",)),
        name="ada",
    )(c, w_ada, b_ada.reshape(1, n_out))


def _expand(d, e_ref):
    hi = d.astype(BF16)
    r1 = d - hi.astype(F32)
    mid = r1.astype(BF16)
    lo = (r1 - mid.astype(F32)).astype(BF16)
    e = e_ref[...]
    return _dot(hi, e) + _dot(mid, e) + _dot(lo, e)


def _rope_kernel(pos_ref, inv_ref, ec_ref, es_ref, base_ref, tc_ref, ts_ref):
    ang = pos_ref[...].astype(F32) * inv_ref[...]
    tc_ref[...] = _expand(jnp.cos(ang), ec_ref) + base_ref[...]
    ts_ref[...] = _expand(jnp.sin(ang), es_ref)


def _rope_tables(positions):
    half = MLA_ROPE // 2
    bsz, seq = positions.shape
    n_tok = bsz * seq
    tok_per_row = LANES // half
    rows = n_tok // tok_per_row
    inv = jnp.power(ROPE_THETA, -jnp.arange(half, dtype=F32) / half)
    inv_t = jnp.tile(inv, tok_per_row).reshape(1, LANES)
    pos_rep = jnp.repeat(positions.reshape(n_tok, 1), half, axis=1).reshape(rows, LANES)

    src = np.arange(LANES)
    sel_c = np.zeros((LANES, tok_per_row * LANES), np.float32)
    sel_s = np.zeros((LANES, tok_per_row * LANES), np.float32)
    for rep, sign in enumerate((-1.0, 1.0, -1.0, 1.0)):
        dst = (src // half) * LANES + MLA_NOPE + rep * half + src % half
        sel_c[src, dst] = 1.0
        sel_s[src, dst] = sign
    base = np.tile((np.arange(LANES) < MLA_NOPE).astype(np.float32), tok_per_row).reshape(1, -1)

    tr = 512
    wide = tok_per_row * LANES
    const = lambda i: (0, 0)
    tc, ts = pl.pallas_call(
        _rope_kernel,
        grid=(rows // tr,),
        in_specs=[
            pl.BlockSpec((tr, LANES), lambda i: (i, 0)),
            pl.BlockSpec((1, LANES), const),
            pl.BlockSpec((LANES, wide), const),
            pl.BlockSpec((LANES, wide), const),
            pl.BlockSpec((1, wide), const),
        ],
        out_specs=[pl.BlockSpec((tr, wide), lambda i: (i, 0))] * 2,
        out_shape=[jax.ShapeDtypeStruct((rows, wide), F32)] * 2,
        compiler_params=_compiler_params(("arbitrary",)),
        name="rope",
    )(pos_rep, inv_t, jnp.asarray(sel_c, BF16), jnp.asarray(sel_s, BF16), jnp.asarray(base))
    return tc.reshape(bsz, seq, LANES), ts.reshape(bsz, seq, LANES)


def _bias_kernel(y_ref, o_ref):
    tq, win = CA_TQ, CA_WIN
    full = jnp.broadcast_to(y_ref[0], (tq, CA_ROLL_W))
    rolled = pltpu.roll(full, 0, 1, stride=1, stride_axis=0)
    q_chunk = lax.broadcasted_iota(jnp.int32, (tq, 1), 0) // CHUNK
    k_chunk = lax.broadcasted_iota(jnp.int32, (1, win), 1) // CHUNK
    valid = jnp.logical_and(k_chunk >= q_chunk, k_chunk <= q_chunk + LEFT_CHUNKS)
    o_ref[0] = jnp.where(valid, rolled[:, :win] * LOG2E, NEG_INF)


def _bias_tiles(rel_bias):
    n_heads = rel_bias.shape[0]
    t = jnp.arange(CA_ROLL_W)
    d = jnp.where(t < CA_WIN, t, t - CA_ROLL_W)
    idx = jnp.clip(LEFT_CHUNKS * CHUNK - d, -REL_CLIP, REL_CLIP) + REL_CLIP
    y = rel_bias[:, idx].reshape(n_heads, 1, CA_ROLL_W)
    return pl.pallas_call(
        _bias_kernel,
        grid=(n_heads,),
        in_specs=[pl.BlockSpec((1, 1, CA_ROLL_W), lambda h: (h, 0, 0))],
        out_specs=pl.BlockSpec((1, CA_TQ, CA_WIN), lambda h: (h, 0, 0)),
        out_shape=jax.ShapeDtypeStruct((n_heads, CA_TQ, CA_WIN), F32),
        compiler_params=_compiler_params(("arbitrary",)),
        name="bias",
    )(y)


def _proj_kernel(x_ref, mod_ref, tc_ref, ts_ref, gattn_ref, gql_ref, gkvl_ref, gv_ref,
                 win_ref, wq_ref, wkv_ref,
                 q_ref, k_ref, v_ref, cq_ref, ck_ref, cv_ref):
    x = x_ref[0]
    mod = mod_ref[0]
    sh, sc = mod[0:1, :], mod[1:2, :]
    y = x * lax.rsqrt(jnp.mean(x * x, axis=-1, keepdims=True) + EPS) * gattn_ref[...]
    h = (y * (1.0 + sc) + sh).astype(BF16)
    proj = _dot(h, win_ref[...])

    lane = lax.broadcasted_iota(jnp.int32, (1, LANES), 1)
    gv = gv_ref[...]
    g_q, g_kn, g_ka, g_kb, g_cq, g_ck = (gv[r:r + 1, :] for r in range(6))

    tc = tc_ref[0]
    ts = ts_ref[0]
    tq = jnp.where(lane < MLA_QK, tc, ts)

    ql = proj[:, _C_QLAT:_C_QLAT + MLA_Q_RANK]
    qn = (ql * lax.rsqrt(jnp.mean(ql * ql, axis=-1, keepdims=True) + EPS) * gql_ref[...]).astype(BF16)
    qp = _dot(qn, wq_ref[...])
    qmul = g_q * tq
    for hd in range(MLA_HEADS):
        qh = qp[:, hd * LANES:(hd + 1) * LANES]
        ss = jnp.sum(jnp.where(lane < MLA_QK, qh * qh, 0.0), axis=-1, keepdims=True)
        r = lax.rsqrt(ss * (1.0 / MLA_QK) + EPS)
        q_ref[0, :, hd * LANES:(hd + 1) * LANES] = (qh * r * qmul).astype(BF16)

    kvl = proj[:, _C_KVLAT:_C_KVLAT + MLA_KV_RANK]
    kvn = (kvl * lax.rsqrt(jnp.mean(kvl * kvl, axis=-1, keepdims=True) + EPS) * gkvl_ref[...]).astype(BF16)
    kvp = _dot(kvn, wkv_ref[...])
    ka = proj[:, _C_KA:_C_KA + LANES]
    kb = proj[:, _C_KB:_C_KB + LANES]
    krot = ka * g_ka * tc + kb * g_kb * ts
    ss_rope = jnp.sum(jnp.where(lane < MLA_QK, ka * ka, 0.0), axis=-1, keepdims=True)
    for hd in range(MLA_HEADS):
        kh = kvp[:, hd * LANES:(hd + 1) * LANES]
        ss = jnp.sum(kh * kh, axis=-1, keepdims=True) + ss_rope
        r = lax.rsqrt(ss * (1.0 / MLA_QK) + EPS)
        k_ref[0, :, hd * LANES:(hd + 1) * LANES] = ((kh * g_kn + krot) * r).astype(BF16)
    v_ref[0] = kvp[:, MLA_HEADS * LANES:].astype(BF16)

    lo = lane < CA_HEAD_DIM
    for src, gain, dst in ((_C_CQ, g_cq, cq_ref), (_C_CK, g_ck, ck_ref)):
        for p in range(CA_HEADS // 2):
            xx = proj[:, src + p * LANES:src + (p + 1) * LANES]
            x2 = xx * xx
            s_all = jnp.sum(x2, axis=-1, keepdims=True)
            s_lo = jnp.sum(jnp.where(lo, x2, 0.0), axis=-1, keepdims=True)
            r_lo = lax.rsqrt(s_lo * (1.0 / CA_HEAD_DIM) + EPS)
            r_hi = lax.rsqrt((s_all - s_lo) * (1.0 / CA_HEAD_DIM) + EPS)
            dst[0, :, p * LANES:(p + 1) * LANES] = (xx * jnp.where(lo, r_lo, r_hi) * gain).astype(BF16)
    cv_ref[0] = proj[:, _C_CV:].astype(BF16)


def _proj(x, mod, tc, ts, g_attn, g_ql, g_kvl, gvec, w_in_ext, w_q_ext, w_kv_ext):
    bsz, seq, _ = x.shape
    tm = PROJ_TM
    tok = lambda b, i: (b, i, 0)
    const2 = lambda b, i: (0, 0)
    wide = MLA_HEADS * LANES
    narrow = CA_HEADS * CA_HEAD_DIM
    out_shapes = [jax.ShapeDtypeStruct((bsz, seq, w), BF16) for w in (wide, wide, narrow, narrow, narrow, narrow)]
    out_specs = [pl.BlockSpec((1, tm, w), tok) for w in (wide, wide, narrow, narrow, narrow, narrow)]
    return pl.pallas_call(
        _proj_kernel,
        grid=(bsz, seq // tm),
        in_specs=[
            pl.BlockSpec((1, tm, D_MODEL), tok),
            pl.BlockSpec((1, N_MOD, D_MODEL), lambda b, i: (b, 0, 0)),
            pl.BlockSpec((1, tm, LANES), tok),
            pl.BlockSpec((1, tm, LANES), tok),
            pl.BlockSpec((1, D_MODEL), const2),
            pl.BlockSpec((1, MLA_Q_RANK), const2),
            pl.BlockSpec((1, MLA_KV_RANK), const2),
            pl.BlockSpec((8, LANES), const2),
            pl.BlockSpec((D_MODEL, D_IN_EXT), const2),
            pl.BlockSpec((MLA_Q_RANK, wide), const2),
            pl.BlockSpec((MLA_KV_RANK, wide + MLA_HEADS * MLA_V), const2),
        ],
        out_specs=out_specs,
        out_shape=out_shapes,
        compiler_params=_compiler_params(("parallel", "parallel")),
        name="proj",
    )(x, mod, tc, ts, g_attn, g_ql, g_kvl, gvec, w_in_ext, w_q_ext, w_kv_ext)


def _softmax_pv(score_parts, value_parts):
    m = score_parts[0].max(axis=-1, keepdims=True)
    for s in score_parts[1:]:
        m = jnp.maximum(m, s.max(axis=-1, keepdims=True))
    l = None
    acc = None
    for s, vb in zip(score_parts, value_parts):
        p = jnp.exp2(s - m)
        ps = jnp.sum(p, axis=-1, keepdims=True)
        pv = _dot(p.astype(BF16), vb)
        l = ps if l is None else l + ps
        acc = pv if acc is None else acc + pv
    return acc / l


def _mla_kernel(q_ref, k_ref, v_ref, o_ref):
    t = MLA_TQ
    seq = q_ref.shape[1]
    lane = lax.broadcasted_iota(jnp.int32, (1, LANES), 1)
    row_chunk = lax.broadcasted_iota(jnp.int32, (t, 1), 0) // CHUNK
    col_chunk = lax.broadcasted_iota(jnp.int32, (1, t), 1) // CHUNK
    diag_mask = col_chunk <= row_chunk

    def scores(i, hh):
        lo = i * t
        hs = slice(hh * LANES, (hh + 1) * LANES)
        q = q_ref[0, lo:lo + t, hs]
        parts = [jnp.where(diag_mask, _dot_nt(q, k_ref[0, lo:lo + t, hs]), NEG_INF)]
        if i > 0:
            parts.insert(0, _dot_nt(q, k_ref[0, 0:lo, hs]))
        return parts

    def finish(i, parts):
        lo = i * t
        vals = [v_ref[0, lo:lo + t, :]]
        if i > 0:
            vals.insert(0, v_ref[0, 0:lo, :])
        return _softmax_pv(parts, vals)

    units = [(i, hh) for i in range(seq // t) for hh in range(2)]
    pending = scores(*units[0])
    outs = []
    for n, (i, hh) in enumerate(units):
        parts = pending
        if n + 1 < len(units):
            pending = scores(*units[n + 1])
        outs.append(finish(i, parts))
        if hh == 1:
            o_ref[0, i * t:(i + 1) * t, :] = jnp.where(lane < MLA_V, outs[0], outs[1]).astype(BF16)
            outs = []


def _mla(q, k, v):
    bsz, seq, _ = q.shape
    pairs = MLA_HEADS // 2
    return pl.pallas_call(
        _mla_kernel,
        grid=(bsz, pairs),
        in_specs=[
            pl.BlockSpec((1, seq, 2 * LANES), lambda b, p: (b, 0, p)),
            pl.BlockSpec((1, seq, 2 * LANES), lambda b, p: (b, 0, p)),
            pl.BlockSpec((1, seq, LANES), lambda b, p: (b, 0, p)),
        ],
        out_specs=pl.BlockSpec((1, seq, LANES), lambda b, p: (b, 0, p)),
        out_shape=jax.ShapeDtypeStruct((bsz, seq, MLA_HEADS * MLA_V), BF16),
        compiler_params=_compiler_params(("parallel", "parallel")),
        name="mla",
    )(q, k, v)


def _ca_kernel(q_ref, k_ref, v_ref, bm_ref, o_ref):
    tq, win = CA_TQ, CA_WIN
    seq = q_ref.shape[1]
    lane = lax.broadcasted_iota(jnp.int32, (1, LANES), 1)
    def window(i):
        hi = (i + 1) * tq
        k_lo = max(hi - win, 0)
        return k_lo, hi, k_lo - (hi - win)

    def scores(i, hh):
        k_lo, hi, c_lo = window(i)
        q = q_ref[0, i * tq:hi, :]
        head_lanes = (lane < CA_HEAD_DIM) if hh == 0 else (lane >= CA_HEAD_DIM)
        qm = jnp.where(head_lanes, q, jnp.zeros_like(q))
        return [_dot_nt(qm, k_ref[0, k_lo:hi, :]) + bm_ref[hh, :, c_lo:]]

    def finish(i, parts):
        k_lo, hi, _ = window(i)
        return _softmax_pv(parts, [v_ref[0, k_lo:hi, :]])

    units = [(i, hh) for i in range(seq // tq) for hh in range(2)]
    pending = scores(*units[0])
    outs = []
    for n, (i, hh) in enumerate(units):
        parts = pending
        if n + 1 < len(units):
            pending = scores(*units[n + 1])
        outs.append(finish(i, parts))
        if hh == 1:
            o_ref[0, i * tq:(i + 1) * tq, :] = jnp.where(lane < CA_HEAD_DIM, outs[0], outs[1]).astype(BF16)
            outs = []


def _ca(cq, ck, cv, bias_mask):
    bsz, seq, _ = cq.shape
    pairs = CA_HEADS // 2
    tok = lambda b, p: (b, 0, p)
    return pl.pallas_call(
        _ca_kernel,
        grid=(bsz, pairs),
        in_specs=[
            pl.BlockSpec((1, seq, LANES), tok),
            pl.BlockSpec((1, seq, LANES), tok),
            pl.BlockSpec((1, seq, LANES), tok),
            pl.BlockSpec((2, CA_TQ, CA_WIN), lambda b, p: (p, 0, 0)),
        ],
        out_specs=pl.BlockSpec((1, seq, LANES), tok),
        out_shape=jax.ShapeDtypeStruct((bsz, seq, CA_HEADS * CA_HEAD_DIM), BF16),
        compiler_params=_compiler_params(("parallel", "parallel")),
        name="ca",
    )(cq, ck, cv, bias_mask)


def _final_kernel(x_ref, xh_ref, om_ref, omh_ref, oc_ref, och_ref, mod_ref, gmlp_ref,
                  wo_ref, wu_ref, cw_ref, cb_ref, wd_ref,
                  out_ref, h2_sc, ug0_sc, ug1_sc, uv0_sc, uv1_sc, acc_sc):
    i = pl.program_id(1)
    tm = FIN_TM
    halo = FIN_HALO
    mod = mod_ref[0]
    g_a, sh_m, sc_m, g_m = mod[2:3, :], mod[3:4, :], mod[4:5, :], mod[5:6, :]

    x_ext = jnp.concatenate([xh_ref[0], x_ref[0]], axis=0)
    o_ext = jnp.concatenate(
        [jnp.concatenate([omh_ref[0], och_ref[0]], axis=-1), jnp.concatenate([om_ref[0], oc_ref[0]], axis=-1)], axis=0)
    x1 = x_ext + g_a * _dot(o_ext, wo_ref[...])
    out_ref[0] = x1[halo:]
    y = x1 * lax.rsqrt(jnp.mean(x1 * x1, axis=-1, keepdims=True) + EPS) * gmlp_ref[...]
    h2 = y * (1.0 + sc_m) + sh_m
    h2_sc[:halo] = jnp.where(i > 0, h2[:halo], 0.0).astype(BF16)
    h2_sc[halo:] = h2[halo:].astype(BF16)

    u_bufs = ((ug0_sc, uv0_sc), (ug1_sc, uv1_sc))
    rb = FIN_ROW_BLOCK
    n_rb = tm // rb

    def up_rows(r):
        return (0, halo + rb) if r == 0 else (halo + r * rb, halo + (r + 1) * rb)

    def up_project(c, r):
        ug_sc, uv_sc = u_bufs[c % 2]
        lo, hi = up_rows(r)
        h2b = h2_sc[lo:hi, :]
        ug_sc[lo:hi, :] = _dot(h2b, wu_ref[:, c * FF_CHUNK:(c + 1) * FF_CHUNK])
        uv_sc[lo:hi, :] = _dot(h2b, wu_ref[:, D_FF + c * FF_CHUNK:D_FF + (c + 1) * FF_CHUNK])

    def conv(u_sc, col, r):
        cw = cw_ref[:, col:col + FF_CHUNK]
        lo = halo + r * rb
        return (u_sc[lo - 2:lo - 2 + rb, :] * cw[0:1, :] + u_sc[lo - 1:lo - 1 + rb, :] * cw[1:2, :]
                + u_sc[lo:lo + rb, :] * cw[2:3, :] + cb_ref[:, col:col + FF_CHUNK])

    def mlp_rows(c, r):
        ug_sc, uv_sc = u_bufs[c % 2]
        gcol, vcol = c * FF_CHUNK, D_FF + c * FF_CHUNK
        gate = conv(ug_sc, gcol, r)
        val = conv(uv_sc, vcol, r)
        act = (gate / (1.0 + jnp.exp(-gate)) * val).astype(BF16)
        down = _dot(act, wd_ref[gcol:gcol + FF_CHUNK, :])
        rows = slice(r * rb, (r + 1) * rb)
        if c == 0:
            acc_sc[rows, :] = down
        elif c < N_FF_CHUNKS - 1:
            acc_sc[rows, :] += down
        else:
            out_ref[0, rows, :] += g_m * (acc_sc[rows, :] + down)

    for r in range(n_rb):
        up_project(0, r)
    for c in range(N_FF_CHUNKS):
        for r in range(n_rb):
            if c + 1 < N_FF_CHUNKS:
                up_project(c + 1, r)
            mlp_rows(c, r)


def _final(x, o_mla, o_ca, mod, g_mlp, w_out, w_up, conv_w, conv_b, w_down):
    bsz, seq, _ = x.shape
    tm, halo = FIN_TM, FIN_HALO
    tok = lambda b, i: (b, i, 0)
    prev = lambda b, i: (b, jnp.maximum(i * (tm // halo) - 1, 0), 0)
    const2 = lambda b, i: (0, 0)
    d_mix = o_mla.shape[-1]
    resident = dict(pipeline_mode=pl.Buffered(1))
    return pl.pallas_call(
        _final_kernel,
        grid=(bsz, seq // tm),
        in_specs=[
            pl.BlockSpec((1, tm, D_MODEL), tok),
            pl.BlockSpec((1, halo, D_MODEL), prev),
            pl.BlockSpec((1, tm, d_mix), tok),
            pl.BlockSpec((1, halo, d_mix), prev),
            pl.BlockSpec((1, tm, d_mix), tok),
            pl.BlockSpec((1, halo, d_mix), prev),
            pl.BlockSpec((1, N_MOD, D_MODEL), lambda b, i: (b, 0, 0)),
            pl.BlockSpec((1, D_MODEL), const2),
            pl.BlockSpec((2 * d_mix, D_MODEL), const2, **resident),
            pl.BlockSpec((D_MODEL, 2 * D_FF), const2, **resident),
            pl.BlockSpec((3, 2 * D_FF), const2),
            pl.BlockSpec((1, 2 * D_FF), const2),
            pl.BlockSpec((D_FF, D_MODEL), const2, **resident),
        ],
        out_specs=pl.BlockSpec((1, tm, D_MODEL), tok),
        out_shape=jax.ShapeDtypeStruct((bsz, seq, D_MODEL), F32),
        scratch_shapes=[
            pltpu.VMEM((halo + tm, D_MODEL), BF16),
            pltpu.VMEM((halo + tm, FF_CHUNK), F32),
            pltpu.VMEM((halo + tm, FF_CHUNK), F32),
            pltpu.VMEM((halo + tm, FF_CHUNK), F32),
            pltpu.VMEM((halo + tm, FF_CHUNK), F32),
            pltpu.VMEM((tm, D_MODEL), F32),
        ],
        compiler_params=_compiler_params(("parallel", "arbitrary")),
        name="final",
    )(x, x, o_mla, o_mla, o_ca, o_ca, mod, g_mlp, w_out, w_up, conv_w, conv_b, w_down)


def _prep_layer(w_in, w_q_up, w_kv_up, g_mla_q, g_mla_k, g_ca_q, g_ca_k):
    half = MLA_ROPE // 2
    c0 = MLA_Q_RANK + MLA_KV_RANK
    k1 = w_in[:, c0:c0 + half]
    k2 = w_in[:, c0 + half:c0 + MLA_ROPE]
    z64 = jnp.zeros((D_MODEL, MLA_NOPE), w_in.dtype)
    w_in_ext = jnp.concatenate(
        [w_in[:, :c0], z64, k1, k2, k1, k2, z64, k2, k1, k2, k1, w_in[:, c0 + MLA_ROPE:]], axis=1).astype(BF16)

    wq = w_q_up.reshape(MLA_Q_RANK, MLA_HEADS, MLA_QK)
    x1 = wq[..., MLA_NOPE:MLA_NOPE + half]
    x2 = wq[..., MLA_NOPE + half:]
    w_q_ext = jnp.concatenate([wq, x2, x1], axis=-1).reshape(MLA_Q_RANK, MLA_HEADS * LANES).astype(BF16)

    wkv = w_kv_up.reshape(MLA_KV_RANK, MLA_HEADS, MLA_NOPE + MLA_V)
    wk = jnp.concatenate([wkv[..., :MLA_NOPE], jnp.zeros((MLA_KV_RANK, MLA_HEADS, LANES - MLA_NOPE), wkv.dtype)], axis=-1)
    w_kv_ext = jnp.concatenate(
        [wk.reshape(MLA_KV_RANK, MLA_HEADS * LANES), wkv[..., MLA_NOPE:].reshape(MLA_KV_RANK, MLA_HEADS * MLA_V)],
        axis=1).astype(BF16)

    gq1, gq2 = g_mla_q[MLA_NOPE:MLA_NOPE + half], g_mla_q[MLA_NOPE + half:]
    gk1, gk2 = g_mla_k[MLA_NOPE:MLA_NOPE + half], g_mla_k[MLA_NOPE + half:]
    z = jnp.zeros((MLA_NOPE,), F32)
    rows = [
        jnp.concatenate([g_mla_q, gq2, gq1]) * (MLA_QK ** -0.5 * LOG2E),
        jnp.concatenate([g_mla_k[:MLA_NOPE], z]),
        jnp.concatenate([z, gk1, gk2, gk1, gk2]),
        jnp.concatenate([z, gk2, gk1, gk2, gk1]),
        jnp.concatenate([g_ca_q, g_ca_q]) * (CA_HEAD_DIM ** -0.5 * LOG2E),
        jnp.concatenate([g_ca_k, g_ca_k]),
        jnp.zeros((LANES,), F32),
        jnp.zeros((LANES,), F32),
    ]
    gvec = jnp.stack(rows)
    return w_in_ext, w_q_ext, w_kv_ext, gvec


def kernel(x, c, positions, w_ada, b_ada, g_attn_norm, w_in, g_q_latent, g_kv_latent, w_q_up, w_kv_up, g_mla_q, g_mla_k, g_ca_q, g_ca_k, rel_bias, w_out, g_mlp_norm, w_up, conv_w, conv_b, w_down):
    bsz, seq, _ = x.shape
    depth = w_ada.shape[0]

    tc, ts = _rope_tables(positions)

    for l in range(depth):
        w_in_ext, w_q_ext, w_kv_ext, gvec = _prep_layer(
            w_in[l], w_q_up[l], w_kv_up[l], g_mla_q[l], g_mla_k[l], g_ca_q[l], g_ca_k[l])
        mod = _ada(c, w_ada[l], b_ada[l]).reshape(bsz, N_MOD, D_MODEL)
        bias_mask = _bias_tiles(rel_bias[l])
        q, k, v, cq, ck, cv = _proj(
            x, mod, tc, ts, g_attn_norm[l].reshape(1, -1), g_q_latent[l].reshape(1, -1),
            g_kv_latent[l].reshape(1, -1), gvec, w_in_ext, w_q_ext, w_kv_ext)
        o_mla = _mla(q, k, v)
        o_ca = _ca(cq, ck, cv, bias_mask)
        x = _final(x, o_mla, o_ca, mod, g_mlp_norm[l].reshape(1, -1), w_out[l].astype(BF16),
                   w_up[l].astype(BF16), conv_w[l], conv_b[l].reshape(1, -1), w_down[l].astype(BF16))
    return x
```

```python
import math

import numpy as np
import jax
import jax.numpy as jnp
from jax import lax
from jax.experimental import pallas as pl
from jax.experimental.pallas import tpu as pltpu

D_MODEL = 1024
CHUNK = 64
LEFT_CHUNKS = 8
MLA_HEADS = 8
MLA_Q_RANK = 256
MLA_KV_RANK = 128
MLA_NOPE = 64
MLA_ROPE = 32
MLA_QK = MLA_NOPE + MLA_ROPE
MLA_V = 64
ROPE_THETA = 10000.0
CA_HEADS = 8
CA_HEAD_DIM = 64
REL_CLIP = 128
D_FF = 2816
N_MOD = 6
EPS = 1e-6
NEG_INF = -1e30
LOG2E = math.log2(math.e)

LANES = 128
BF16_SUBLANES = 16
VMEM_LIMIT_BYTES = 56 * 1024 * 1024

PROJ_TM = 512
MLA_TQ = 256
CA_TQ = 256
CA_WIN = CA_TQ + LEFT_CHUNKS * CHUNK
CA_ROLL_W = 1024
REDUCE_ROWS = 64
MLA_LOOKAHEAD = 3
CA_LOOKAHEAD = 2
FIN_TM = 512
FIN_HALO = BF16_SUBLANES
FIN_ROW_BLOCK = 128
FF_CHUNK = 256
N_FF_CHUNKS = D_FF // FF_CHUNK

_C_QLAT = 0
_C_KVLAT = _C_QLAT + MLA_Q_RANK
_C_KA = _C_KVLAT + MLA_KV_RANK
_C_KB = _C_KA + LANES
_C_CQ = _C_KB + LANES
_C_CK = _C_CQ + CA_HEADS * CA_HEAD_DIM
D_IN_EXT = _C_CK + CA_HEADS * CA_HEAD_DIM

F32 = jnp.float32
BF16 = jnp.bfloat16


def _dot(a, b):
    return jnp.dot(a, b, preferred_element_type=F32)


def _dot_nt(a, b):
    return lax.dot_general(a, b, (((1,), (1,)), ((), ())), preferred_element_type=F32)


def _compiler_params(semantics):
    return pltpu.CompilerParams(dimension_semantics=semantics, vmem_limit_bytes=VMEM_LIMIT_BYTES)


def _ada_kernel(c_ref, w_ref, b_ref, o_ref):
    c = c_ref[...]
    s = c / (1.0 + jnp.exp(-c))
    o_ref[...] = _dot(s.astype(BF16), w_ref[...].astype(BF16)) + b_ref[...]


def _ada(c, w_ada, b_ada):
    bsz = c.shape[0]
    n_out = w_ada.shape[1]
    tn = D_MODEL
    return pl.pallas_call(
        _ada_kernel,
        grid=(n_out // tn,),
        in_specs=[
            pl.BlockSpec((bsz, D_MODEL), lambda j: (0, 0)),
            pl.BlockSpec((D_MODEL, tn), lambda j: (0, j)),
            pl.BlockSpec((1, tn), lambda j: (0, j)),
        ],
        out_specs=pl.BlockSpec((bsz, tn), lambda j: (0, j)),
        out_shape=jax.ShapeDtypeStruct((bsz, n_out), F32),
        compiler_params=_compiler_params(("arbitrary",)),
        name="ada",
    )(c, w_ada, b_ada.reshape(1, n_out))


def _expand(d, e_ref):
    hi = d.astype(BF16)
    r1 = d - hi.astype(F32)
    mid = r1.astype(BF16)
    lo = (r1 - mid.astype(F32)).astype(BF16)
    e = e_ref[...]
    return _dot(hi, e) + _dot(mid, e) + _dot(lo, e)


def _rope_kernel(pos_ref, inv_ref, ec_ref, es_ref, base_ref, tc_ref, ts_ref):
    rows = pos_ref.shape[0]
    tok_per_row = ec_ref.shape[1] // LANES
    ang = pos_ref[...].astype(F32) * inv_ref[...]
    tc_wide = _expand(jnp.cos(ang), ec_ref) + base_ref[...]
    ts_wide = _expand(jnp.sin(ang), es_ref)
    for t in range(tok_per_row):
        tc_ref[pl.ds(t, rows, stride=tok_per_row), :] = tc_wide[:, t * LANES:(t + 1) * LANES]
        ts_ref[pl.ds(t, rows, stride=tok_per_row), :] = ts_wide[:, t * LANES:(t + 1) * LANES]


def _rope_tables(positions):
    half = MLA_ROPE // 2
    bsz, seq = positions.shape
    n_tok = bsz * seq
    tok_per_row = LANES // half
    rows = n_tok // tok_per_row
    inv = jnp.power(ROPE_THETA, -jnp.arange(half, dtype=F32) / half)
    inv_t = jnp.tile(inv, tok_per_row).reshape(1, LANES)
    pos_rep = jnp.repeat(positions.reshape(n_tok, 1), half, axis=1).reshape(rows, LANES)

    src = np.arange(LANES)
    sel_c = np.zeros((LANES, tok_per_row * LANES), np.float32)
    sel_s = np.zeros((LANES, tok_per_row * LANES), np.float32)
    for rep, sign in enumerate((-1.0, 1.0, -1.0, 1.0)):
        dst = (src // half) * LANES + MLA_NOPE + rep * half + src % half
        sel_c[src, dst] = 1.0
        sel_s[src, dst] = sign
    base = np.tile((np.arange(LANES) < MLA_NOPE).astype(np.float32), tok_per_row).reshape(1, -1)

    tr = 512
    wide = tok_per_row * LANES
    const = lambda i: (0, 0)
    tc, ts = pl.pallas_call(
        _rope_kernel,
        grid=(rows // tr,),
        in_specs=[
            pl.BlockSpec((tr, LANES), lambda i: (i, 0)),
            pl.BlockSpec((1, LANES), const),
            pl.BlockSpec((LANES, wide), const),
            pl.BlockSpec((LANES, wide), const),
            pl.BlockSpec((1, wide), const),
        ],
        out_specs=[pl.BlockSpec((tr * tok_per_row, LANES), lambda i: (i, 0))] * 2,
        out_shape=[jax.ShapeDtypeStruct((n_tok, LANES), F32)] * 2,
        compiler_params=_compiler_params(("arbitrary",)),
        name="rope",
    )(pos_rep, inv_t, jnp.asarray(sel_c, BF16), jnp.asarray(sel_s, BF16), jnp.asarray(base))
    return tc.reshape(bsz, seq, LANES), ts.reshape(bsz, seq, LANES)


def _bias_kernel(y_ref, o_ref):
    tq, win = CA_TQ, CA_WIN
    full = jnp.broadcast_to(y_ref[0], (tq, CA_ROLL_W))
    rolled = pltpu.roll(full, 0, 1, stride=1, stride_axis=0)
    q_chunk = lax.broadcasted_iota(jnp.int32, (tq, 1), 0) // CHUNK
    k_chunk = lax.broadcasted_iota(jnp.int32, (1, win), 1) // CHUNK
    valid = jnp.logical_and(k_chunk >= q_chunk, k_chunk <= q_chunk + LEFT_CHUNKS)
    o_ref[0] = jnp.where(valid, rolled[:, :win] * LOG2E, NEG_INF).T


def _bias_tiles(rel_bias):
    n_heads = rel_bias.shape[0]
    t = jnp.arange(CA_ROLL_W)
    d = jnp.where(t < CA_WIN, t, t - CA_ROLL_W)
    idx = jnp.clip(LEFT_CHUNKS * CHUNK - d, -REL_CLIP, REL_CLIP) + REL_CLIP
    y = rel_bias[:, idx].reshape(n_heads, 1, CA_ROLL_W)
    return pl.pallas_call(
        _bias_kernel,
        grid=(n_heads,),
        in_specs=[pl.BlockSpec((1, 1, CA_ROLL_W), lambda h: (h, 0, 0))],
        out_specs=pl.BlockSpec((1, CA_WIN, CA_TQ), lambda h: (h, 0, 0)),
        out_shape=jax.ShapeDtypeStruct((n_heads, CA_WIN, CA_TQ), F32),
        compiler_params=_compiler_params(("arbitrary",)),
        name="bias",
    )(y)


def _proj_kernel(x_ref, mod_ref, tc_ref, ts_ref, gattn_ref, gql_ref, gkvl_ref, gv_ref,
                 win_ref, wq_ref, wk_ref, wvt_ref, wcvt_ref,
                 q_ref, k_ref, vt_ref, cq_ref, ck_ref, cvt_ref):
    x = x_ref[0]
    mod = mod_ref[0]
    sh, sc = mod[0:1, :], mod[1:2, :]
    y = x * lax.rsqrt(jnp.mean(x * x, axis=-1, keepdims=True) + EPS) * gattn_ref[...]
    h = (y * (1.0 + sc) + sh).astype(BF16)
    proj = _dot(h, win_ref[...])

    lane = lax.broadcasted_iota(jnp.int32, (1, LANES), 1)
    gv = gv_ref[...]
    g_q, g_kn, g_ka, g_kb, g_cq, g_ck = (gv[r:r + 1, :] for r in range(6))

    tc = tc_ref[0]
    ts = ts_ref[0]
    tq = jnp.where(lane < MLA_QK, tc, ts)

    ql = proj[:, _C_QLAT:_C_QLAT + MLA_Q_RANK]
    qn = (ql * lax.rsqrt(jnp.mean(ql * ql, axis=-1, keepdims=True) + EPS) * gql_ref[...]).astype(BF16)
    qp = _dot(qn, wq_ref[...])
    qmul = g_q * tq
    for hd in range(MLA_HEADS):
        qh = qp[:, hd * LANES:(hd + 1) * LANES]
        ss = jnp.sum(jnp.where(lane < MLA_QK, qh * qh, 0.0), axis=-1, keepdims=True)
        r = lax.rsqrt(ss * (1.0 / MLA_QK) + EPS)
        q_ref[0, :, hd * LANES:(hd + 1) * LANES] = (qh * r * qmul).astype(BF16)

    kvl = proj[:, _C_KVLAT:_C_KVLAT + MLA_KV_RANK]
    kvn = (kvl * lax.rsqrt(jnp.mean(kvl * kvl, axis=-1, keepdims=True) + EPS) * gkvl_ref[...]).astype(BF16)
    kvp = _dot(kvn, wk_ref[...])
    ka = proj[:, _C_KA:_C_KA + LANES]
    kb = proj[:, _C_KB:_C_KB + LANES]
    krot = ka * g_ka * tc + kb * g_kb * ts
    ss_rope = jnp.sum(jnp.where(lane < MLA_QK, ka * ka, 0.0), axis=-1, keepdims=True)
    for hd in range(MLA_HEADS):
        kh = kvp[:, hd * LANES:(hd + 1) * LANES]
        ss = jnp.sum(kh * kh, axis=-1, keepdims=True) + ss_rope
        r = lax.rsqrt(ss * (1.0 / MLA_QK) + EPS)
        k_ref[0, :, hd * LANES:(hd + 1) * LANES] = ((kh * g_kn + krot) * r).astype(BF16)
    vt_ref[0] = _dot_nt(wvt_ref[...], kvn).astype(BF16)

    lo = lane < CA_HEAD_DIM
    for src, gain, dst in ((_C_CQ, g_cq, cq_ref), (_C_CK, g_ck, ck_ref)):
        for p in range(CA_HEADS // 2):
            xx = proj[:, src + p * LANES:src + (p + 1) * LANES]
            x2 = xx * xx
            s_all = jnp.sum(x2, axis=-1, keepdims=True)
            s_lo = jnp.sum(jnp.where(lo, x2, 0.0), axis=-1, keepdims=True)
            r_lo = lax.rsqrt(s_lo * (1.0 / CA_HEAD_DIM) + EPS)
            r_hi = lax.rsqrt((s_all - s_lo) * (1.0 / CA_HEAD_DIM) + EPS)
            dst[0, :, p * LANES:(p + 1) * LANES] = (xx * jnp.where(lo, r_lo, r_hi) * gain).astype(BF16)
    cvt_ref[0] = _dot_nt(wcvt_ref[...], h).astype(BF16)


def _proj(x, mod, tc, ts, g_attn, g_ql, g_kvl, gvec, w_in_ext, w_q_ext, w_k_ext, w_vt, w_cvt):
    bsz, seq, _ = x.shape
    tm = PROJ_TM
    tok = lambda b, i: (b, i, 0)
    tok_t = lambda b, i: (b, 0, i)
    const2 = lambda b, i: (0, 0)
    wide = MLA_HEADS * LANES
    narrow = CA_HEADS * CA_HEAD_DIM
    row_major = lambda w: (jax.ShapeDtypeStruct((bsz, seq, w), BF16), pl.BlockSpec((1, tm, w), tok))
    feat_major = lambda w: (jax.ShapeDtypeStruct((bsz, w, seq), BF16), pl.BlockSpec((1, w, tm), tok_t))
    outs = [row_major(wide), row_major(wide), feat_major(MLA_HEADS * MLA_V),
            row_major(narrow), row_major(narrow), feat_major(narrow)]
    out_shapes = [o[0] for o in outs]
    out_specs = [o[1] for o in outs]
    return pl.pallas_call(
        _proj_kernel,
        grid=(bsz, seq // tm),
        in_specs=[
            pl.BlockSpec((1, tm, D_MODEL), tok),
            pl.BlockSpec((1, N_MOD, D_MODEL), lambda b, i: (b, 0, 0)),
            pl.BlockSpec((1, tm, LANES), tok),
            pl.BlockSpec((1, tm, LANES), tok),
            pl.BlockSpec((1, D_MODEL), const2),
            pl.BlockSpec((1, MLA_Q_RANK), const2),
            pl.BlockSpec((1, MLA_KV_RANK), const2),
            pl.BlockSpec((8, LANES), const2),
            pl.BlockSpec((D_MODEL, D_IN_EXT), const2),
            pl.BlockSpec((MLA_Q_RANK, wide), const2),
            pl.BlockSpec((MLA_KV_RANK, wide), const2),
            pl.BlockSpec((MLA_HEADS * MLA_V, MLA_KV_RANK), const2),
            pl.BlockSpec((narrow, D_MODEL), const2),
        ],
        out_specs=out_specs,
        out_shape=out_shapes,
        compiler_params=_compiler_params(("parallel", "parallel")),
        name="proj",
    )(x, mod, tc, ts, g_attn, g_ql, g_kvl, gvec, w_in_ext, w_q_ext, w_k_ext, w_vt, w_cvt)


def _softmax_pv_t(score_parts, value_parts):
    def fold(x, op):
        return op(x.reshape(x.shape[0] // REDUCE_ROWS, REDUCE_ROWS, x.shape[1]), axis=0)

    m = fold(score_parts[0], jnp.max)
    for s in score_parts[1:]:
        m = jnp.maximum(m, fold(s, jnp.max))
    m = m.max(axis=0, keepdims=True)
    l = None
    acc = None
    for s, vt in zip(score_parts, value_parts):
        p = jnp.exp2(s - m)
        ps = fold(p, jnp.sum).sum(axis=0, keepdims=True)
        pv = _dot(vt, p.astype(BF16))
        l = ps if l is None else l + ps
        acc = pv if acc is None else acc + pv
    return acc / l


def _attention_pipeline(units, lookahead, scores, finish, store):
    pending = [scores(*u) for u in units[:lookahead]]
    outs = []
    for n, (i, hh) in enumerate(units):
        if n + lookahead < len(units):
            pending.append(scores(*units[n + lookahead]))
        outs.append(finish(i, pending.pop(0)))
        if hh == 1:
            store(i, outs)
            outs = []


def _store_pair(o_ref, lo, hi, outs, head_rows):
    row = lax.broadcasted_iota(jnp.int32, (LANES, 1), 0)
    o_t = jnp.where(row < head_rows, outs[0], outs[1])
    o_ref[0, lo:hi, :] = o_t.T.astype(BF16)


def _mla_kernel(q_ref, k_ref, vt_ref, o_ref):
    t = MLA_TQ
    seq = q_ref.shape[1]
    key_chunk = lax.broadcasted_iota(jnp.int32, (t, 1), 0) // CHUNK
    qry_chunk = lax.broadcasted_iota(jnp.int32, (1, t), 1) // CHUNK
    diag_mask = key_chunk <= qry_chunk

    def scores(i, hh):
        lo = i * t
        hs = slice(hh * LANES, (hh + 1) * LANES)
        q = q_ref[0, lo:lo + t, hs]
        parts = [jnp.where(diag_mask, _dot_nt(k_ref[0, lo:lo + t, hs], q), NEG_INF)]
        if i > 0:
            parts.insert(0, _dot_nt(k_ref[0, 0:lo, hs], q))
        return parts

    def finish(i, parts):
        lo = i * t
        vals = [vt_ref[0, :, lo:lo + t]]
        if i > 0:
            vals.insert(0, vt_ref[0, :, 0:lo])
        return _softmax_pv_t(parts, vals)

    def store(i, outs):
        _store_pair(o_ref, i * t, (i + 1) * t, outs, MLA_V)

    _attention_pipeline([(i, hh) for i in range(seq // t) for hh in range(2)], MLA_LOOKAHEAD, scores, finish, store)


def _mla(q, k, vt):
    bsz, seq, _ = q.shape
    pairs = MLA_HEADS // 2
    return pl.pallas_call(
        _mla_kernel,
        grid=(bsz, pairs),
        in_specs=[
            pl.BlockSpec((1, seq, 2 * LANES), lambda b, p: (b, 0, p)),
            pl.BlockSpec((1, seq, 2 * LANES), lambda b, p: (b, 0, p)),
            pl.BlockSpec((1, LANES, seq), lambda b, p: (b, p, 0)),
        ],
        out_specs=pl.BlockSpec((1, seq, LANES), lambda b, p: (b, 0, p)),
        out_shape=jax.ShapeDtypeStruct((bsz, seq, MLA_HEADS * MLA_V), BF16),
        compiler_params=_compiler_params(("parallel", "parallel")),
        name="mla",
    )(q, k, vt)


def _ca_kernel(q_ref, k_ref, vt_ref, bm_ref, o_ref):
    tq, win = CA_TQ, CA_WIN
    seq = q_ref.shape[1]
    lane = lax.broadcasted_iota(jnp.int32, (1, LANES), 1)

    def window(i):
        hi = (i + 1) * tq
        k_lo = max(hi - win, 0)
        return k_lo, hi, k_lo - (hi - win)

    def scores(i, hh):
        k_lo, hi, c_lo = window(i)
        q = q_ref[0, i * tq:hi, :]
        head_lanes = (lane < CA_HEAD_DIM) if hh == 0 else (lane >= CA_HEAD_DIM)
        qm = jnp.where(head_lanes, q, jnp.zeros_like(q))
        return [_dot_nt(k_ref[0, k_lo:hi, :], qm) + bm_ref[hh, c_lo:, :]]

    def finish(i, parts):
        k_lo, hi, _ = window(i)
        return _softmax_pv_t(parts, [vt_ref[0, :, k_lo:hi]])

    def store(i, outs):
        _store_pair(o_ref, i * tq, (i + 1) * tq, outs, CA_HEAD_DIM)

    _attention_pipeline([(i, hh) for i in range(seq // tq) for hh in range(2)], CA_LOOKAHEAD, scores, finish, store)


def _ca(cq, ck, cvt, bias_mask):
    bsz, seq, _ = cq.shape
    pairs = CA_HEADS // 2
    tok = lambda b, p: (b, 0, p)
    return pl.pallas_call(
        _ca_kernel,
        grid=(bsz, pairs),
        in_specs=[
            pl.BlockSpec((1, seq, LANES), tok),
            pl.BlockSpec((1, seq, LANES), tok),
            pl.BlockSpec((1, LANES, seq), lambda b, p: (b, p, 0)),
            pl.BlockSpec((2, CA_WIN, CA_TQ), lambda b, p: (p, 0, 0)),
        ],
        out_specs=pl.BlockSpec((1, seq, LANES), tok),
        out_shape=jax.ShapeDtypeStruct((bsz, seq, CA_HEADS * CA_HEAD_DIM), BF16),
        compiler_params=_compiler_params(("parallel", "parallel")),
        name="ca",
    )(cq, ck, cvt, bias_mask)


def _final_kernel(x_ref, xh_ref, om_ref, omh_ref, oc_ref, och_ref, mod_ref, gmlp_ref,
                  wo_ref, wu_ref, cw_ref, cb_ref, wd_ref,
                  out_ref, h2_sc, ug0_sc, ug1_sc, uv0_sc, uv1_sc, acc_sc):
    i = pl.program_id(1)
    tm = FIN_TM
    halo = FIN_HALO
    mod = mod_ref[0]
    g_a, sh_m, sc_m, g_m = mod[2:3, :], mod[3:4, :], mod[4:5, :], mod[5:6, :]

    u_bufs = ((ug0_sc, uv0_sc), (ug1_sc, uv1_sc))
    rb = FIN_ROW_BLOCK
    n_rb = tm // rb

    def up_rows(r):
        return (0, halo + rb) if r == 0 else (halo + r * rb, halo + (r + 1) * rb)

    def ext_rows(halo_ref, main_ref, r):
        if r == 0:
            return jnp.concatenate([halo_ref[0], main_ref[0, 0:rb, :]], axis=0)
        return main_ref[0, r * rb:(r + 1) * rb, :]

    def out_project(r):
        o_r = jnp.concatenate([ext_rows(omh_ref, om_ref, r), ext_rows(och_ref, oc_ref, r)], axis=-1)
        x1 = ext_rows(xh_ref, x_ref, r) + g_a * _dot(o_r, wo_ref[...])
        out_ref[0, r * rb:(r + 1) * rb, :] = x1[halo:] if r == 0 else x1
        return x1

    def normalize(r, x1):
        lo, hi = up_rows(r)
        y = x1 * lax.rsqrt(jnp.mean(x1 * x1, axis=-1, keepdims=True) + EPS) * gmlp_ref[...]
        h2 = y * (1.0 + sc_m) + sh_m
        if r == 0:
            h2_sc[:halo] = jnp.where(i > 0, h2[:halo], 0.0).astype(BF16)
            h2_sc[halo:hi] = h2[halo:].astype(BF16)
        else:
            h2_sc[lo:hi] = h2.astype(BF16)

    def up_project(c, r):
        ug_sc, uv_sc = u_bufs[c % 2]
        lo, hi = up_rows(r)
        h2b = h2_sc[lo:hi, :]
        ug_sc[lo:hi, :] = _dot(h2b, wu_ref[:, c * FF_CHUNK:(c + 1) * FF_CHUNK])
        uv_sc[lo:hi, :] = _dot(h2b, wu_ref[:, D_FF + c * FF_CHUNK:D_FF + (c + 1) * FF_CHUNK])

    def conv(u_sc, col, r):
        cw = cw_ref[:, col:col + FF_CHUNK]
        lo = halo + r * rb
        return (u_sc[lo - 2:lo - 2 + rb, :] * cw[0:1, :] + u_sc[lo - 1:lo - 1 + rb, :] * cw[1:2, :]
                + u_sc[lo:lo + rb, :] * cw[2:3, :] + cb_ref[:, col:col + FF_CHUNK])

    def mlp_rows(c, r):
        ug_sc, uv_sc = u_bufs[c % 2]
        gcol, vcol = c * FF_CHUNK, D_FF + c * FF_CHUNK
        gate = conv(ug_sc, gcol, r)
        val = conv(uv_sc, vcol, r)
        act = (gate / (1.0 + jnp.exp(-gate)) * val).astype(BF16)
        down = _dot(act, wd_ref[gcol:gcol + FF_CHUNK, :])
        rows = slice(r * rb, (r + 1) * rb)
        if c == 0:
            acc_sc[rows, :] = down
        elif c < N_FF_CHUNKS - 1:
            acc_sc[rows, :] += down
        else:
            out_ref[0, rows, :] += g_m * (acc_sc[rows, :] + down)

    x1_blocks = {}
    for step in range(n_rb + 2):
        if step < n_rb:
            x1_blocks[step] = out_project(step)
        if 0 <= step - 1 < n_rb:
            normalize(step - 1, x1_blocks.pop(step - 1))
        if 0 <= step - 2 < n_rb:
            up_project(0, step - 2)
    for c in range(N_FF_CHUNKS):
        for r in range(n_rb):
            if c + 1 < N_FF_CHUNKS:
                up_project(c + 1, r)
            mlp_rows(c, r)


def _final(x, o_mla, o_ca, mod, g_mlp, w_out, w_up, conv_w, conv_b, w_down):
    bsz, seq, _ = x.shape
    tm, halo = FIN_TM, FIN_HALO
    tok = lambda b, i: (b, i, 0)
    prev = lambda b, i: (b, jnp.maximum(i * (tm // halo) - 1, 0), 0)
    const2 = lambda b, i: (0, 0)
    d_mix = o_mla.shape[-1]
    resident = dict(pipeline_mode=pl.Buffered(1))
    return pl.pallas_call(
        _final_kernel,
        grid=(bsz, seq // tm),
        in_specs=[
            pl.BlockSpec((1, tm, D_MODEL), tok),
            pl.BlockSpec((1, halo, D_MODEL), prev),
            pl.BlockSpec((1, tm, d_mix), tok),
            pl.BlockSpec((1, halo, d_mix), prev),
            pl.BlockSpec((1, tm, d_mix), tok),
            pl.BlockSpec((1, halo, d_mix), prev),
            pl.BlockSpec((1, N_MOD, D_MODEL), lambda b, i: (b, 0, 0)),
            pl.BlockSpec((1, D_MODEL), const2),
            pl.BlockSpec((2 * d_mix, D_MODEL), const2, **resident),
            pl.BlockSpec((D_MODEL, 2 * D_FF), const2, **resident),
            pl.BlockSpec((3, 2 * D_FF), const2),
            pl.BlockSpec((1, 2 * D_FF), const2),
            pl.BlockSpec((D_FF, D_MODEL), const2, **resident),
        ],
        out_specs=pl.BlockSpec((1, tm, D_MODEL), tok),
        out_shape=jax.ShapeDtypeStruct((bsz, seq, D_MODEL), F32),
        scratch_shapes=[
            pltpu.VMEM((halo + tm, D_MODEL), BF16),
            pltpu.VMEM((halo + tm, FF_CHUNK), F32),
            pltpu.VMEM((halo + tm, FF_CHUNK), F32),
            pltpu.VMEM((halo + tm, FF_CHUNK), F32),
            pltpu.VMEM((halo + tm, FF_CHUNK), F32),
            pltpu.VMEM((tm, D_MODEL), F32),
        ],
        compiler_params=_compiler_params(("parallel", "arbitrary")),
        name="final",
    )(x, x, o_mla, o_mla, o_ca, o_ca, mod, g_mlp, w_out, w_up, conv_w, conv_b, w_down)


def _prep_layer(w_in, w_q_up, w_kv_up, g_mla_q, g_mla_k, g_ca_q, g_ca_k):
    half = MLA_ROPE // 2
    c0 = MLA_Q_RANK + MLA_KV_RANK
    k1 = w_in[:, c0:c0 + half]
    k2 = w_in[:, c0 + half:c0 + MLA_ROPE]
    z64 = jnp.zeros((D_MODEL, MLA_NOPE), w_in.dtype)
    c_cv = c0 + MLA_ROPE + 2 * CA_HEADS * CA_HEAD_DIM
    w_in_ext = jnp.concatenate(
        [w_in[:, :c0], z64, k1, k2, k1, k2, z64, k2, k1, k2, k1, w_in[:, c0 + MLA_ROPE:c_cv]], axis=1).astype(BF16)
    w_cvt = w_in[:, c_cv:].T.astype(BF16)

    wq = w_q_up.reshape(MLA_Q_RANK, MLA_HEADS, MLA_QK)
    x1 = wq[..., MLA_NOPE:MLA_NOPE + half]
    x2 = wq[..., MLA_NOPE + half:]
    w_q_ext = jnp.concatenate([wq, x2, x1], axis=-1).reshape(MLA_Q_RANK, MLA_HEADS * LANES).astype(BF16)

    wkv = w_kv_up.reshape(MLA_KV_RANK, MLA_HEADS, MLA_NOPE + MLA_V)
    wk = jnp.concatenate([wkv[..., :MLA_NOPE], jnp.zeros((MLA_KV_RANK, MLA_HEADS, LANES - MLA_NOPE), wkv.dtype)], axis=-1)
    w_k_ext = wk.reshape(MLA_KV_RANK, MLA_HEADS * LANES).astype(BF16)
    w_vt = wkv[..., MLA_NOPE:].reshape(MLA_KV_RANK, MLA_HEADS * MLA_V).T.astype(BF16)

    gq1, gq2 = g_mla_q[MLA_NOPE:MLA_NOPE + half], g_mla_q[MLA_NOPE + half:]
    gk1, gk2 = g_mla_k[MLA_NOPE:MLA_NOPE + half], g_mla_k[MLA_NOPE + half:]
    z = jnp.zeros((MLA_NOPE,), F32)
    rows = [
        jnp.concatenate([g_mla_q, gq2, gq1]) * (MLA_QK ** -0.5 * LOG2E),
        jnp.concatenate([g_mla_k[:MLA_NOPE], z]),
        jnp.concatenate([z, gk1, gk2, gk1, gk2]),
        jnp.concatenate([z, gk2, gk1, gk2, gk1]),
        jnp.concatenate([g_ca_q, g_ca_q]) * (CA_HEAD_DIM ** -0.5 * LOG2E),
        jnp.concatenate([g_ca_k, g_ca_k]),
        jnp.zeros((LANES,), F32),
        jnp.zeros((LANES,), F32),
    ]
    gvec = jnp.stack(rows)
    return w_in_ext, w_q_ext, w_k_ext, w_vt, w_cvt, gvec


def kernel(x, c, positions, w_ada, b_ada, g_attn_norm, w_in, g_q_latent, g_kv_latent, w_q_up, w_kv_up, g_mla_q, g_mla_k, g_ca_q, g_ca_k, rel_bias, w_out, g_mlp_norm, w_up, conv_w, conv_b, w_down):
    bsz, seq, _ = x.shape
    depth = w_ada.shape[0]

    tc, ts = _rope_tables(positions)

    for l in range(depth):
        w_in_ext, w_q_ext, w_k_ext, w_vt, w_cvt, gvec = _prep_layer(
            w_in[l], w_q_up[l], w_kv_up[l], g_mla_q[l], g_mla_k[l], g_ca_q[l], g_ca_k[l])
        mod = _ada(c, w_ada[l], b_ada[l]).reshape(bsz, N_MOD, D_MODEL)
        bias_mask = _bias_tiles(rel_bias[l])
        q, k, vt, cq, ck, cvt = _proj(
            x, mod, tc, ts, g_attn_norm[l].reshape(1, -1), g_q_latent[l].reshape(1, -1),
            g_kv_latent[l].reshape(1, -1), gvec, w_in_ext, w_q_ext, w_k_ext, w_vt, w_cvt)
        o_mla = _mla(q, k, vt)
        o_ca = _ca(cq, ck, cvt, bias_mask)
        x = _final(x, o_mla, o_ca, mod, g_mlp_norm[l].reshape(1, -1), w_out[l].astype(BF16),
                   w_up[l].astype(BF16), conv_w[l], conv_b[l].reshape(1, -1), w_down[l].astype(BF16))
    return x
```

```python
import math

import numpy as np
import jax
import jax.numpy as jnp
from jax import lax
from jax.experimental import pallas as pl
from jax.experimental.pallas import tpu as pltpu

D_MODEL = 1024
CHUNK = 64
LEFT_CHUNKS = 8
MLA_HEADS = 8
MLA_Q_RANK = 256
MLA_KV_RANK = 128
MLA_NOPE = 64
MLA_ROPE = 32
MLA_QK = MLA_NOPE + MLA_ROPE
MLA_V = 64
ROPE_THETA = 10000.0
CA_HEADS = 8
CA_HEAD_DIM = 64
REL_CLIP = 128
D_FF = 2816
N_MOD = 6
EPS = 1e-6
NEG_INF = -1e30
LOG2E = math.log2(math.e)

LANES = 128
BF16_SUBLANES = 16
VMEM_LIMIT_BYTES = 56 * 1024 * 1024

PROJ_TM = 512
MLA_TQ = 256
CA_TQ = 256
CA_WIN = CA_TQ + LEFT_CHUNKS * CHUNK
CA_ROLL_W = 1024
REDUCE_ROWS = 64
MLA_LOOKAHEAD = 3
CA_LOOKAHEAD = 2
FIN_TM = 512
FIN_HALO = BF16_SUBLANES
FIN_ROW_BLOCK = 256
FF_CHUNK = 256
N_FF_CHUNKS = D_FF // FF_CHUNK

_C_QLAT = 0
_C_KVLAT = _C_QLAT + MLA_Q_RANK
_C_KA = _C_KVLAT + MLA_KV_RANK
_C_KB = _C_KA + LANES
_C_CQ = _C_KB + LANES
_C_CK = _C_CQ + CA_HEADS * CA_HEAD_DIM
D_IN_EXT = _C_CK + CA_HEADS * CA_HEAD_DIM

F32 = jnp.float32
BF16 = jnp.bfloat16


def _dot(a, b):
    return jnp.dot(a, b, preferred_element_type=F32)


def _dot_nt(a, b):
    return lax.dot_general(a, b, (((1,), (1,)), ((), ())), preferred_element_type=F32)


def _compiler_params(semantics):
    return pltpu.CompilerParams(dimension_semantics=semantics, vmem_limit_bytes=VMEM_LIMIT_BYTES)


def _ada_kernel(c_ref, w_ref, b_ref, o_ref):
    c = c_ref[...]
    s = c / (1.0 + jnp.exp(-c))
    o_ref[...] = _dot(s.astype(BF16), w_ref[...].astype(BF16)) + b_ref[...]


def _ada(c, w_ada, b_ada):
    bsz = c.shape[0]
    n_out = w_ada.shape[1]
    tn = D_MODEL
    return pl.pallas_call(
        _ada_kernel,
        grid=(n_out // tn,),
        in_specs=[
            pl.BlockSpec((bsz, D_MODEL), lambda j: (0, 0)),
            pl.BlockSpec((D_MODEL, tn), lambda j: (0, j)),
            pl.BlockSpec((1, tn), lambda j: (0, j)),
        ],
        out_specs=pl.BlockSpec((bsz, tn), lambda j: (0, j)),
        out_shape=jax.ShapeDtypeStruct((bsz, n_out), F32),
        compiler_params=_compiler_params(("arbitrary",)),
        name="ada",
    )(c, w_ada, b_ada.reshape(1, n_out))


def _expand(d, e_ref):
    hi = d.astype(BF16)
    r1 = d - hi.astype(F32)
    mid = r1.astype(BF16)
    lo = (r1 - mid.astype(F32)).astype(BF16)
    e = e_ref[...]
    return _dot(hi, e) + _dot(mid, e) + _dot(lo, e)


def _rope_kernel(pos_ref, inv_ref, ec_ref, es_ref, base_ref, tc_ref, ts_ref):
    rows = pos_ref.shape[0]
    tok_per_row = ec_ref.shape[1] // LANES
    ang = pos_ref[...].astype(F32) * inv_ref[...]
    tc_wide = _expand(jnp.cos(ang), ec_ref) + base_ref[...]
    ts_wide = _expand(jnp.sin(ang), es_ref)
    for t in range(tok_per_row):
        tc_ref[pl.ds(t, rows, stride=tok_per_row), :] = tc_wide[:, t * LANES:(t + 1) * LANES]
        ts_ref[pl.ds(t, rows, stride=tok_per_row), :] = ts_wide[:, t * LANES:(t + 1) * LANES]


def _rope_tables(positions):
    half = MLA_ROPE // 2
    bsz, seq = positions.shape
    n_tok = bsz * seq
    tok_per_row = LANES // half
    rows = n_tok // tok_per_row
    inv = jnp.power(ROPE_THETA, -jnp.arange(half, dtype=F32) / half)
    inv_t = jnp.tile(inv, tok_per_row).reshape(1, LANES)
    pos_rep = jnp.repeat(positions.reshape(n_tok, 1), half, axis=1).reshape(rows, LANES)

    src = np.arange(LANES)
    sel_c = np.zeros((LANES, tok_per_row * LANES), np.float32)
    sel_s = np.zeros((LANES, tok_per_row * LANES), np.float32)
    for rep, sign in enumerate((-1.0, 1.0, -1.0, 1.0)):
        dst = (src // half) * LANES + MLA_NOPE + rep * half + src % half
        sel_c[src, dst] = 1.0
        sel_s[src, dst] = sign
    base = np.tile((np.arange(LANES) < MLA_NOPE).astype(np.float32), tok_per_row).reshape(1, -1)

    tr = 512
    wide = tok_per_row * LANES
    const = lambda i: (0, 0)
    tc, ts = pl.pallas_call(
        _rope_kernel,
        grid=(rows // tr,),
        in_specs=[
            pl.BlockSpec((tr, LANES), lambda i: (i, 0)),
            pl.BlockSpec((1, LANES), const),
            pl.BlockSpec((LANES, wide), const),
            pl.BlockSpec((LANES, wide), const),
            pl.BlockSpec((1, wide), const),
        ],
        out_specs=[pl.BlockSpec((tr * tok_per_row, LANES), lambda i: (i, 0))] * 2,
        out_shape=[jax.ShapeDtypeStruct((n_tok, LANES), F32)] * 2,
        compiler_params=_compiler_params(("arbitrary",)),
        name="rope",
    )(pos_rep, inv_t, jnp.asarray(sel_c, BF16), jnp.asarray(sel_s, BF16), jnp.asarray(base))
    return tc.reshape(bsz, seq, LANES), ts.reshape(bsz, seq, LANES)


def _bias_kernel(y_ref, o_ref):
    tq, win = CA_TQ, CA_WIN
    full = jnp.broadcast_to(y_ref[0], (tq, CA_ROLL_W))
    rolled = pltpu.roll(full, 0, 1, stride=1, stride_axis=0)
    q_chunk = lax.broadcasted_iota(jnp.int32, (tq, 1), 0) // CHUNK
    k_chunk = lax.broadcasted_iota(jnp.int32, (1, win), 1) // CHUNK
    valid = jnp.logical_and(k_chunk >= q_chunk, k_chunk <= q_chunk + LEFT_CHUNKS)
    o_ref[0] = jnp.where(valid, rolled[:, :win] * LOG2E, NEG_INF).T


def _bias_tiles(rel_bias):
    n_heads = rel_bias.shape[0]
    t = jnp.arange(CA_ROLL_W)
    d = jnp.where(t < CA_WIN, t, t - CA_ROLL_W)
    idx = jnp.clip(LEFT_CHUNKS * CHUNK - d, -REL_CLIP, REL_CLIP) + REL_CLIP
    y = rel_bias[:, idx].reshape(n_heads, 1, CA_ROLL_W)
    return pl.pallas_call(
        _bias_kernel,
        grid=(n_heads,),
        in_specs=[pl.BlockSpec((1, 1, CA_ROLL_W), lambda h: (h, 0, 0))],
        out_specs=pl.BlockSpec((1, CA_WIN, CA_TQ), lambda h: (h, 0, 0)),
        out_shape=jax.ShapeDtypeStruct((n_heads, CA_WIN, CA_TQ), F32),
        compiler_params=_compiler_params(("arbitrary",)),
        name="bias",
    )(y)


def _proj_kernel(x_ref, mod_ref, tc_ref, ts_ref, gattn_ref, gql_ref, gkvl_ref, gv_ref,
                 win_ref, wq_ref, wkv_ref, wcvt_ref,
                 q_ref, k_ref, v_ref, cq_ref, ck_ref, cvt_ref):
    x = x_ref[0]
    mod = mod_ref[0]
    sh, sc = mod[0:1, :], mod[1:2, :]
    y = x * lax.rsqrt(jnp.mean(x * x, axis=-1, keepdims=True) + EPS) * gattn_ref[...]
    h = (y * (1.0 + sc) + sh).astype(BF16)
    proj = _dot(h, win_ref[...])

    lane = lax.broadcasted_iota(jnp.int32, (1, LANES), 1)
    gv = gv_ref[...]
    g_q, g_kn, g_ka, g_kb, g_cq, g_ck = (gv[r:r + 1, :] for r in range(6))

    tc = tc_ref[0]
    ts = ts_ref[0]
    tq = jnp.where(lane < MLA_QK, tc, ts)

    ql = proj[:, _C_QLAT:_C_QLAT + MLA_Q_RANK]
    qn = (ql * lax.rsqrt(jnp.mean(ql * ql, axis=-1, keepdims=True) + EPS) * gql_ref[...]).astype(BF16)
    qp = _dot(qn, wq_ref[...])
    qmul = g_q * tq
    for hd in range(MLA_HEADS):
        qh = qp[:, hd * LANES:(hd + 1) * LANES]
        ss = jnp.sum(jnp.where(lane < MLA_QK, qh * qh, 0.0), axis=-1, keepdims=True)
        r = lax.rsqrt(ss * (1.0 / MLA_QK) + EPS)
        q_ref[0, :, hd * LANES:(hd + 1) * LANES] = (qh * r * qmul).astype(BF16)

    kvl = proj[:, _C_KVLAT:_C_KVLAT + MLA_KV_RANK]
    kvn = (kvl * lax.rsqrt(jnp.mean(kvl * kvl, axis=-1, keepdims=True) + EPS) * gkvl_ref[...]).astype(BF16)
    kvp = _dot(kvn, wkv_ref[...])
    ka = proj[:, _C_KA:_C_KA + LANES]
    kb = proj[:, _C_KB:_C_KB + LANES]
    krot = ka * g_ka * tc + kb * g_kb * ts
    ss_rope = jnp.sum(jnp.where(lane < MLA_QK, ka * ka, 0.0), axis=-1, keepdims=True)
    for hd in range(MLA_HEADS):
        kh = kvp[:, hd * LANES:(hd + 1) * LANES]
        ss = jnp.sum(kh * kh, axis=-1, keepdims=True) + ss_rope
        r = lax.rsqrt(ss * (1.0 / MLA_QK) + EPS)
        k_ref[0, :, hd * LANES:(hd + 1) * LANES] = ((kh * g_kn + krot) * r).astype(BF16)
    v_ref[0] = kvp[:, MLA_HEADS * LANES:].astype(BF16)

    lo = lane < CA_HEAD_DIM
    for src, gain, dst in ((_C_CQ, g_cq, cq_ref), (_C_CK, g_ck, ck_ref)):
        for p in range(CA_HEADS // 2):
            xx = proj[:, src + p * LANES:src + (p + 1) * LANES]
            x2 = xx * xx
            s_all = jnp.sum(x2, axis=-1, keepdims=True)
            s_lo = jnp.sum(jnp.where(lo, x2, 0.0), axis=-1, keepdims=True)
            r_lo = lax.rsqrt(s_lo * (1.0 / CA_HEAD_DIM) + EPS)
            r_hi = lax.rsqrt((s_all - s_lo) * (1.0 / CA_HEAD_DIM) + EPS)
            dst[0, :, p * LANES:(p + 1) * LANES] = (xx * jnp.where(lo, r_lo, r_hi) * gain).astype(BF16)
    cvt_ref[0] = _dot_nt(wcvt_ref[...], h).astype(BF16)


def _proj(x, mod, tc, ts, g_attn, g_ql, g_kvl, gvec, w_in_ext, w_q_ext, w_kv_ext, w_cvt):
    bsz, seq, _ = x.shape
    tm = PROJ_TM
    tok = lambda b, i: (b, i, 0)
    tok_t = lambda b, i: (b, 0, i)
    const2 = lambda b, i: (0, 0)
    wide = MLA_HEADS * LANES
    narrow = CA_HEADS * CA_HEAD_DIM
    row_major = lambda w: (jax.ShapeDtypeStruct((bsz, seq, w), BF16), pl.BlockSpec((1, tm, w), tok))
    feat_major = lambda w: (jax.ShapeDtypeStruct((bsz, w, seq), BF16), pl.BlockSpec((1, w, tm), tok_t))
    outs = [row_major(wide), row_major(wide), row_major(MLA_HEADS * MLA_V),
            row_major(narrow), row_major(narrow), feat_major(narrow)]
    out_shapes = [o[0] for o in outs]
    out_specs = [o[1] for o in outs]
    return pl.pallas_call(
        _proj_kernel,
        grid=(bsz, seq // tm),
        in_specs=[
            pl.BlockSpec((1, tm, D_MODEL), tok),
            pl.BlockSpec((1, N_MOD, D_MODEL), lambda b, i: (b, 0, 0)),
            pl.BlockSpec((1, tm, LANES), tok),
            pl.BlockSpec((1, tm, LANES), tok),
            pl.BlockSpec((1, D_MODEL), const2),
            pl.BlockSpec((1, MLA_Q_RANK), const2),
            pl.BlockSpec((1, MLA_KV_RANK), const2),
            pl.BlockSpec((8, LANES), const2),
            pl.BlockSpec((D_MODEL, D_IN_EXT), const2),
            pl.BlockSpec((MLA_Q_RANK, wide), const2),
            pl.BlockSpec((MLA_KV_RANK, wide + MLA_HEADS * MLA_V), const2),
            pl.BlockSpec((narrow, D_MODEL), const2),
        ],
        out_specs=out_specs,
        out_shape=out_shapes,
        compiler_params=_compiler_params(("parallel", "parallel")),
        name="proj",
    )(x, mod, tc, ts, g_attn, g_ql, g_kvl, gvec, w_in_ext, w_q_ext, w_kv_ext, w_cvt)


def _softmax_pv_t(score_parts, value_parts):
    def fold(x, op):
        return op(x.reshape(x.shape[0] // REDUCE_ROWS, REDUCE_ROWS, x.shape[1]), axis=0)

    m = fold(score_parts[0], jnp.max)
    for s in score_parts[1:]:
        m = jnp.maximum(m, fold(s, jnp.max))
    m = m.max(axis=0, keepdims=True)
    l = None
    acc = None
    for s, vt in zip(score_parts, value_parts):
        p = jnp.exp2(s - m)
        ps = fold(p, jnp.sum).sum(axis=0, keepdims=True)
        pv = _dot(vt, p.astype(BF16))
        l = ps if l is None else l + ps
        acc = pv if acc is None else acc + pv
    return acc / l


def _attention_pipeline(units, lookahead, scores, finish, store):
    pending = [scores(*u) for u in units[:lookahead]]
    outs = []
    for n, (i, hh) in enumerate(units):
        if n + lookahead < len(units):
            pending.append(scores(*units[n + lookahead]))
        outs.append(finish(i, pending.pop(0)))
        if hh == 1:
            store(i, outs)
            outs = []


def _store_pair(o_ref, lo, hi, outs, head_rows):
    row = lax.broadcasted_iota(jnp.int32, (LANES, 1), 0)
    o_t = jnp.where(row < head_rows, outs[0], outs[1])
    o_ref[0, lo:hi, :] = o_t.T.astype(BF16)


def _softmax_pv(score_parts, value_parts):
    m = score_parts[0].max(axis=-1, keepdims=True)
    for s in score_parts[1:]:
        m = jnp.maximum(m, s.max(axis=-1, keepdims=True))
    l = None
    acc = None
    for s, vb in zip(score_parts, value_parts):
        p = jnp.exp2(s - m)
        ps = jnp.sum(p, axis=-1, keepdims=True)
        pv = _dot(p.astype(BF16), vb)
        l = ps if l is None else l + ps
        acc = pv if acc is None else acc + pv
    return acc / l


def _mla_kernel(q_ref, k_ref, v_ref, o_ref):
    t = MLA_TQ
    seq = q_ref.shape[1]
    lane = lax.broadcasted_iota(jnp.int32, (1, LANES), 1)
    qry_chunk = lax.broadcasted_iota(jnp.int32, (t, 1), 0) // CHUNK
    key_chunk = lax.broadcasted_iota(jnp.int32, (1, t), 1) // CHUNK
    diag_mask = key_chunk <= qry_chunk

    def scores(i, hh):
        lo = i * t
        hs = slice(hh * LANES, (hh + 1) * LANES)
        q = q_ref[0, lo:lo + t, hs]
        parts = [jnp.where(diag_mask, _dot_nt(q, k_ref[0, lo:lo + t, hs]), NEG_INF)]
        if i > 0:
            parts.insert(0, _dot_nt(q, k_ref[0, 0:lo, hs]))
        return parts

    def finish(i, parts):
        lo = i * t
        vals = [v_ref[0, lo:lo + t, :]]
        if i > 0:
            vals.insert(0, v_ref[0, 0:lo, :])
        return _softmax_pv(parts, vals)

    def store(i, outs):
        o_ref[0, i * t:(i + 1) * t, :] = jnp.where(lane < MLA_V, outs[0], outs[1]).astype(BF16)

    _attention_pipeline([(i, hh) for i in range(seq // t) for hh in range(2)], MLA_LOOKAHEAD, scores, finish, store)


def _mla(q, k, v):
    bsz, seq, _ = q.shape
    pairs = MLA_HEADS // 2
    return pl.pallas_call(
        _mla_kernel,
        grid=(bsz, pairs),
        in_specs=[
            pl.BlockSpec((1, seq, 2 * LANES), lambda b, p: (b, 0, p)),
            pl.BlockSpec((1, seq, 2 * LANES), lambda b, p: (b, 0, p)),
            pl.BlockSpec((1, seq, LANES), lambda b, p: (b, 0, p)),
        ],
        out_specs=pl.BlockSpec((1, seq, LANES), lambda b, p: (b, 0, p)),
        out_shape=jax.ShapeDtypeStruct((bsz, seq, MLA_HEADS * MLA_V), BF16),
        compiler_params=_compiler_params(("parallel", "parallel")),
        name="mla",
    )(q, k, v)


def _ca_kernel(q_ref, k_ref, vt_ref, bm_ref, o_ref):
    tq, win = CA_TQ, CA_WIN
    seq = q_ref.shape[1]
    lane = lax.broadcasted_iota(jnp.int32, (1, LANES), 1)

    def window(i):
        hi = (i + 1) * tq
        k_lo = max(hi - win, 0)
        return k_lo, hi, k_lo - (hi - win)

    def scores(i, hh):
        k_lo, hi, c_lo = window(i)
        q = q_ref[0, i * tq:hi, :]
        head_lanes = (lane < CA_HEAD_DIM) if hh == 0 else (lane >= CA_HEAD_DIM)
        qm = jnp.where(head_lanes, q, jnp.zeros_like(q))
        return [_dot_nt(k_ref[0, k_lo:hi, :], qm) + bm_ref[hh, c_lo:, :]]

    def finish(i, parts):
        k_lo, hi, _ = window(i)
        return _softmax_pv_t(parts, [vt_ref[0, :, k_lo:hi]])

    def store(i, outs):
        _store_pair(o_ref, i * tq, (i + 1) * tq, outs, CA_HEAD_DIM)

    _attention_pipeline([(i, hh) for i in range(seq // tq) for hh in range(2)], CA_LOOKAHEAD, scores, finish, store)


def _ca(cq, ck, cvt, bias_mask):
    bsz, seq, _ = cq.shape
    pairs = CA_HEADS // 2
    tok = lambda b, p: (b, 0, p)
    return pl.pallas_call(
        _ca_kernel,
        grid=(bsz, pairs),
        in_specs=[
            pl.BlockSpec((1, seq, LANES), tok),
            pl.BlockSpec((1, seq, LANES), tok),
            pl.BlockSpec((1, LANES, seq), lambda b, p: (b, p, 0)),
            pl.BlockSpec((2, CA_WIN, CA_TQ), lambda b, p: (p, 0, 0)),
        ],
        out_specs=pl.BlockSpec((1, seq, LANES), tok),
        out_shape=jax.ShapeDtypeStruct((bsz, seq, CA_HEADS * CA_HEAD_DIM), BF16),
        compiler_params=_compiler_params(("parallel", "parallel")),
        name="ca",
    )(cq, ck, cvt, bias_mask)


def _final_kernel(x_ref, xh_ref, om_ref, omh_ref, oc_ref, och_ref, mod_ref, gmlp_ref,
                  wo_ref, wu_ref, cw_ref, cb_ref, wd_ref,
                  out_ref, h2_sc, ug0_sc, ug1_sc, uv0_sc, uv1_sc, acc_sc):
    i = pl.program_id(1)
    tm = FIN_TM
    halo = FIN_HALO
    mod = mod_ref[0]
    g_a, sh_m, sc_m, g_m = mod[2:3, :], mod[3:4, :], mod[4:5, :], mod[5:6, :]

    u_bufs = ((ug0_sc, uv0_sc), (ug1_sc, uv1_sc))
    rb = FIN_ROW_BLOCK
    n_rb = tm // rb

    def up_rows(r):
        return (0, halo + rb) if r == 0 else (halo + r * rb, halo + (r + 1) * rb)

    x_ext = jnp.concatenate([xh_ref[0], x_ref[0]], axis=0)
    o_ext = jnp.concatenate(
        [jnp.concatenate([omh_ref[0], och_ref[0]], axis=-1), jnp.concatenate([om_ref[0], oc_ref[0]], axis=-1)], axis=0)
    x1 = x_ext + g_a * _dot(o_ext, wo_ref[...])
    out_ref[0] = x1[halo:]
    y = x1 * lax.rsqrt(jnp.mean(x1 * x1, axis=-1, keepdims=True) + EPS) * gmlp_ref[...]
    h2 = y * (1.0 + sc_m) + sh_m
    h2_sc[:halo] = jnp.where(i > 0, h2[:halo], 0.0).astype(BF16)
    h2_sc[halo:] = h2[halo:].astype(BF16)

    def up_project(c, r):
        ug_sc, uv_sc = u_bufs[c % 2]
        lo, hi = up_rows(r)
        h2b = h2_sc[lo:hi, :]
        ug_sc[lo:hi, :] = _dot(h2b, wu_ref[:, c * FF_CHUNK:(c + 1) * FF_CHUNK])
        uv_sc[lo:hi, :] = _dot(h2b, wu_ref[:, D_FF + c * FF_CHUNK:D_FF + (c + 1) * FF_CHUNK])

    def conv(u_sc, col, r):
        cw = cw_ref[:, col:col + FF_CHUNK]
        lo = halo + r * rb
        return (u_sc[lo - 2:lo - 2 + rb, :] * cw[0:1, :] + u_sc[lo - 1:lo - 1 + rb, :] * cw[1:2, :]
                + u_sc[lo:lo + rb, :] * cw[2:3, :] + cb_ref[:, col:col + FF_CHUNK])

    def mlp_rows(c, r):
        ug_sc, uv_sc = u_bufs[c % 2]
        gcol, vcol = c * FF_CHUNK, D_FF + c * FF_CHUNK
        gate = conv(ug_sc, gcol, r)
        val = conv(uv_sc, vcol, r)
        act = (gate / (1.0 + jnp.exp(-gate)) * val).astype(BF16)
        down = _dot(act, wd_ref[gcol:gcol + FF_CHUNK, :])
        rows = slice(r * rb, (r + 1) * rb)
        if c == 0:
            acc_sc[rows, :] = down
        elif c < N_FF_CHUNKS - 1:
            acc_sc[rows, :] += down
        else:
            out_ref[0, rows, :] += g_m * (acc_sc[rows, :] + down)

    for r in range(n_rb):
        up_project(0, r)
    for c in range(N_FF_CHUNKS):
        for r in range(n_rb):
            if c + 1 < N_FF_CHUNKS:
                up_project(c + 1, r)
            mlp_rows(c, r)


def _final(x, o_mla, o_ca, mod, g_mlp, w_out, w_up, conv_w, conv_b, w_down):
    bsz, seq, _ = x.shape
    tm, halo = FIN_TM, FIN_HALO
    tok = lambda b, i: (b, i, 0)
    prev = lambda b, i: (b, jnp.maximum(i * (tm // halo) - 1, 0), 0)
    const2 = lambda b, i: (0, 0)
    d_mix = o_mla.shape[-1]
    resident = dict(pipeline_mode=pl.Buffered(1))
    return pl.pallas_call(
        _final_kernel,
        grid=(bsz, seq // tm),
        in_specs=[
            pl.BlockSpec((1, tm, D_MODEL), tok),
            pl.BlockSpec((1, halo, D_MODEL), prev),
            pl.BlockSpec((1, tm, d_mix), tok),
            pl.BlockSpec((1, halo, d_mix), prev),
            pl.BlockSpec((1, tm, d_mix), tok),
            pl.BlockSpec((1, halo, d_mix), prev),
            pl.BlockSpec((1, N_MOD, D_MODEL), lambda b, i: (b, 0, 0)),
            pl.BlockSpec((1, D_MODEL), const2),
            pl.BlockSpec((2 * d_mix, D_MODEL), const2, **resident),
            pl.BlockSpec((D_MODEL, 2 * D_FF), const2, **resident),
            pl.BlockSpec((3, 2 * D_FF), const2),
            pl.BlockSpec((1, 2 * D_FF), const2),
            pl.BlockSpec((D_FF, D_MODEL), const2, **resident),
        ],
        out_specs=pl.BlockSpec((1, tm, D_MODEL), tok),
        out_shape=jax.ShapeDtypeStruct((bsz, seq, D_MODEL), F32),
        scratch_shapes=[
            pltpu.VMEM((halo + tm, D_MODEL), BF16),
            pltpu.VMEM((halo + tm, FF_CHUNK), F32),
            pltpu.VMEM((halo + tm, FF_CHUNK), F32),
            pltpu.VMEM((halo + tm, FF_CHUNK), F32),
            pltpu.VMEM((halo + tm, FF_CHUNK), F32),
            pltpu.VMEM((tm, D_MODEL), F32),
        ],
        compiler_params=_compiler_params(("parallel", "arbitrary")),
        name="final",
    )(x, x, o_mla, o_mla, o_ca, o_ca, mod, g_mlp, w_out, w_up, conv_w, conv_b, w_down)


def _prep_layer(w_in, w_q_up, w_kv_up, g_mla_q, g_mla_k, g_ca_q, g_ca_k):
    half = MLA_ROPE // 2
    c0 = MLA_Q_RANK + MLA_KV_RANK
    k1 = w_in[:, c0:c0 + half]
    k2 = w_in[:, c0 + half:c0 + MLA_ROPE]
    z64 = jnp.zeros((D_MODEL, MLA_NOPE), w_in.dtype)
    c_cv = c0 + MLA_ROPE + 2 * CA_HEADS * CA_HEAD_DIM
    w_in_ext = jnp.concatenate(
        [w_in[:, :c0], z64, k1, k2, k1, k2, z64, k2, k1, k2, k1, w_in[:, c0 + MLA_ROPE:c_cv]], axis=1).astype(BF16)
    w_cvt = w_in[:, c_cv:].T.astype(BF16)

    wq = w_q_up.reshape(MLA_Q_RANK, MLA_HEADS, MLA_QK)
    x1 = wq[..., MLA_NOPE:MLA_NOPE + half]
    x2 = wq[..., MLA_NOPE + half:]
    w_q_ext = jnp.concatenate([wq, x2, x1], axis=-1).reshape(MLA_Q_RANK, MLA_HEADS * LANES).astype(BF16)

    wkv = w_kv_up.reshape(MLA_KV_RANK, MLA_HEADS, MLA_NOPE + MLA_V)
    wk = jnp.concatenate([wkv[..., :MLA_NOPE], jnp.zeros((MLA_KV_RANK, MLA_HEADS, LANES - MLA_NOPE), wkv.dtype)], axis=-1)
    w_kv_ext = jnp.concatenate(
        [wk.reshape(MLA_KV_RANK, MLA_HEADS * LANES), wkv[..., MLA_NOPE:].reshape(MLA_KV_RANK, MLA_HEADS * MLA_V)],
        axis=1).astype(BF16)

    gq1, gq2 = g_mla_q[MLA_NOPE:MLA_NOPE + half], g_mla_q[MLA_NOPE + half:]
    gk1, gk2 = g_mla_k[MLA_NOPE:MLA_NOPE + half], g_mla_k[MLA_NOPE + half:]
    z = jnp.zeros((MLA_NOPE,), F32)
    rows = [
        jnp.concatenate([g_mla_q, gq2, gq1]) * (MLA_QK ** -0.5 * LOG2E),
        jnp.concatenate([g_mla_k[:MLA_NOPE], z]),
        jnp.concatenate([z, gk1, gk2, gk1, gk2]),
        jnp.concatenate([z, gk2, gk1, gk2, gk1]),
        jnp.concatenate([g_ca_q, g_ca_q]) * (CA_HEAD_DIM ** -0.5 * LOG2E),
        jnp.concatenate([g_ca_k, g_ca_k]),
        jnp.zeros((LANES,), F32),
        jnp.zeros((LANES,), F32),
    ]
    gvec = jnp.stack(rows)
    return w_in_ext, w_q_ext, w_kv_ext, w_cvt, gvec


def kernel(x, c, positions, w_ada, b_ada, g_attn_norm, w_in, g_q_latent, g_kv_latent, w_q_up, w_kv_up, g_mla_q, g_mla_k, g_ca_q, g_ca_k, rel_bias, w_out, g_mlp_norm, w_up, conv_w, conv_b, w_down):
    bsz, seq, _ = x.shape
    depth = w_ada.shape[0]

    tc, ts = _rope_tables(positions)

    for l in range(depth):
        w_in_ext, w_q_ext, w_kv_ext, w_cvt, gvec = _prep_layer(
            w_in[l], w_q_up[l], w_kv_up[l], g_mla_q[l], g_mla_k[l], g_ca_q[l], g_ca_k[l])
        mod = _ada(c, w_ada[l], b_ada[l]).reshape(bsz, N_MOD, D_MODEL)
        bias_mask = _bias_tiles(rel_bias[l])
        q, k, v, cq, ck, cvt = _proj(
            x, mod, tc, ts, g_attn_norm[l].reshape(1, -1), g_q_latent[l].reshape(1, -1),
            g_kv_latent[l].reshape(1, -1), gvec, w_in_ext, w_q_ext, w_kv_ext, w_cvt)
        o_mla = _mla(q, k, v)
        o_ca = _ca(cq, ck, cvt, bias_mask)
        x = _final(x, o_mla, o_ca, mod, g_mlp_norm[l].reshape(1, -1), w_out[l].astype(BF16),
                   w_up[l].astype(BF16), conv_w[l], conv_b[l].reshape(1, -1), w_down[l].astype(BF16))
    return x
```

```python
import math

import numpy as np
import jax
import jax.numpy as jnp
from jax import lax
from jax.experimental import pallas as pl
from jax.experimental.pallas import tpu as pltpu

D_MODEL = 1024
CHUNK = 64
LEFT_CHUNKS = 8
MLA_HEADS = 8
MLA_Q_RANK = 256
MLA_KV_RANK = 128
MLA_NOPE = 64
MLA_ROPE = 32
MLA_QK = MLA_NOPE + MLA_ROPE
MLA_V = 64
ROPE_THETA = 10000.0
CA_HEADS = 8
CA_HEAD_DIM = 64
REL_CLIP = 128
D_FF = 2816
N_MOD = 6
EPS = 1e-6
NEG_INF = -1e30
LOG2E = math.log2(math.e)

LANES = 128
BF16_SUBLANES = 16
VMEM_LIMIT_BYTES = 56 * 1024 * 1024

PROJ_TM = 1024
MLA_TQ = 256
CA_TQ = 256
CA_WIN = CA_TQ + LEFT_CHUNKS * CHUNK
CA_ROLL_W = 1024
REDUCE_ROWS = 64
MLA_LOOKAHEAD = 3
CA_LOOKAHEAD = 2
FIN_TM = 1024
FIN_HALO = BF16_SUBLANES
FIN_ROW_BLOCK = 256
FF_CHUNK = 256
N_FF_CHUNKS = D_FF // FF_CHUNK

_C_QLAT = 0
_C_KVLAT = _C_QLAT + MLA_Q_RANK
_C_KA = _C_KVLAT + MLA_KV_RANK
_C_KB = _C_KA + LANES
_C_CQ = _C_KB + LANES
_C_CK = _C_CQ + CA_HEADS * CA_HEAD_DIM
D_IN_EXT = _C_CK + CA_HEADS * CA_HEAD_DIM

F32 = jnp.float32
BF16 = jnp.bfloat16


def _dot(a, b):
    return jnp.dot(a, b, preferred_element_type=F32)


def _dot_nt(a, b):
    return lax.dot_general(a, b, (((1,), (1,)), ((), ())), preferred_element_type=F32)


def _compiler_params(semantics):
    return pltpu.CompilerParams(dimension_semantics=semantics, vmem_limit_bytes=VMEM_LIMIT_BYTES)


def _ada_kernel(c_ref, w_ref, b_ref, o_ref):
    c = c_ref[...]
    s = c / (1.0 + jnp.exp(-c))
    o_ref[...] = _dot(s.astype(BF16), w_ref[...].astype(BF16)) + b_ref[...]


def _ada(c, w_ada, b_ada):
    bsz = c.shape[0]
    n_out = w_ada.shape[1]
    tn = D_MODEL
    return pl.pallas_call(
        _ada_kernel,
        grid=(n_out // tn,),
        in_specs=[
            pl.BlockSpec((bsz, D_MODEL), lambda j: (0, 0)),
            pl.BlockSpec((D_MODEL, tn), lambda j: (0, j)),
            pl.BlockSpec((1, tn), lambda j: (0, j)),
        ],
        out_specs=pl.BlockSpec((bsz, tn), lambda j: (0, j)),
        out_shape=jax.ShapeDtypeStruct((bsz, n_out), F32),
        compiler_params=_compiler_params(("arbitrary",)),
        name="ada",
    )(c, w_ada, b_ada.reshape(1, n_out))


def _expand(d, e_ref):
    hi = d.astype(BF16)
    r1 = d - hi.astype(F32)
    mid = r1.astype(BF16)
    lo = (r1 - mid.astype(F32)).astype(BF16)
    e = e_ref[...]
    return _dot(hi, e) + _dot(mid, e) + _dot(lo, e)


def _rope_kernel(pos_ref, rep_ref, inv_ref, ec_ref, es_ref, base_ref, tc_ref, ts_ref):
    rows = pos_ref.shape[0]
    tok_per_row = ec_ref.shape[1] // LANES
    pos = _expand(pos_ref[...].astype(F32), rep_ref)
    ang = pos * inv_ref[...]
    tc_wide = _expand(jnp.cos(ang), ec_ref) + base_ref[...]
    ts_wide = _expand(jnp.sin(ang), es_ref)
    for t in range(tok_per_row):
        tc_ref[pl.ds(t, rows, stride=tok_per_row), :] = tc_wide[:, t * LANES:(t + 1) * LANES]
        ts_ref[pl.ds(t, rows, stride=tok_per_row), :] = ts_wide[:, t * LANES:(t + 1) * LANES]


def _rope_tables(positions):
    half = MLA_ROPE // 2
    bsz, seq = positions.shape
    n_tok = bsz * seq
    tok_per_row = LANES // half
    rows = n_tok // tok_per_row
    inv = jnp.power(ROPE_THETA, -jnp.arange(half, dtype=F32) / half)
    inv_t = jnp.tile(inv, tok_per_row).reshape(1, LANES)
    spread = np.repeat(np.eye(tok_per_row, dtype=np.float32), half, axis=1)

    src = np.arange(LANES)
    sel_c = np.zeros((LANES, tok_per_row * LANES), np.float32)
    sel_s = np.zeros((LANES, tok_per_row * LANES), np.float32)
    for rep, sign in enumerate((-1.0, 1.0, -1.0, 1.0)):
        dst = (src // half) * LANES + MLA_NOPE + rep * half + src % half
        sel_c[src, dst] = 1.0
        sel_s[src, dst] = sign
    base = np.tile((np.arange(LANES) < MLA_NOPE).astype(np.float32), tok_per_row).reshape(1, -1)

    tr = 512
    wide = tok_per_row * LANES
    const = lambda i: (0, 0)
    tc, ts = pl.pallas_call(
        _rope_kernel,
        grid=(rows // tr,),
        in_specs=[
            pl.BlockSpec((tr, tok_per_row), lambda i: (i, 0)),
            pl.BlockSpec((tok_per_row, LANES), const),
            pl.BlockSpec((1, LANES), const),
            pl.BlockSpec((LANES, wide), const),
            pl.BlockSpec((LANES, wide), const),
            pl.BlockSpec((1, wide), const),
        ],
        out_specs=[pl.BlockSpec((tr * tok_per_row, LANES), lambda i: (i, 0))] * 2,
        out_shape=[jax.ShapeDtypeStruct((n_tok, LANES), F32)] * 2,
        compiler_params=_compiler_params(("arbitrary",)),
        name="rope",
    )(positions.reshape(rows, tok_per_row), jnp.asarray(spread, BF16), inv_t,
      jnp.asarray(sel_c, BF16), jnp.asarray(sel_s, BF16), jnp.asarray(base))
    return tc.reshape(bsz, seq, LANES), ts.reshape(bsz, seq, LANES)


def _bias_kernel(y_ref, o_ref):
    tq, win = CA_TQ, CA_WIN
    full = jnp.broadcast_to(y_ref[0], (tq, CA_ROLL_W))
    rolled = pltpu.roll(full, 0, 1, stride=1, stride_axis=0)
    q_chunk = lax.broadcasted_iota(jnp.int32, (tq, 1), 0) // CHUNK
    k_chunk = lax.broadcasted_iota(jnp.int32, (1, win), 1) // CHUNK
    valid = jnp.logical_and(k_chunk >= q_chunk, k_chunk <= q_chunk + LEFT_CHUNKS)
    o_ref[0] = jnp.where(valid, rolled[:, :win] * LOG2E, NEG_INF).T


def _bias_tiles(rel_bias):
    n_heads = rel_bias.shape[0]
    t = jnp.arange(CA_ROLL_W)
    d = jnp.where(t < CA_WIN, t, t - CA_ROLL_W)
    idx = jnp.clip(LEFT_CHUNKS * CHUNK - d, -REL_CLIP, REL_CLIP) + REL_CLIP
    y = rel_bias[:, idx].reshape(n_heads, 1, CA_ROLL_W)
    return pl.pallas_call(
        _bias_kernel,
        grid=(n_heads,),
        in_specs=[pl.BlockSpec((1, 1, CA_ROLL_W), lambda h: (h, 0, 0))],
        out_specs=pl.BlockSpec((1, CA_WIN, CA_TQ), lambda h: (h, 0, 0)),
        out_shape=jax.ShapeDtypeStruct((n_heads, CA_WIN, CA_TQ), F32),
        compiler_params=_compiler_params(("arbitrary",)),
        name="bias",
    )(y)


def _proj_kernel(x_ref, mod_ref, tc_ref, ts_ref, gattn_ref, gql_ref, gkvl_ref, gv_ref,
                 win_ref, wq_ref, wkv_ref, wcvt_ref,
                 q_ref, k_ref, v_ref, cq_ref, ck_ref, cvt_ref):
    x = x_ref[0]
    mod = mod_ref[0]
    sh, sc = mod[0:1, :], mod[1:2, :]
    y = x * lax.rsqrt(jnp.mean(x * x, axis=-1, keepdims=True) + EPS) * gattn_ref[...]
    h = (y * (1.0 + sc) + sh).astype(BF16)
    proj = _dot(h, win_ref[...])

    lane = lax.broadcasted_iota(jnp.int32, (1, LANES), 1)
    gv = gv_ref[...]
    g_q, g_kn, g_ka, g_kb, g_cq, g_ck = (gv[r:r + 1, :] for r in range(6))

    tc = tc_ref[0]
    ts = ts_ref[0]
    tq = jnp.where(lane < MLA_QK, tc, ts)

    ql = proj[:, _C_QLAT:_C_QLAT + MLA_Q_RANK]
    qn = (ql * lax.rsqrt(jnp.mean(ql * ql, axis=-1, keepdims=True) + EPS) * gql_ref[...]).astype(BF16)
    qp = _dot(qn, wq_ref[...])
    qmul = g_q * tq
    for hd in range(MLA_HEADS):
        qh = qp[:, hd * LANES:(hd + 1) * LANES]
        ss = jnp.sum(jnp.where(lane < MLA_QK, qh * qh, 0.0), axis=-1, keepdims=True)
        r = lax.rsqrt(ss * (1.0 / MLA_QK) + EPS)
        q_ref[0, :, hd * LANES:(hd + 1) * LANES] = (qh * r * qmul).astype(BF16)

    kvl = proj[:, _C_KVLAT:_C_KVLAT + MLA_KV_RANK]
    kvn = (kvl * lax.rsqrt(jnp.mean(kvl * kvl, axis=-1, keepdims=True) + EPS) * gkvl_ref[...]).astype(BF16)
    kvp = _dot(kvn, wkv_ref[...])
    ka = proj[:, _C_KA:_C_KA + LANES]
    kb = proj[:, _C_KB:_C_KB + LANES]
    krot = ka * g_ka * tc + kb * g_kb * ts
    ss_rope = jnp.sum(jnp.where(lane < MLA_QK, ka * ka, 0.0), axis=-1, keepdims=True)
    for hd in range(MLA_HEADS):
        kh = kvp[:, hd * LANES:(hd + 1) * LANES]
        ss = jnp.sum(kh * kh, axis=-1, keepdims=True) + ss_rope
        r = lax.rsqrt(ss * (1.0 / MLA_QK) + EPS)
        k_ref[0, :, hd * LANES:(hd + 1) * LANES] = ((kh * g_kn + krot) * r).astype(BF16)
    v_ref[0] = kvp[:, MLA_HEADS * LANES:].astype(BF16)

    lo = lane < CA_HEAD_DIM
    for src, gain, dst in ((_C_CQ, g_cq, cq_ref), (_C_CK, g_ck, ck_ref)):
        for p in range(CA_HEADS // 2):
            xx = proj[:, src + p * LANES:src + (p + 1) * LANES]
            x2 = xx * xx
            s_all = jnp.sum(x2, axis=-1, keepdims=True)
            s_lo = jnp.sum(jnp.where(lo, x2, 0.0), axis=-1, keepdims=True)
            r_lo = lax.rsqrt(s_lo * (1.0 / CA_HEAD_DIM) + EPS)
            r_hi = lax.rsqrt((s_all - s_lo) * (1.0 / CA_HEAD_DIM) + EPS)
            dst[0, :, p * LANES:(p + 1) * LANES] = (xx * jnp.where(lo, r_lo, r_hi) * gain).astype(BF16)
    cvt_ref[0] = _dot_nt(wcvt_ref[...], h).astype(BF16)


def _proj(x, mod, tc, ts, g_attn, g_ql, g_kvl, gvec, w_in_ext, w_q_ext, w_kv_ext, w_cvt):
    bsz, seq, _ = x.shape
    tm = PROJ_TM
    tok = lambda b, i: (b, i, 0)
    tok_t = lambda b, i: (b, 0, i)
    const2 = lambda b, i: (0, 0)
    wide = MLA_HEADS * LANES
    narrow = CA_HEADS * CA_HEAD_DIM
    row_major = lambda w: (jax.ShapeDtypeStruct((bsz, seq, w), BF16), pl.BlockSpec((1, tm, w), tok))
    feat_major = lambda w: (jax.ShapeDtypeStruct((bsz, w, seq), BF16), pl.BlockSpec((1, w, tm), tok_t))
    outs = [row_major(wide), row_major(wide), row_major(MLA_HEADS * MLA_V),
            row_major(narrow), row_major(narrow), feat_major(narrow)]
    out_shapes = [o[0] for o in outs]
    out_specs = [o[1] for o in outs]
    return pl.pallas_call(
        _proj_kernel,
        grid=(bsz, seq // tm),
        in_specs=[
            pl.BlockSpec((1, tm, D_MODEL), tok),
            pl.BlockSpec((1, N_MOD, D_MODEL), lambda b, i: (b, 0, 0)),
            pl.BlockSpec((1, tm, LANES), tok),
            pl.BlockSpec((1, tm, LANES), tok),
            pl.BlockSpec((1, D_MODEL), const2),
            pl.BlockSpec((1, MLA_Q_RANK), const2),
            pl.BlockSpec((1, MLA_KV_RANK), const2),
            pl.BlockSpec((8, LANES), const2),
            pl.BlockSpec((D_MODEL, D_IN_EXT), const2),
            pl.BlockSpec((MLA_Q_RANK, wide), const2),
            pl.BlockSpec((MLA_KV_RANK, wide + MLA_HEADS * MLA_V), const2),
            pl.BlockSpec((narrow, D_MODEL), const2),
        ],
        out_specs=out_specs,
        out_shape=out_shapes,
        compiler_params=_compiler_params(("parallel", "parallel")),
        name="proj",
    )(x, mod, tc, ts, g_attn, g_ql, g_kvl, gvec, w_in_ext, w_q_ext, w_kv_ext, w_cvt)


def _softmax_pv_t(score_parts, value_parts):
    def fold(x, op):
        return op(x.reshape(x.shape[0] // REDUCE_ROWS, REDUCE_ROWS, x.shape[1]), axis=0)

    m = fold(score_parts[0], jnp.max)
    for s in score_parts[1:]:
        m = jnp.maximum(m, fold(s, jnp.max))
    m = m.max(axis=0, keepdims=True)
    l = None
    acc = None
    for s, vt in zip(score_parts, value_parts):
        p = jnp.exp2(s - m)
        ps = fold(p, jnp.sum).sum(axis=0, keepdims=True)
        pv = _dot(vt, p.astype(BF16))
        l = ps if l is None else l + ps
        acc = pv if acc is None else acc + pv
    return acc / l


def _attention_pipeline(units, lookahead, scores, finish, store):
    pending = [scores(*u) for u in units[:lookahead]]
    outs = []
    for n, (i, hh) in enumerate(units):
        if n + lookahead < len(units):
            pending.append(scores(*units[n + lookahead]))
        outs.append(finish(i, pending.pop(0)))
        if hh == 1:
            store(i, outs)
            outs = []


def _store_pair(o_ref, lo, hi, outs, head_rows):
    row = lax.broadcasted_iota(jnp.int32, (LANES, 1), 0)
    o_t = jnp.where(row < head_rows, outs[0], outs[1])
    o_ref[0, lo:hi, :] = o_t.T.astype(BF16)


def _softmax_pv(score_parts, value_parts):
    m = score_parts[0].max(axis=-1, keepdims=True)
    for s in score_parts[1:]:
        m = jnp.maximum(m, s.max(axis=-1, keepdims=True))
    l = None
    acc = None
    for s, vb in zip(score_parts, value_parts):
        p = jnp.exp2(s - m)
        ps = jnp.sum(p, axis=-1, keepdims=True)
        pv = _dot(p.astype(BF16), vb)
        l = ps if l is None else l + ps
        acc = pv if acc is None else acc + pv
    return acc / l


def _mla_kernel(q_ref, k_ref, v_ref, o_ref):
    t = MLA_TQ
    seq = q_ref.shape[1]
    lane = lax.broadcasted_iota(jnp.int32, (1, LANES), 1)
    qry_chunk = lax.broadcasted_iota(jnp.int32, (t, 1), 0) // CHUNK
    key_chunk = lax.broadcasted_iota(jnp.int32, (1, t), 1) // CHUNK
    diag_mask = key_chunk <= qry_chunk

    def scores(i, hh):
        lo = i * t
        hs = slice(hh * LANES, (hh + 1) * LANES)
        q = q_ref[0, lo:lo + t, hs]
        parts = [jnp.where(diag_mask, _dot_nt(q, k_ref[0, lo:lo + t, hs]), NEG_INF)]
        if i > 0:
            parts.insert(0, _dot_nt(q, k_ref[0, 0:lo, hs]))
        return parts

    def finish(i, parts):
        lo = i * t
        vals = [v_ref[0, lo:lo + t, :]]
        if i > 0:
            vals.insert(0, v_ref[0, 0:lo, :])
        return _softmax_pv(parts, vals)

    def store(i, outs):
        o_ref[0, i * t:(i + 1) * t, :] = jnp.where(lane < MLA_V, outs[0], outs[1]).astype(BF16)

    _attention_pipeline([(i, hh) for i in range(seq // t) for hh in range(2)], MLA_LOOKAHEAD, scores, finish, store)


def _mla(q, k, v):
    bsz, seq, _ = q.shape
    pairs = MLA_HEADS // 2
    return pl.pallas_call(
        _mla_kernel,
        grid=(bsz, pairs),
        in_specs=[
            pl.BlockSpec((1, seq, 2 * LANES), lambda b, p: (b, 0, p)),
            pl.BlockSpec((1, seq, 2 * LANES), lambda b, p: (b, 0, p)),
            pl.BlockSpec((1, seq, LANES), lambda b, p: (b, 0, p)),
        ],
        out_specs=pl.BlockSpec((1, seq, LANES), lambda b, p: (b, 0, p)),
        out_shape=jax.ShapeDtypeStruct((bsz, seq, MLA_HEADS * MLA_V), BF16),
        compiler_params=_compiler_params(("parallel", "parallel")),
        name="mla",
    )(q, k, v)


def _ca_kernel(q_ref, k_ref, vt_ref, bm_ref, o_ref):
    tq, win = CA_TQ, CA_WIN
    seq = q_ref.shape[1]
    lane = lax.broadcasted_iota(jnp.int32, (1, LANES), 1)

    def window(i):
        hi = (i + 1) * tq
        k_lo = max(hi - win, 0)
        return k_lo, hi, k_lo - (hi - win)

    def scores(i, hh):
        k_lo, hi, c_lo = window(i)
        q = q_ref[0, i * tq:hi, :]
        head_lanes = (lane < CA_HEAD_DIM) if hh == 0 else (lane >= CA_HEAD_DIM)
        qm = jnp.where(head_lanes, q, jnp.zeros_like(q))
        return [_dot_nt(k_ref[0, k_lo:hi, :], qm) + bm_ref[hh, c_lo:, :]]

    def finish(i, parts):
        k_lo, hi, _ = window(i)
        return _softmax_pv_t(parts, [vt_ref[0, :, k_lo:hi]])

    def store(i, outs):
        _store_pair(o_ref, i * tq, (i + 1) * tq, outs, CA_HEAD_DIM)

    _attention_pipeline([(i, hh) for i in range(seq // tq) for hh in range(2)], CA_LOOKAHEAD, scores, finish, store)


def _ca(cq, ck, cvt, bias_mask):
    bsz, seq, _ = cq.shape
    pairs = CA_HEADS // 2
    tok = lambda b, p: (b, 0, p)
    return pl.pallas_call(
        _ca_kernel,
        grid=(bsz, pairs),
        in_specs=[
            pl.BlockSpec((1, seq, LANES), tok),
            pl.BlockSpec((1, seq, LANES), tok),
            pl.BlockSpec((1, LANES, seq), lambda b, p: (b, p, 0)),
            pl.BlockSpec((2, CA_WIN, CA_TQ), lambda b, p: (p, 0, 0)),
        ],
        out_specs=pl.BlockSpec((1, seq, LANES), tok),
        out_shape=jax.ShapeDtypeStruct((bsz, seq, CA_HEADS * CA_HEAD_DIM), BF16),
        compiler_params=_compiler_params(("parallel", "parallel")),
        name="ca",
    )(cq, ck, cvt, bias_mask)


def _final_kernel(x_ref, xh_ref, om_ref, omh_ref, oc_ref, och_ref, mod_ref, gmlp_ref,
                  wo_ref, wu_ref, cw_ref, cb_ref, wd_ref,
                  out_ref, h2_sc, ug0_sc, ug1_sc, uv0_sc, uv1_sc, acc_sc):
    i = pl.program_id(1)
    tm = FIN_TM
    halo = FIN_HALO
    mod = mod_ref[0]
    g_a, sh_m, sc_m, g_m = mod[2:3, :], mod[3:4, :], mod[4:5, :], mod[5:6, :]

    u_bufs = ((ug0_sc, uv0_sc), (ug1_sc, uv1_sc))
    rb = FIN_ROW_BLOCK
    n_rb = tm // rb

    def up_rows(r):
        return (0, halo + rb) if r == 0 else (halo + r * rb, halo + (r + 1) * rb)

    x_ext = jnp.concatenate([xh_ref[0], x_ref[0]], axis=0)
    o_ext = jnp.concatenate(
        [jnp.concatenate([omh_ref[0], och_ref[0]], axis=-1), jnp.concatenate([om_ref[0], oc_ref[0]], axis=-1)], axis=0)
    x1 = x_ext + g_a * _dot(o_ext, wo_ref[...])
    out_ref[0] = x1[halo:]
    y = x1 * lax.rsqrt(jnp.mean(x1 * x1, axis=-1, keepdims=True) + EPS) * gmlp_ref[...]
    h2 = y * (1.0 + sc_m) + sh_m
    h2_sc[:halo] = jnp.where(i > 0, h2[:halo], 0.0).astype(BF16)
    h2_sc[halo:] = h2[halo:].astype(BF16)

    def up_project(c, r):
        ug_sc, uv_sc = u_bufs[c % 2]
        lo, hi = up_rows(r)
        h2b = h2_sc[lo:hi, :]
        ug_sc[lo:hi, :] = _dot(h2b, wu_ref[:, c * FF_CHUNK:(c + 1) * FF_CHUNK])
        uv_sc[lo:hi, :] = _dot(h2b, wu_ref[:, D_FF + c * FF_CHUNK:D_FF + (c + 1) * FF_CHUNK])

    def conv(u_sc, col, r):
        cw = cw_ref[:, col:col + FF_CHUNK]
        lo = halo + r * rb
        return (u_sc[lo - 2:lo - 2 + rb, :] * cw[0:1, :] + u_sc[lo - 1:lo - 1 + rb, :] * cw[1:2, :]
                + u_sc[lo:lo + rb, :] * cw[2:3, :] + cb_ref[:, col:col + FF_CHUNK])

    def mlp_rows(c, r):
        ug_sc, uv_sc = u_bufs[c % 2]
        gcol, vcol = c * FF_CHUNK, D_FF + c * FF_CHUNK
        gate = conv(ug_sc, gcol, r)
        val = conv(uv_sc, vcol, r)
        act = (gate / (1.0 + jnp.exp(-gate)) * val).astype(BF16)
        down = _dot(act, wd_ref[gcol:gcol + FF_CHUNK, :])
        rows = slice(r * rb, (r + 1) * rb)
        if c == 0:
            acc_sc[rows, :] = down
        elif c < N_FF_CHUNKS - 1:
            acc_sc[rows, :] += down
        else:
            out_ref[0, rows, :] += g_m * (acc_sc[rows, :] + down)

    for r in range(n_rb):
        up_project(0, r)
    for c in range(N_FF_CHUNKS):
        for r in range(n_rb):
            if c + 1 < N_FF_CHUNKS:
                up_project(c + 1, r)
            mlp_rows(c, r)


def _final(x, o_mla, o_ca, mod, g_mlp, w_out, w_up, conv_w, conv_b, w_down):
    bsz, seq, _ = x.shape
    tm, halo = FIN_TM, FIN_HALO
    tok = lambda b, i: (b, i, 0)
    prev = lambda b, i: (b, jnp.maximum(i * (tm // halo) - 1, 0), 0)
    const2 = lambda b, i: (0, 0)
    d_mix = o_mla.shape[-1]
    resident = dict(pipeline_mode=pl.Buffered(1))
    return pl.pallas_call(
        _final_kernel,
        grid=(bsz, seq // tm),
        in_specs=[
            pl.BlockSpec((1, tm, D_MODEL), tok),
            pl.BlockSpec((1, halo, D_MODEL), prev),
            pl.BlockSpec((1, tm, d_mix), tok),
            pl.BlockSpec((1, halo, d_mix), prev),
            pl.BlockSpec((1, tm, d_mix), tok),
            pl.BlockSpec((1, halo, d_mix), prev),
            pl.BlockSpec((1, N_MOD, D_MODEL), lambda b, i: (b, 0, 0)),
            pl.BlockSpec((1, D_MODEL), const2),
            pl.BlockSpec((2 * d_mix, D_MODEL), const2, **resident),
            pl.BlockSpec((D_MODEL, 2 * D_FF), const2, **resident),
            pl.BlockSpec((3, 2 * D_FF), const2),
            pl.BlockSpec((1, 2 * D_FF), const2),
            pl.BlockSpec((D_FF, D_MODEL), const2, **resident),
        ],
        out_specs=pl.BlockSpec((1, tm, D_MODEL), tok),
        out_shape=jax.ShapeDtypeStruct((bsz, seq, D_MODEL), F32),
        scratch_shapes=[
            pltpu.VMEM((halo + tm, D_MODEL), BF16),
            pltpu.VMEM((halo + tm, FF_CHUNK), F32),
            pltpu.VMEM((halo + tm, FF_CHUNK), F32),
            pltpu.VMEM((halo + tm, FF_CHUNK), F32),
            pltpu.VMEM((halo + tm, FF_CHUNK), F32),
            pltpu.VMEM((tm, D_MODEL), F32),
        ],
        compiler_params=_compiler_params(("parallel", "arbitrary")),
        name="final",
    )(x, x, o_mla, o_mla, o_ca, o_ca, mod, g_mlp, w_out, w_up, conv_w, conv_b, w_down)


def _prep_layer(w_in, w_q_up, w_kv_up, g_mla_q, g_mla_k, g_ca_q, g_ca_k):
    half = MLA_ROPE // 2
    c0 = MLA_Q_RANK + MLA_KV_RANK
    k1 = w_in[:, c0:c0 + half]
    k2 = w_in[:, c0 + half:c0 + MLA_ROPE]
    z64 = jnp.zeros((D_MODEL, MLA_NOPE), w_in.dtype)
    c_cv = c0 + MLA_ROPE + 2 * CA_HEADS * CA_HEAD_DIM
    w_in_ext = jnp.concatenate(
        [w_in[:, :c0], z64, k1, k2, k1, k2, z64, k2, k1, k2, k1, w_in[:, c0 + MLA_ROPE:c_cv]], axis=1).astype(BF16)
    w_cvt = w_in[:, c_cv:].T.astype(BF16)

    wq = w_q_up.reshape(MLA_Q_RANK, MLA_HEADS, MLA_QK)
    x1 = wq[..., MLA_NOPE:MLA_NOPE + half]
    x2 = wq[..., MLA_NOPE + half:]
    w_q_ext = jnp.concatenate([wq, x2, x1], axis=-1).reshape(MLA_Q_RANK, MLA_HEADS * LANES).astype(BF16)

    wkv = w_kv_up.reshape(MLA_KV_RANK, MLA_HEADS, MLA_NOPE + MLA_V)
    wk = jnp.concatenate([wkv[..., :MLA_NOPE], jnp.zeros((MLA_KV_RANK, MLA_HEADS, LANES - MLA_NOPE), wkv.dtype)], axis=-1)
    w_kv_ext = jnp.concatenate(
        [wk.reshape(MLA_KV_RANK, MLA_HEADS * LANES), wkv[..., MLA_NOPE:].reshape(MLA_KV_RANK, MLA_HEADS * MLA_V)],
        axis=1).astype(BF16)

    gq1, gq2 = g_mla_q[MLA_NOPE:MLA_NOPE + half], g_mla_q[MLA_NOPE + half:]
    gk1, gk2 = g_mla_k[MLA_NOPE:MLA_NOPE + half], g_mla_k[MLA_NOPE + half:]
    z = jnp.zeros((MLA_NOPE,), F32)
    rows = [
        jnp.concatenate([g_mla_q, gq2, gq1]) * (MLA_QK ** -0.5 * LOG2E),
        jnp.concatenate([g_mla_k[:MLA_NOPE], z]),
        jnp.concatenate([z, gk1, gk2, gk1, gk2]),
        jnp.concatenate([z, gk2, gk1, gk2, gk1]),
        jnp.concatenate([g_ca_q, g_ca_q]) * (CA_HEAD_DIM ** -0.5 * LOG2E),
        jnp.concatenate([g_ca_k, g_ca_k]),
        jnp.zeros((LANES,), F32),
        jnp.zeros((LANES,), F32),
    ]
    gvec = jnp.stack(rows)
    return w_in_ext, w_q_ext, w_kv_ext, w_cvt, gvec


def kernel(x, c, positions, w_ada, b_ada, g_attn_norm, w_in, g_q_latent, g_kv_latent, w_q_up, w_kv_up, g_mla_q, g_mla_k, g_ca_q, g_ca_k, rel_bias, w_out, g_mlp_norm, w_up, conv_w, conv_b, w_down):
    bsz, seq, _ = x.shape
    depth = w_ada.shape[0]

    tc, ts = _rope_tables(positions)

    for l in range(depth):
        w_in_ext, w_q_ext, w_kv_ext, w_cvt, gvec = _prep_layer(
            w_in[l], w_q_up[l], w_kv_up[l], g_mla_q[l], g_mla_k[l], g_ca_q[l], g_ca_k[l])
        mod = _ada(c, w_ada[l], b_ada[l]).reshape(bsz, N_MOD, D_MODEL)
        bias_mask = _bias_tiles(rel_bias[l])
        q, k, v, cq, ck, cvt = _proj(
            x, mod, tc, ts, g_attn_norm[l].reshape(1, -1), g_q_latent[l].reshape(1, -1),
            g_kv_latent[l].reshape(1, -1), gvec, w_in_ext, w_q_ext, w_kv_ext, w_cvt)
        o_mla = _mla(q, k, v)
        o_ca = _ca(cq, ck, cvt, bias_mask)
        x = _final(x, o_mla, o_ca, mod, g_mlp_norm[l].reshape(1, -1), w_out[l].astype(BF16),
                   w_up[l].astype(BF16), conv_w[l], conv_b[l].reshape(1, -1), w_down[l].astype(BF16))
    return x
```

```python
import math

import numpy as np
import jax
import jax.numpy as jnp
from jax import lax
from jax.experimental import pallas as pl
from jax.experimental.pallas import tpu as pltpu

D_MODEL = 1024
CHUNK = 64
LEFT_CHUNKS = 8
MLA_HEADS = 8
MLA_Q_RANK = 256
MLA_KV_RANK = 128
MLA_NOPE = 64
MLA_ROPE = 32
MLA_QK = MLA_NOPE + MLA_ROPE
MLA_V = 64
ROPE_THETA = 10000.0
CA_HEADS = 8
CA_HEAD_DIM = 64
REL_CLIP = 128
D_FF = 2816
N_MOD = 6
EPS = 1e-6
NEG_INF = -1e30
LOG2E = math.log2(math.e)

LANES = 128
BF16_SUBLANES = 16
VMEM_LIMIT_BYTES = 56 * 1024 * 1024

PROJ_TM = 1024
MLA_TQ = 256
CA_TQ = 256
CA_WIN = CA_TQ + LEFT_CHUNKS * CHUNK
CA_ROLL_W = 1024
REDUCE_ROWS = 64
MLA_LOOKAHEAD = 3
CA_LOOKAHEAD = 2
FIN_TM = 512
FIN_HALO = BF16_SUBLANES
FIN_ROW_BLOCK = 128
FF_CHUNK = 256
N_FF_CHUNKS = D_FF // FF_CHUNK

_C_QLAT = 0
_C_KVLAT = _C_QLAT + MLA_Q_RANK
_C_KA = _C_KVLAT + MLA_KV_RANK
_C_KB = _C_KA + LANES
_C_CQ = _C_KB + LANES
_C_CK = _C_CQ + CA_HEADS * CA_HEAD_DIM
D_IN_EXT = _C_CK + CA_HEADS * CA_HEAD_DIM

F32 = jnp.float32
BF16 = jnp.bfloat16


def _dot(a, b):
    return jnp.dot(a, b, preferred_element_type=F32)


def _dot_nt(a, b):
    return lax.dot_general(a, b, (((1,), (1,)), ((), ())), preferred_element_type=F32)


def _compiler_params(semantics):
    return pltpu.CompilerParams(dimension_semantics=semantics, vmem_limit_bytes=VMEM_LIMIT_BYTES)


def _ada_kernel(c_ref, w_ref, b_ref, o_ref):
    c = c_ref[...]
    s = c / (1.0 + jnp.exp(-c))
    o_ref[...] = _dot(s.astype(BF16), w_ref[...].astype(BF16)) + b_ref[...]


def _ada(c, w_ada, b_ada):
    bsz = c.shape[0]
    n_out = w_ada.shape[1]
    tn = D_MODEL
    return pl.pallas_call(
        _ada_kernel,
        grid=(n_out // tn,),
        in_specs=[
            pl.BlockSpec((bsz, D_MODEL), lambda j: (0, 0)),
            pl.BlockSpec((D_MODEL, tn), lambda j: (0, j)),
            pl.BlockSpec((1, tn), lambda j: (0, j)),
        ],
        out_specs=pl.BlockSpec((bsz, tn), lambda j: (0, j)),
        out_shape=jax.ShapeDtypeStruct((bsz, n_out), F32),
        compiler_params=_compiler_params(("arbitrary",)),
        name="ada",
    )(c, w_ada, b_ada.reshape(1, n_out))


def _expand(d, e_ref):
    hi = d.astype(BF16)
    r1 = d - hi.astype(F32)
    mid = r1.astype(BF16)
    lo = (r1 - mid.astype(F32)).astype(BF16)
    e = e_ref[...]
    return _dot(hi, e) + _dot(mid, e) + _dot(lo, e)


def _rope_kernel(pos_ref, rep_ref, inv_ref, ec_ref, es_ref, base_ref, tc_ref, ts_ref):
    rows = pos_ref.shape[0]
    tok_per_row = ec_ref.shape[1] // LANES
    pos = _expand(pos_ref[...].astype(F32), rep_ref)
    ang = pos * inv_ref[...]
    tc_wide = _expand(jnp.cos(ang), ec_ref) + base_ref[...]
    ts_wide = _expand(jnp.sin(ang), es_ref)
    for t in range(tok_per_row):
        tc_ref[pl.ds(t, rows, stride=tok_per_row), :] = tc_wide[:, t * LANES:(t + 1) * LANES]
        ts_ref[pl.ds(t, rows, stride=tok_per_row), :] = ts_wide[:, t * LANES:(t + 1) * LANES]


def _rope_tables(positions):
    half = MLA_ROPE // 2
    bsz, seq = positions.shape
    n_tok = bsz * seq
    tok_per_row = LANES // half
    rows = n_tok // tok_per_row
    inv = jnp.power(ROPE_THETA, -jnp.arange(half, dtype=F32) / half)
    inv_t = jnp.tile(inv, tok_per_row).reshape(1, LANES)
    spread = np.repeat(np.eye(tok_per_row, dtype=np.float32), half, axis=1)

    src = np.arange(LANES)
    sel_c = np.zeros((LANES, tok_per_row * LANES), np.float32)
    sel_s = np.zeros((LANES, tok_per_row * LANES), np.float32)
    for rep, sign in enumerate((-1.0, 1.0, -1.0, 1.0)):
        dst = (src // half) * LANES + MLA_NOPE + rep * half + src % half
        sel_c[src, dst] = 1.0
        sel_s[src, dst] = sign
    base = np.tile((np.arange(LANES) < MLA_NOPE).astype(np.float32), tok_per_row).reshape(1, -1)

    tr = 512
    wide = tok_per_row * LANES
    const = lambda i: (0, 0)
    tc, ts = pl.pallas_call(
        _rope_kernel,
        grid=(rows // tr,),
        in_specs=[
            pl.BlockSpec((tr, tok_per_row), lambda i: (i, 0)),
            pl.BlockSpec((tok_per_row, LANES), const),
            pl.BlockSpec((1, LANES), const),
            pl.BlockSpec((LANES, wide), const),
            pl.BlockSpec((LANES, wide), const),
            pl.BlockSpec((1, wide), const),
        ],
        out_specs=[pl.BlockSpec((tr * tok_per_row, LANES), lambda i: (i, 0))] * 2,
        out_shape=[jax.ShapeDtypeStruct((n_tok, LANES), F32)] * 2,
        compiler_params=_compiler_params(("arbitrary",)),
        name="rope",
    )(positions.reshape(rows, tok_per_row), jnp.asarray(spread, BF16), inv_t,
      jnp.asarray(sel_c, BF16), jnp.asarray(sel_s, BF16), jnp.asarray(base))
    return tc.reshape(bsz, seq, LANES), ts.reshape(bsz, seq, LANES)


def _bias_kernel(y_ref, o_ref):
    tq, win = CA_TQ, CA_WIN
    full = jnp.broadcast_to(y_ref[0], (tq, CA_ROLL_W))
    rolled = pltpu.roll(full, 0, 1, stride=1, stride_axis=0)
    q_chunk = lax.broadcasted_iota(jnp.int32, (tq, 1), 0) // CHUNK
    k_chunk = lax.broadcasted_iota(jnp.int32, (1, win), 1) // CHUNK
    valid = jnp.logical_and(k_chunk >= q_chunk, k_chunk <= q_chunk + LEFT_CHUNKS)
    o_ref[0] = jnp.where(valid, rolled[:, :win] * LOG2E, NEG_INF).T


def _bias_tiles(rel_bias):
    n_heads = rel_bias.shape[0]
    t = jnp.arange(CA_ROLL_W)
    d = jnp.where(t < CA_WIN, t, t - CA_ROLL_W)
    idx = jnp.clip(LEFT_CHUNKS * CHUNK - d, -REL_CLIP, REL_CLIP) + REL_CLIP
    y = rel_bias[:, idx].reshape(n_heads, 1, CA_ROLL_W)
    return pl.pallas_call(
        _bias_kernel,
        grid=(n_heads,),
        in_specs=[pl.BlockSpec((1, 1, CA_ROLL_W), lambda h: (h, 0, 0))],
        out_specs=pl.BlockSpec((1, CA_WIN, CA_TQ), lambda h: (h, 0, 0)),
        out_shape=jax.ShapeDtypeStruct((n_heads, CA_WIN, CA_TQ), F32),
        compiler_params=_compiler_params(("arbitrary",)),
        name="bias",
    )(y)


def _proj_kernel(x_ref, mod_ref, tc_ref, ts_ref, gattn_ref, gql_ref, gkvl_ref, gv_ref,
                 win_ref, wq_ref, wkv_ref, wcvt_ref,
                 q_ref, k_ref, v_ref, cq_ref, ck_ref, cvt_ref):
    x = x_ref[0]
    mod = mod_ref[0]
    sh, sc = mod[0:1, :], mod[1:2, :]
    y = x * lax.rsqrt(jnp.mean(x * x, axis=-1, keepdims=True) + EPS) * gattn_ref[...]
    h = (y * (1.0 + sc) + sh).astype(BF16)
    proj = _dot(h, win_ref[...])

    lane = lax.broadcasted_iota(jnp.int32, (1, LANES), 1)
    gv = gv_ref[...]
    g_q, g_kn, g_ka, g_kb, g_cq, g_ck = (gv[r:r + 1, :] for r in range(6))

    tc = tc_ref[0]
    ts = ts_ref[0]
    tq = jnp.where(lane < MLA_QK, tc, ts)

    ql = proj[:, _C_QLAT:_C_QLAT + MLA_Q_RANK]
    qn = (ql * lax.rsqrt(jnp.mean(ql * ql, axis=-1, keepdims=True) + EPS) * gql_ref[...]).astype(BF16)
    qp = _dot(qn, wq_ref[...])
    qmul = g_q * tq
    for hd in range(MLA_HEADS):
        qh = qp[:, hd * LANES:(hd + 1) * LANES]
        ss = jnp.sum(jnp.where(lane < MLA_QK, qh * qh, 0.0), axis=-1, keepdims=True)
        r = lax.rsqrt(ss * (1.0 / MLA_QK) + EPS)
        q_ref[0, :, hd * LANES:(hd + 1) * LANES] = (qh * r * qmul).astype(BF16)

    kvl = proj[:, _C_KVLAT:_C_KVLAT + MLA_KV_RANK]
    kvn = (kvl * lax.rsqrt(jnp.mean(kvl * kvl, axis=-1, keepdims=True) + EPS) * gkvl_ref[...]).astype(BF16)
    kvp = _dot(kvn, wkv_ref[...])
    ka = proj[:, _C_KA:_C_KA + LANES]
    kb = proj[:, _C_KB:_C_KB + LANES]
    krot = ka * g_ka * tc + kb * g_kb * ts
    ss_rope = jnp.sum(jnp.where(lane < MLA_QK, ka * ka, 0.0), axis=-1, keepdims=True)
    for hd in range(MLA_HEADS):
        kh = kvp[:, hd * LANES:(hd + 1) * LANES]
        ss = jnp.sum(kh * kh, axis=-1, keepdims=True) + ss_rope
        r = lax.rsqrt(ss * (1.0 / MLA_QK) + EPS)
        k_ref[0, :, hd * LANES:(hd + 1) * LANES] = ((kh * g_kn + krot) * r).astype(BF16)
    v_ref[0] = kvp[:, MLA_HEADS * LANES:].astype(BF16)

    lo = lane < CA_HEAD_DIM
    for src, gain, dst in ((_C_CQ, g_cq, cq_ref), (_C_CK, g_ck, ck_ref)):
        for p in range(CA_HEADS // 2):
            xx = proj[:, src + p * LANES:src + (p + 1) * LANES]
            x2 = xx * xx
            s_all = jnp.sum(x2, axis=-1, keepdims=True)
            s_lo = jnp.sum(jnp.where(lo, x2, 0.0), axis=-1, keepdims=True)
            r_lo = lax.rsqrt(s_lo * (1.0 / CA_HEAD_DIM) + EPS)
            r_hi = lax.rsqrt((s_all - s_lo) * (1.0 / CA_HEAD_DIM) + EPS)
            dst[0, :, p * LANES:(p + 1) * LANES] = (xx * jnp.where(lo, r_lo, r_hi) * gain).astype(BF16)
    cvt_ref[0] = _dot_nt(wcvt_ref[...], h).astype(BF16)


def _proj(x, mod, tc, ts, g_attn, g_ql, g_kvl, gvec, w_in_ext, w_q_ext, w_kv_ext, w_cvt):
    bsz, seq, _ = x.shape
    tm = PROJ_TM
    tok = lambda b, i: (b, i, 0)
    tok_t = lambda b, i: (b, 0, i)
    const2 = lambda b, i: (0, 0)
    wide = MLA_HEADS * LANES
    narrow = CA_HEADS * CA_HEAD_DIM
    row_major = lambda w: (jax.ShapeDtypeStruct((bsz, seq, w), BF16), pl.BlockSpec((1, tm, w), tok))
    feat_major = lambda w: (jax.ShapeDtypeStruct((bsz, w, seq), BF16), pl.BlockSpec((1, w, tm), tok_t))
    outs = [row_major(wide), row_major(wide), row_major(MLA_HEADS * MLA_V),
            row_major(narrow), row_major(narrow), feat_major(narrow)]
    out_shapes = [o[0] for o in outs]
    out_specs = [o[1] for o in outs]
    return pl.pallas_call(
        _proj_kernel,
        grid=(bsz, seq // tm),
        in_specs=[
            pl.BlockSpec((1, tm, D_MODEL), tok),
            pl.BlockSpec((1, N_MOD, D_MODEL), lambda b, i: (b, 0, 0)),
            pl.BlockSpec((1, tm, LANES), tok),
            pl.BlockSpec((1, tm, LANES), tok),
            pl.BlockSpec((1, D_MODEL), const2),
            pl.BlockSpec((1, MLA_Q_RANK), const2),
            pl.BlockSpec((1, MLA_KV_RANK), const2),
            pl.BlockSpec((8, LANES), const2),
            pl.BlockSpec((D_MODEL, D_IN_EXT), const2),
            pl.BlockSpec((MLA_Q_RANK, wide), const2),
            pl.BlockSpec((MLA_KV_RANK, wide + MLA_HEADS * MLA_V), const2),
            pl.BlockSpec((narrow, D_MODEL), const2),
        ],
        out_specs=out_specs,
        out_shape=out_shapes,
        compiler_params=_compiler_params(("parallel", "parallel")),
        name="proj",
    )(x, mod, tc, ts, g_attn, g_ql, g_kvl, gvec, w_in_ext, w_q_ext, w_kv_ext, w_cvt)


def _softmax_pv_t(score_parts, value_parts):
    def fold(x, op):
        return op(x.reshape(x.shape[0] // REDUCE_ROWS, REDUCE_ROWS, x.shape[1]), axis=0)

    m = fold(score_parts[0], jnp.max)
    for s in score_parts[1:]:
        m = jnp.maximum(m, fold(s, jnp.max))
    m = m.max(axis=0, keepdims=True)
    l = None
    acc = None
    for s, vt in zip(score_parts, value_parts):
        p = jnp.exp2(s - m)
        ps = fold(p, jnp.sum).sum(axis=0, keepdims=True)
        pv = _dot(vt, p.astype(BF16))
        l = ps if l is None else l + ps
        acc = pv if acc is None else acc + pv
    return acc / l


def _attention_pipeline(units, lookahead, scores, finish, store):
    pending = [scores(*u) for u in units[:lookahead]]
    outs = []
    for n, (i, hh) in enumerate(units):
        if n + lookahead < len(units):
            pending.append(scores(*units[n + lookahead]))
        outs.append(finish(i, pending.pop(0)))
        if hh == 1:
            store(i, outs)
            outs = []


def _store_pair(o_ref, lo, hi, outs, head_rows):
    row = lax.broadcasted_iota(jnp.int32, (LANES, 1), 0)
    o_t = jnp.where(row < head_rows, outs[0], outs[1])
    o_ref[0, lo:hi, :] = o_t.T.astype(BF16)


def _softmax_pv(score_parts, value_parts):
    m = score_parts[0].max(axis=-1, keepdims=True)
    for s in score_parts[1:]:
        m = jnp.maximum(m, s.max(axis=-1, keepdims=True))
    l = None
    acc = None
    for s, vb in zip(score_parts, value_parts):
        p = jnp.exp2(s - m)
        ps = jnp.sum(p, axis=-1, keepdims=True)
        pv = _dot(p.astype(BF16), vb)
        l = ps if l is None else l + ps
        acc = pv if acc is None else acc + pv
    return acc / l


def _mla_kernel(q_ref, k_ref, v_ref, o_ref):
    t = MLA_TQ
    seq = q_ref.shape[1]
    lane = lax.broadcasted_iota(jnp.int32, (1, LANES), 1)
    qry_chunk = lax.broadcasted_iota(jnp.int32, (t, 1), 0) // CHUNK
    key_chunk = lax.broadcasted_iota(jnp.int32, (1, t), 1) // CHUNK
    diag_mask = key_chunk <= qry_chunk

    def scores(i, hh):
        lo = i * t
        hs = slice(hh * LANES, (hh + 1) * LANES)
        q = q_ref[0, lo:lo + t, hs]
        parts = [jnp.where(diag_mask, _dot_nt(q, k_ref[0, lo:lo + t, hs]), NEG_INF)]
        if i > 0:
            parts.insert(0, _dot_nt(q, k_ref[0, 0:lo, hs]))
        return parts

    def finish(i, parts):
        lo = i * t
        vals = [v_ref[0, lo:lo + t, :]]
        if i > 0:
            vals.insert(0, v_ref[0, 0:lo, :])
        return _softmax_pv(parts, vals)

    def store(i, outs):
        o_ref[0, i * t:(i + 1) * t, :] = jnp.where(lane < MLA_V, outs[0], outs[1]).astype(BF16)

    _attention_pipeline([(i, hh) for i in range(seq // t) for hh in range(2)], MLA_LOOKAHEAD, scores, finish, store)


def _mla(q, k, v):
    bsz, seq, _ = q.shape
    pairs = MLA_HEADS // 2
    return pl.pallas_call(
        _mla_kernel,
        grid=(bsz, pairs),
        in_specs=[
            pl.BlockSpec((1, seq, 2 * LANES), lambda b, p: (b, 0, p)),
            pl.BlockSpec((1, seq, 2 * LANES), lambda b, p: (b, 0, p)),
            pl.BlockSpec((1, seq, LANES), lambda b, p: (b, 0, p)),
        ],
        out_specs=pl.BlockSpec((1, seq, LANES), lambda b, p: (b, 0, p)),
        out_shape=jax.ShapeDtypeStruct((bsz, seq, MLA_HEADS * MLA_V), BF16),
        compiler_params=_compiler_params(("parallel", "parallel")),
        name="mla",
    )(q, k, v)


def _ca_kernel(q_ref, k_ref, vt_ref, bm_ref, o_ref):
    tq, win = CA_TQ, CA_WIN
    seq = q_ref.shape[1]
    lane = lax.broadcasted_iota(jnp.int32, (1, LANES), 1)

    def window(i):
        hi = (i + 1) * tq
        k_lo = max(hi - win, 0)
        return k_lo, hi, k_lo - (hi - win)

    def scores(i, hh):
        k_lo, hi, c_lo = window(i)
        q = q_ref[0, i * tq:hi, :]
        head_lanes = (lane < CA_HEAD_DIM) if hh == 0 else (lane >= CA_HEAD_DIM)
        qm = jnp.where(head_lanes, q, jnp.zeros_like(q))
        return [_dot_nt(k_ref[0, k_lo:hi, :], qm) + bm_ref[hh, c_lo:, :]]

    def finish(i, parts):
        k_lo, hi, _ = window(i)
        return _softmax_pv_t(parts, [vt_ref[0, :, k_lo:hi]])

    def store(i, outs):
        _store_pair(o_ref, i * tq, (i + 1) * tq, outs, CA_HEAD_DIM)

    _attention_pipeline([(i, hh) for i in range(seq // tq) for hh in range(2)], CA_LOOKAHEAD, scores, finish, store)


def _ca(cq, ck, cvt, bias_mask):
    bsz, seq, _ = cq.shape
    pairs = CA_HEADS // 2
    tok = lambda b, p: (b, 0, p)
    return pl.pallas_call(
        _ca_kernel,
        grid=(bsz, pairs),
        in_specs=[
            pl.BlockSpec((1, seq, LANES), tok),
            pl.BlockSpec((1, seq, LANES), tok),
            pl.BlockSpec((1, LANES, seq), lambda b, p: (b, p, 0)),
            pl.BlockSpec((2, CA_WIN, CA_TQ), lambda b, p: (p, 0, 0)),
        ],
        out_specs=pl.BlockSpec((1, seq, LANES), tok),
        out_shape=jax.ShapeDtypeStruct((bsz, seq, CA_HEADS * CA_HEAD_DIM), BF16),
        compiler_params=_compiler_params(("parallel", "parallel")),
        name="ca",
    )(cq, ck, cvt, bias_mask)


def _final_kernel(x_ref, xh_ref, om_ref, omh_ref, oc_ref, och_ref, mod_ref, gmlp_ref,
                  wo_ref, wu_ref, cw_ref, cb_ref, wd_ref,
                  out_ref, h2_sc, ug0_sc, ug1_sc, uv0_sc, uv1_sc, acc_sc):
    i = pl.program_id(1)
    tm = FIN_TM
    halo = FIN_HALO
    mod = mod_ref[0]
    g_a, sh_m, sc_m, g_m = mod[2:3, :], mod[3:4, :], mod[4:5, :], mod[5:6, :]

    u_bufs = ((ug0_sc, uv0_sc), (ug1_sc, uv1_sc))
    rb = FIN_ROW_BLOCK
    n_rb = tm // rb

    def up_rows(r):
        return (0, halo + rb) if r == 0 else (halo + r * rb, halo + (r + 1) * rb)

    x_ext = jnp.concatenate([xh_ref[0], x_ref[0]], axis=0)
    o_ext = jnp.concatenate(
        [jnp.concatenate([omh_ref[0], och_ref[0]], axis=-1), jnp.concatenate([om_ref[0], oc_ref[0]], axis=-1)], axis=0)
    x1 = x_ext + g_a * _dot(o_ext, wo_ref[...])
    out_ref[0] = x1[halo:]
    y = x1 * lax.rsqrt(jnp.mean(x1 * x1, axis=-1, keepdims=True) + EPS) * gmlp_ref[...]
    h2 = y * (1.0 + sc_m) + sh_m
    h2_sc[:halo] = jnp.where(i > 0, h2[:halo], 0.0).astype(BF16)
    h2_sc[halo:] = h2[halo:].astype(BF16)

    def up_project(c, r):
        ug_sc, uv_sc = u_bufs[c % 2]
        lo, hi = up_rows(r)
        h2b = h2_sc[lo:hi, :]
        ug_sc[lo:hi, :] = _dot(h2b, wu_ref[:, c * FF_CHUNK:(c + 1) * FF_CHUNK])
        uv_sc[lo:hi, :] = _dot(h2b, wu_ref[:, D_FF + c * FF_CHUNK:D_FF + (c + 1) * FF_CHUNK])

    def conv(u_sc, col, r):
        cw = cw_ref[:, col:col + FF_CHUNK]
        lo = halo + r * rb
        return (u_sc[lo - 2:lo - 2 + rb, :] * cw[0:1, :] + u_sc[lo - 1:lo - 1 + rb, :] * cw[1:2, :]
                + u_sc[lo:lo + rb, :] * cw[2:3, :] + cb_ref[:, col:col + FF_CHUNK])

    def mlp_rows(c, r):
        ug_sc, uv_sc = u_bufs[c % 2]
        gcol, vcol = c * FF_CHUNK, D_FF + c * FF_CHUNK
        gate = conv(ug_sc, gcol, r)
        val = conv(uv_sc, vcol, r)
        act = (gate / (1.0 + jnp.exp(-gate)) * val).astype(BF16)
        down = _dot(act, wd_ref[gcol:gcol + FF_CHUNK, :])
        rows = slice(r * rb, (r + 1) * rb)
        if c == 0:
            acc_sc[rows, :] = down
        elif c < N_FF_CHUNKS - 1:
            acc_sc[rows, :] += down
        else:
            out_ref[0, rows, :] += g_m * (acc_sc[rows, :] + down)

    for r in range(n_rb):
        up_project(0, r)
    for c in range(N_FF_CHUNKS):
        for r in range(n_rb):
            if c + 1 < N_FF_CHUNKS:
                up_project(c + 1, r)
            mlp_rows(c, r)


def _final(x, o_mla, o_ca, mod, g_mlp, w_out, w_up, conv_w, conv_b, w_down):
    bsz, seq, _ = x.shape
    tm, halo = FIN_TM, FIN_HALO
    tok = lambda b, i: (b, i, 0)
    prev = lambda b, i: (b, jnp.maximum(i * (tm // halo) - 1, 0), 0)
    const2 = lambda b, i: (0, 0)
    d_mix = o_mla.shape[-1]
    resident = dict(pipeline_mode=pl.Buffered(1))
    return pl.pallas_call(
        _final_kernel,
        grid=(bsz, seq // tm),
        in_specs=[
            pl.BlockSpec((1, tm, D_MODEL), tok),
            pl.BlockSpec((1, halo, D_MODEL), prev),
            pl.BlockSpec((1, tm, d_mix), tok),
            pl.BlockSpec((1, halo, d_mix), prev),
            pl.BlockSpec((1, tm, d_mix), tok),
            pl.BlockSpec((1, halo, d_mix), prev),
            pl.BlockSpec((1, N_MOD, D_MODEL), lambda b, i: (b, 0, 0)),
            pl.BlockSpec((1, D_MODEL), const2),
            pl.BlockSpec((2 * d_mix, D_MODEL), const2, **resident),
            pl.BlockSpec((D_MODEL, 2 * D_FF), const2, **resident),
            pl.BlockSpec((3, 2 * D_FF), const2),
            pl.BlockSpec((1, 2 * D_FF), const2),
            pl.BlockSpec((D_FF, D_MODEL), const2, **resident),
        ],
        out_specs=pl.BlockSpec((1, tm, D_MODEL), tok),
        out_shape=jax.ShapeDtypeStruct((bsz, seq, D_MODEL), F32),
        scratch_shapes=[
            pltpu.VMEM((halo + tm, D_MODEL), BF16),
            pltpu.VMEM((halo + tm, FF_CHUNK), F32),
            pltpu.VMEM((halo + tm, FF_CHUNK), F32),
            pltpu.VMEM((halo + tm, FF_CHUNK), F32),
            pltpu.VMEM((halo + tm, FF_CHUNK), F32),
            pltpu.VMEM((tm, D_MODEL), F32),
        ],
        compiler_params=_compiler_params(("parallel", "arbitrary")),
        name="final",
    )(x, x, o_mla, o_mla, o_ca, o_ca, mod, g_mlp, w_out, w_up, conv_w, conv_b, w_down)


def _prep_layer(w_in, w_q_up, w_kv_up, g_mla_q, g_mla_k, g_ca_q, g_ca_k):
    half = MLA_ROPE // 2
    c0 = MLA_Q_RANK + MLA_KV_RANK
    k1 = w_in[:, c0:c0 + half]
    k2 = w_in[:, c0 + half:c0 + MLA_ROPE]
    z64 = jnp.zeros((D_MODEL, MLA_NOPE), w_in.dtype)
    c_cv = c0 + MLA_ROPE + 2 * CA_HEADS * CA_HEAD_DIM
    w_in_ext = jnp.concatenate(
        [w_in[:, :c0], z64, k1, k2, k1, k2, z64, k2, k1, k2, k1, w_in[:, c0 + MLA_ROPE:c_cv]], axis=1).astype(BF16)
    w_cvt = w_in[:, c_cv:].T.astype(BF16)

    wq = w_q_up.reshape(MLA_Q_RANK, MLA_HEADS, MLA_QK)
    x1 = wq[..., MLA_NOPE:MLA_NOPE + half]
    x2 = wq[..., MLA_NOPE + half:]
    w_q_ext = jnp.concatenate([wq, x2, x1], axis=-1).reshape(MLA_Q_RANK, MLA_HEADS * LANES).astype(BF16)

    wkv = w_kv_up.reshape(MLA_KV_RANK, MLA_HEADS, MLA_NOPE + MLA_V)
    wk = jnp.concatenate([wkv[..., :MLA_NOPE], jnp.zeros((MLA_KV_RANK, MLA_HEADS, LANES - MLA_NOPE), wkv.dtype)], axis=-1)
    w_kv_ext = jnp.concatenate(
        [wk.reshape(MLA_KV_RANK, MLA_HEADS * LANES), wkv[..., MLA_NOPE:].reshape(MLA_KV_RANK, MLA_HEADS * MLA_V)],
        axis=1).astype(BF16)

    gq1, gq2 = g_mla_q[MLA_NOPE:MLA_NOPE + half], g_mla_q[MLA_NOPE + half:]
    gk1, gk2 = g_mla_k[MLA_NOPE:MLA_NOPE + half], g_mla_k[MLA_NOPE + half:]
    z = jnp.zeros((MLA_NOPE,), F32)
    rows = [
        jnp.concatenate([g_mla_q, gq2, gq1]) * (MLA_QK ** -0.5 * LOG2E),
        jnp.concatenate([g_mla_k[:MLA_NOPE], z]),
        jnp.concatenate([z, gk1, gk2, gk1, gk2]),
        jnp.concatenate([z, gk2, gk1, gk2, gk1]),
        jnp.concatenate([g_ca_q, g_ca_q]) * (CA_HEAD_DIM ** -0.5 * LOG2E),
        jnp.concatenate([g_ca_k, g_ca_k]),
        jnp.zeros((LANES,), F32),
        jnp.zeros((LANES,), F32),
    ]
    gvec = jnp.stack(rows)
    return w_in_ext, w_q_ext, w_kv_ext, w_cvt, gvec


def kernel(x, c, positions, w_ada, b_ada, g_attn_norm, w_in, g_q_latent, g_kv_latent, w_q_up, w_kv_up, g_mla_q, g_mla_k, g_ca_q, g_ca_k, rel_bias, w_out, g_mlp_norm, w_up, conv_w, conv_b, w_down):
    bsz, seq, _ = x.shape
    depth = w_ada.shape[0]

    tc, ts = _rope_tables(positions)

    for l in range(depth):
        w_in_ext, w_q_ext, w_kv_ext, w_cvt, gvec = _prep_layer(
            w_in[l], w_q_up[l], w_kv_up[l], g_mla_q[l], g_mla_k[l], g_ca_q[l], g_ca_k[l])
        mod = _ada(c, w_ada[l], b_ada[l]).reshape(bsz, N_MOD, D_MODEL)
        bias_mask = _bias_tiles(rel_bias[l])
        q, k, v, cq, ck, cvt = _proj(
            x, mod, tc, ts, g_attn_norm[l].reshape(1, -1), g_q_latent[l].reshape(1, -1),
            g_kv_latent[l].reshape(1, -1), gvec, w_in_ext, w_q_ext, w_kv_ext, w_cvt)
        o_mla = _mla(q, k, v)
        o_ca = _ca(cq, ck, cvt, bias_mask)
        x = _final(x, o_mla, o_ca, mod, g_mlp_norm[l].reshape(1, -1), w_out[l].astype(BF16),
                   w_up[l].astype(BF16), conv_w[l], conv_b[l].reshape(1, -1), w_down[l].astype(BF16))
    return x
```

```python
import math

import numpy as np
import jax
import jax.numpy as jnp
from jax import lax
from jax.experimental import pallas as pl
from jax.experimental.pallas import tpu as pltpu

D_MODEL = 1024
CHUNK = 64
LEFT_CHUNKS = 8
MLA_HEADS = 8
MLA_Q_RANK = 256
MLA_KV_RANK = 128
MLA_NOPE = 64
MLA_ROPE = 32
MLA_QK = MLA_NOPE + MLA_ROPE
MLA_V = 64
ROPE_THETA = 10000.0
CA_HEADS = 8
CA_HEAD_DIM = 64
REL_CLIP = 128
D_FF = 2816
N_MOD = 6
EPS = 1e-6
NEG_INF = -1e30
LOG2E = math.log2(math.e)

LANES = 128
BF16_SUBLANES = 16
VMEM_LIMIT_BYTES = 56 * 1024 * 1024

PROJ_TM = 1024
MLA_TQ = 256
CA_TQ = 256
CA_WIN = CA_TQ + LEFT_CHUNKS * CHUNK
CA_ROLL_W = 1024
REDUCE_ROWS = 64
MLA_LOOKAHEAD = 3
CA_LOOKAHEAD = 2
FIN_TM = 512
FIN_TILES_PER_SEQ = 4
FIN_HALO = BF16_SUBLANES
FIN_ROW_BLOCK = 128
FF_CHUNK = 256
N_FF_CHUNKS = D_FF // FF_CHUNK

_C_QLAT = 0
_C_KVLAT = _C_QLAT + MLA_Q_RANK
_C_KA = _C_KVLAT + MLA_KV_RANK
_C_KB = _C_KA + LANES
_C_CQ = _C_KB + LANES
_C_CK = _C_CQ + CA_HEADS * CA_HEAD_DIM
D_IN_EXT = _C_CK + CA_HEADS * CA_HEAD_DIM

F32 = jnp.float32
BF16 = jnp.bfloat16


def _dot(a, b):
    return jnp.dot(a, b, preferred_element_type=F32)


def _dot_nt(a, b):
    return lax.dot_general(a, b, (((1,), (1,)), ((), ())), preferred_element_type=F32)


def _compiler_params(semantics):
    return pltpu.CompilerParams(dimension_semantics=semantics, vmem_limit_bytes=VMEM_LIMIT_BYTES)


def _ada_kernel(c_ref, w_ref, b_ref, o_ref):
    c = c_ref[...]
    s = c / (1.0 + jnp.exp(-c))
    o_ref[...] = _dot(s.astype(BF16), w_ref[...].astype(BF16)) + b_ref[...]


def _ada(c, w_ada, b_ada):
    bsz = c.shape[0]
    n_out = w_ada.shape[1]
    tn = D_MODEL
    return pl.pallas_call(
        _ada_kernel,
        grid=(n_out // tn,),
        in_specs=[
            pl.BlockSpec((bsz, D_MODEL), lambda j: (0, 0)),
            pl.BlockSpec((D_MODEL, tn), lambda j: (0, j)),
            pl.BlockSpec((1, tn), lambda j: (0, j)),
        ],
        out_specs=pl.BlockSpec((bsz, tn), lambda j: (0, j)),
        out_shape=jax.ShapeDtypeStruct((bsz, n_out), F32),
        compiler_params=_compiler_params(("arbitrary",)),
        name="ada",
    )(c, w_ada, b_ada.reshape(1, n_out))


def _expand(d, e_ref):
    hi = d.astype(BF16)
    r1 = d - hi.astype(F32)
    mid = r1.astype(BF16)
    lo = (r1 - mid.astype(F32)).astype(BF16)
    e = e_ref[...]
    return _dot(hi, e) + _dot(mid, e) + _dot(lo, e)


def _rope_kernel(pos_ref, rep_ref, inv_ref, ec_ref, es_ref, base_ref, tc_ref, ts_ref):
    rows = pos_ref.shape[0]
    tok_per_row = ec_ref.shape[1] // LANES
    pos = _expand(pos_ref[...].astype(F32), rep_ref)
    ang = pos * inv_ref[...]
    tc_wide = _expand(jnp.cos(ang), ec_ref) + base_ref[...]
    ts_wide = _expand(jnp.sin(ang), es_ref)
    for t in range(tok_per_row):
        tc_ref[pl.ds(t, rows, stride=tok_per_row), :] = tc_wide[:, t * LANES:(t + 1) * LANES]
        ts_ref[pl.ds(t, rows, stride=tok_per_row), :] = ts_wide[:, t * LANES:(t + 1) * LANES]


def _rope_tables(positions):
    half = MLA_ROPE // 2
    bsz, seq = positions.shape
    n_tok = bsz * seq
    tok_per_row = LANES // half
    rows = n_tok // tok_per_row
    inv = jnp.power(ROPE_THETA, -jnp.arange(half, dtype=F32) / half)
    inv_t = jnp.tile(inv, tok_per_row).reshape(1, LANES)
    spread = np.repeat(np.eye(tok_per_row, dtype=np.float32), half, axis=1)

    src = np.arange(LANES)
    sel_c = np.zeros((LANES, tok_per_row * LANES), np.float32)
    sel_s = np.zeros((LANES, tok_per_row * LANES), np.float32)
    for rep, sign in enumerate((-1.0, 1.0, -1.0, 1.0)):
        dst = (src // half) * LANES + MLA_NOPE + rep * half + src % half
        sel_c[src, dst] = 1.0
        sel_s[src, dst] = sign
    base = np.tile((np.arange(LANES) < MLA_NOPE).astype(np.float32), tok_per_row).reshape(1, -1)

    tr = 512
    wide = tok_per_row * LANES
    const = lambda i: (0, 0)
    tc, ts = pl.pallas_call(
        _rope_kernel,
        grid=(rows // tr,),
        in_specs=[
            pl.BlockSpec((tr, tok_per_row), lambda i: (i, 0)),
            pl.BlockSpec((tok_per_row, LANES), const),
            pl.BlockSpec((1, LANES), const),
            pl.BlockSpec((LANES, wide), const),
            pl.BlockSpec((LANES, wide), const),
            pl.BlockSpec((1, wide), const),
        ],
        out_specs=[pl.BlockSpec((tr * tok_per_row, LANES), lambda i: (i, 0))] * 2,
        out_shape=[jax.ShapeDtypeStruct((n_tok, LANES), F32)] * 2,
        compiler_params=_compiler_params(("arbitrary",)),
        name="rope",
    )(positions.reshape(rows, tok_per_row), jnp.asarray(spread, BF16), inv_t,
      jnp.asarray(sel_c, BF16), jnp.asarray(sel_s, BF16), jnp.asarray(base))
    return tc.reshape(bsz, seq, LANES), ts.reshape(bsz, seq, LANES)


def _bias_kernel(y_ref, o_ref):
    tq, win = CA_TQ, CA_WIN
    full = jnp.broadcast_to(y_ref[0], (tq, CA_ROLL_W))
    rolled = pltpu.roll(full, 0, 1, stride=1, stride_axis=0)
    q_chunk = lax.broadcasted_iota(jnp.int32, (tq, 1), 0) // CHUNK
    k_chunk = lax.broadcasted_iota(jnp.int32, (1, win), 1) // CHUNK
    valid = jnp.logical_and(k_chunk >= q_chunk, k_chunk <= q_chunk + LEFT_CHUNKS)
    o_ref[0] = jnp.where(valid, rolled[:, :win] * LOG2E, NEG_INF).T


def _bias_tiles(rel_bias):
    n_heads = rel_bias.shape[0]
    t = jnp.arange(CA_ROLL_W)
    d = jnp.where(t < CA_WIN, t, t - CA_ROLL_W)
    idx = jnp.clip(LEFT_CHUNKS * CHUNK - d, -REL_CLIP, REL_CLIP) + REL_CLIP
    y = rel_bias[:, idx].reshape(n_heads, 1, CA_ROLL_W)
    return pl.pallas_call(
        _bias_kernel,
        grid=(n_heads,),
        in_specs=[pl.BlockSpec((1, 1, CA_ROLL_W), lambda h: (h, 0, 0))],
        out_specs=pl.BlockSpec((1, CA_WIN, CA_TQ), lambda h: (h, 0, 0)),
        out_shape=jax.ShapeDtypeStruct((n_heads, CA_WIN, CA_TQ), F32),
        compiler_params=_compiler_params(("arbitrary",)),
        name="bias",
    )(y)


def _proj_kernel(x_ref, mod_ref, tc_ref, ts_ref, gattn_ref, gql_ref, gkvl_ref, gv_ref,
                 win_ref, wq_ref, wkv_ref, wcvt_ref,
                 q_ref, k_ref, v_ref, cq_ref, ck_ref, cvt_ref):
    x = x_ref[0]
    mod = mod_ref[0]
    sh, sc = mod[0:1, :], mod[1:2, :]
    y = x * lax.rsqrt(jnp.mean(x * x, axis=-1, keepdims=True) + EPS) * gattn_ref[...]
    h = (y * (1.0 + sc) + sh).astype(BF16)
    proj = _dot(h, win_ref[...])

    lane = lax.broadcasted_iota(jnp.int32, (1, LANES), 1)
    gv = gv_ref[...]
    g_q, g_kn, g_ka, g_kb, g_cq, g_ck = (gv[r:r + 1, :] for r in range(6))

    tc = tc_ref[0]
    ts = ts_ref[0]
    tq = jnp.where(lane < MLA_QK, tc, ts)

    ql = proj[:, _C_QLAT:_C_QLAT + MLA_Q_RANK]
    qn = (ql * lax.rsqrt(jnp.mean(ql * ql, axis=-1, keepdims=True) + EPS) * gql_ref[...]).astype(BF16)
    qp = _dot(qn, wq_ref[...])
    qmul = g_q * tq
    for hd in range(MLA_HEADS):
        qh = qp[:, hd * LANES:(hd + 1) * LANES]
        ss = jnp.sum(jnp.where(lane < MLA_QK, qh * qh, 0.0), axis=-1, keepdims=True)
        r = lax.rsqrt(ss * (1.0 / MLA_QK) + EPS)
        q_ref[0, :, hd * LANES:(hd + 1) * LANES] = (qh * r * qmul).astype(BF16)

    kvl = proj[:, _C_KVLAT:_C_KVLAT + MLA_KV_RANK]
    kvn = (kvl * lax.rsqrt(jnp.mean(kvl * kvl, axis=-1, keepdims=True) + EPS) * gkvl_ref[...]).astype(BF16)
    kvp = _dot(kvn, wkv_ref[...])
    ka = proj[:, _C_KA:_C_KA + LANES]
    kb = proj[:, _C_KB:_C_KB + LANES]
    krot = ka * g_ka * tc + kb * g_kb * ts
    ss_rope = jnp.sum(jnp.where(lane < MLA_QK, ka * ka, 0.0), axis=-1, keepdims=True)
    for hd in range(MLA_HEADS):
        kh = kvp[:, hd * LANES:(hd + 1) * LANES]
        ss = jnp.sum(kh * kh, axis=-1, keepdims=True) + ss_rope
        r = lax.rsqrt(ss * (1.0 / MLA_QK) + EPS)
        k_ref[0, :, hd * LANES:(hd + 1) * LANES] = ((kh * g_kn + krot) * r).astype(BF16)
    v_ref[0] = kvp[:, MLA_HEADS * LANES:].astype(BF16)

    lo = lane < CA_HEAD_DIM
    for src, gain, dst in ((_C_CQ, g_cq, cq_ref), (_C_CK, g_ck, ck_ref)):
        for p in range(CA_HEADS // 2):
            xx = proj[:, src + p * LANES:src + (p + 1) * LANES]
            x2 = xx * xx
            s_all = jnp.sum(x2, axis=-1, keepdims=True)
            s_lo = jnp.sum(jnp.where(lo, x2, 0.0), axis=-1, keepdims=True)
            r_lo = lax.rsqrt(s_lo * (1.0 / CA_HEAD_DIM) + EPS)
            r_hi = lax.rsqrt((s_all - s_lo) * (1.0 / CA_HEAD_DIM) + EPS)
            dst[0, :, p * LANES:(p + 1) * LANES] = (xx * jnp.where(lo, r_lo, r_hi) * gain).astype(BF16)
    cvt_ref[0] = _dot_nt(wcvt_ref[...], h).astype(BF16)


def _proj(x, mod, tc, ts, g_attn, g_ql, g_kvl, gvec, w_in_ext, w_q_ext, w_kv_ext, w_cvt):
    bsz, seq, _ = x.shape
    tm = PROJ_TM
    tok = lambda b, i: (b, i, 0)
    tok_t = lambda b, i: (b, 0, i)
    const2 = lambda b, i: (0, 0)
    wide = MLA_HEADS * LANES
    narrow = CA_HEADS * CA_HEAD_DIM
    row_major = lambda w: (jax.ShapeDtypeStruct((bsz, seq, w), BF16), pl.BlockSpec((1, tm, w), tok))
    feat_major = lambda w: (jax.ShapeDtypeStruct((bsz, w, seq), BF16), pl.BlockSpec((1, w, tm), tok_t))
    outs = [row_major(wide), row_major(wide), row_major(MLA_HEADS * MLA_V),
            row_major(narrow), row_major(narrow), feat_major(narrow)]
    out_shapes = [o[0] for o in outs]
    out_specs = [o[1] for o in outs]
    return pl.pallas_call(
        _proj_kernel,
        grid=(bsz, seq // tm),
        in_specs=[
            pl.BlockSpec((1, tm, D_MODEL), tok),
            pl.BlockSpec((1, N_MOD, D_MODEL), lambda b, i: (b, 0, 0)),
            pl.BlockSpec((1, tm, LANES), tok),
            pl.BlockSpec((1, tm, LANES), tok),
            pl.BlockSpec((1, D_MODEL), const2),
            pl.BlockSpec((1, MLA_Q_RANK), const2),
            pl.BlockSpec((1, MLA_KV_RANK), const2),
            pl.BlockSpec((8, LANES), const2),
            pl.BlockSpec((D_MODEL, D_IN_EXT), const2),
            pl.BlockSpec((MLA_Q_RANK, wide), const2),
            pl.BlockSpec((MLA_KV_RANK, wide + MLA_HEADS * MLA_V), const2),
            pl.BlockSpec((narrow, D_MODEL), const2),
        ],
        out_specs=out_specs,
        out_shape=out_shapes,
        compiler_params=_compiler_params(("parallel", "parallel")),
        name="proj",
    )(x, mod, tc, ts, g_attn, g_ql, g_kvl, gvec, w_in_ext, w_q_ext, w_kv_ext, w_cvt)


def _softmax_pv_t(score_parts, value_parts):
    def fold(x, op):
        return op(x.reshape(x.shape[0] // REDUCE_ROWS, REDUCE_ROWS, x.shape[1]), axis=0)

    m = fold(score_parts[0], jnp.max)
    for s in score_parts[1:]:
        m = jnp.maximum(m, fold(s, jnp.max))
    m = m.max(axis=0, keepdims=True)
    l = None
    acc = None
    for s, vt in zip(score_parts, value_parts):
        p = jnp.exp2(s - m)
        ps = fold(p, jnp.sum).sum(axis=0, keepdims=True)
        pv = _dot(vt, p.astype(BF16))
        l = ps if l is None else l + ps
        acc = pv if acc is None else acc + pv
    return acc / l


def _attention_pipeline(units, lookahead, scores, finish, store):
    pending = [scores(*u) for u in units[:lookahead]]
    outs = []
    for n, (i, hh) in enumerate(units):
        if n + lookahead < len(units):
            pending.append(scores(*units[n + lookahead]))
        outs.append(finish(i, pending.pop(0)))
        if hh == 1:
            store(i, outs)
            outs = []


def _store_pair(o_ref, lo, hi, outs, head_rows):
    row = lax.broadcasted_iota(jnp.int32, (LANES, 1), 0)
    o_t = jnp.where(row < head_rows, outs[0], outs[1])
    o_ref[0, lo:hi, :] = o_t.T.astype(BF16)


def _softmax_pv(score_parts, value_parts):
    m = score_parts[0].max(axis=-1, keepdims=True)
    for s in score_parts[1:]:
        m = jnp.maximum(m, s.max(axis=-1, keepdims=True))
    l = None
    acc = None
    for s, vb in zip(score_parts, value_parts):
        p = jnp.exp2(s - m)
        ps = jnp.sum(p, axis=-1, keepdims=True)
        pv = _dot(p.astype(BF16), vb)
        l = ps if l is None else l + ps
        acc = pv if acc is None else acc + pv
    return acc / l


def _mla_kernel(q_ref, k_ref, v_ref, o_ref):
    t = MLA_TQ
    seq = q_ref.shape[1]
    lane = lax.broadcasted_iota(jnp.int32, (1, LANES), 1)
    qry_chunk = lax.broadcasted_iota(jnp.int32, (t, 1), 0) // CHUNK
    key_chunk = lax.broadcasted_iota(jnp.int32, (1, t), 1) // CHUNK
    diag_mask = key_chunk <= qry_chunk

    def scores(i, hh):
        lo = i * t
        hs = slice(hh * LANES, (hh + 1) * LANES)
        q = q_ref[0, lo:lo + t, hs]
        parts = [jnp.where(diag_mask, _dot_nt(q, k_ref[0, lo:lo + t, hs]), NEG_INF)]
        if i > 0:
            parts.insert(0, _dot_nt(q, k_ref[0, 0:lo, hs]))
        return parts

    def finish(i, parts):
        lo = i * t
        vals = [v_ref[0, lo:lo + t, :]]
        if i > 0:
            vals.insert(0, v_ref[0, 0:lo, :])
        return _softmax_pv(parts, vals)

    def store(i, outs):
        o_ref[0, i * t:(i + 1) * t, :] = jnp.where(lane < MLA_V, outs[0], outs[1]).astype(BF16)

    _attention_pipeline([(i, hh) for i in range(seq // t) for hh in range(2)], MLA_LOOKAHEAD, scores, finish, store)


def _mla(q, k, v):
    bsz, seq, _ = q.shape
    pairs = MLA_HEADS // 2
    return pl.pallas_call(
        _mla_kernel,
        grid=(bsz, pairs),
        in_specs=[
            pl.BlockSpec((1, seq, 2 * LANES), lambda b, p: (b, 0, p)),
            pl.BlockSpec((1, seq, 2 * LANES), lambda b, p: (b, 0, p)),
            pl.BlockSpec((1, seq, LANES), lambda b, p: (b, 0, p)),
        ],
        out_specs=pl.BlockSpec((1, seq, LANES), lambda b, p: (b, 0, p)),
        out_shape=jax.ShapeDtypeStruct((bsz, seq, MLA_HEADS * MLA_V), BF16),
        compiler_params=_compiler_params(("parallel", "parallel")),
        name="mla",
    )(q, k, v)


def _ca_kernel(q_ref, k_ref, vt_ref, bm_ref, o_ref):
    tq, win = CA_TQ, CA_WIN
    seq = q_ref.shape[1]
    lane = lax.broadcasted_iota(jnp.int32, (1, LANES), 1)

    def window(i):
        hi = (i + 1) * tq
        k_lo = max(hi - win, 0)
        return k_lo, hi, k_lo - (hi - win)

    def scores(i, hh):
        k_lo, hi, c_lo = window(i)
        q = q_ref[0, i * tq:hi, :]
        head_lanes = (lane < CA_HEAD_DIM) if hh == 0 else (lane >= CA_HEAD_DIM)
        qm = jnp.where(head_lanes, q, jnp.zeros_like(q))
        return [_dot_nt(k_ref[0, k_lo:hi, :], qm) + bm_ref[hh, c_lo:, :]]

    def finish(i, parts):
        k_lo, hi, _ = window(i)
        return _softmax_pv_t(parts, [vt_ref[0, :, k_lo:hi]])

    def store(i, outs):
        _store_pair(o_ref, i * tq, (i + 1) * tq, outs, CA_HEAD_DIM)

    _attention_pipeline([(i, hh) for i in range(seq // tq) for hh in range(2)], CA_LOOKAHEAD, scores, finish, store)


def _ca(cq, ck, cvt, bias_mask):
    bsz, seq, _ = cq.shape
    pairs = CA_HEADS // 2
    tok = lambda b, p: (b, 0, p)
    return pl.pallas_call(
        _ca_kernel,
        grid=(bsz, pairs),
        in_specs=[
            pl.BlockSpec((1, seq, LANES), tok),
            pl.BlockSpec((1, seq, LANES), tok),
            pl.BlockSpec((1, LANES, seq), lambda b, p: (b, p, 0)),
            pl.BlockSpec((2, CA_WIN, CA_TQ), lambda b, p: (p, 0, 0)),
        ],
        out_specs=pl.BlockSpec((1, seq, LANES), tok),
        out_shape=jax.ShapeDtypeStruct((bsz, seq, CA_HEADS * CA_HEAD_DIM), BF16),
        compiler_params=_compiler_params(("parallel", "parallel")),
        name="ca",
    )(cq, ck, cvt, bias_mask)


def _final_kernel(x_ref, xh_ref, om_ref, omh_ref, oc_ref, och_ref, mod_ref, modp_ref, gmlp_ref,
                  wo_ref, wu_ref, cw_ref, cb_ref, wd_ref,
                  out_ref, h2_sc, h2n_sc, x1n_sc, ug0_sc, ug1_sc, ug2_sc, uv0_sc, uv1_sc, uv2_sc, acc_sc):
    j = pl.program_id(0)
    n_tiles = pl.num_programs(0) - 1
    tm = FIN_TM
    halo = FIN_HALO
    tiles_per_seq = FIN_TILES_PER_SEQ
    cur = jnp.minimum(j, n_tiles - 1)
    cur_i = lax.rem(cur, tiles_per_seq)

    mod = mod_ref[0]
    g_a, sh_m, sc_m = mod[2:3, :], mod[3:4, :], mod[4:5, :]
    g_m = modp_ref[0][5:6, :]

    first_bufs = (ug2_sc, uv2_sc)
    u_bufs = ((ug0_sc, uv0_sc), (ug1_sc, uv1_sc))
    rb = FIN_ROW_BLOCK
    n_rb = tm // rb

    @pl.when(j == 0)
    def _():
        h2n_sc[...] = jnp.zeros_like(h2n_sc)
        x1n_sc[...] = jnp.zeros_like(x1n_sc)
        ug2_sc[...] = jnp.zeros_like(ug2_sc)
        uv2_sc[...] = jnp.zeros_like(uv2_sc)

    h2_sc[...] = h2n_sc[...]
    out_ref[0] = x1n_sc[...]

    def up_rows(r):
        return (0, halo + rb) if r == 0 else (halo + r * rb, halo + (r + 1) * rb)

    def ext_rows(halo_ref, main_ref, r):
        if r == 0:
            return jnp.concatenate([halo_ref[0], main_ref[0, 0:rb, :]], axis=0)
        return main_ref[0, r * rb:(r + 1) * rb, :]

    def out_project(r):
        o_r = jnp.concatenate([ext_rows(omh_ref, om_ref, r), ext_rows(och_ref, oc_ref, r)], axis=-1)
        x1 = ext_rows(xh_ref, x_ref, r) + g_a * _dot(o_r, wo_ref[...])
        x1n_sc[r * rb:(r + 1) * rb, :] = x1[halo:] if r == 0 else x1
        return x1

    def normalize(r, x1):
        lo, hi = up_rows(r)
        y = x1 * lax.rsqrt(jnp.mean(x1 * x1, axis=-1, keepdims=True) + EPS) * gmlp_ref[...]
        h2 = y * (1.0 + sc_m) + sh_m
        if r == 0:
            h2n_sc[:halo, :] = jnp.where(cur_i > 0, h2[:halo], 0.0).astype(BF16)
            h2n_sc[halo:hi, :] = h2[halo:].astype(BF16)
        else:
            h2n_sc[lo:hi, :] = h2.astype(BF16)

    def up_project(c, r, src_sc, bufs):
        ug_sc, uv_sc = bufs
        lo, hi = up_rows(r)
        h2b = src_sc[lo:hi, :]
        ug_sc[lo:hi, :] = _dot(h2b, wu_ref[:, c * FF_CHUNK:(c + 1) * FF_CHUNK])
        uv_sc[lo:hi, :] = _dot(h2b, wu_ref[:, D_FF + c * FF_CHUNK:D_FF + (c + 1) * FF_CHUNK])

    def chunk_bufs(c):
        return first_bufs if c == 0 else u_bufs[c % 2]

    def conv(u_sc, col, r):
        cw = cw_ref[:, col:col + FF_CHUNK]
        lo = halo + r * rb
        return (u_sc[lo - 2:lo - 2 + rb, :] * cw[0:1, :] + u_sc[lo - 1:lo - 1 + rb, :] * cw[1:2, :]
                + u_sc[lo:lo + rb, :] * cw[2:3, :] + cb_ref[:, col:col + FF_CHUNK])

    def mlp_rows(c, r):
        ug_sc, uv_sc = chunk_bufs(c)
        gcol, vcol = c * FF_CHUNK, D_FF + c * FF_CHUNK
        gate = conv(ug_sc, gcol, r)
        val = conv(uv_sc, vcol, r)
        act = (gate / (1.0 + jnp.exp(-gate)) * val).astype(BF16)
        down = _dot(act, wd_ref[gcol:gcol + FF_CHUNK, :])
        rows = slice(r * rb, (r + 1) * rb)
        if c == 0:
            acc_sc[rows, :] = down
        elif c < N_FF_CHUNKS - 1:
            acc_sc[rows, :] += down
        else:
            out_ref[0, rows, :] += g_m * (acc_sc[rows, :] + down)

    last = N_FF_CHUNKS - 1
    for c in range(last):
        for r in range(n_rb):
            up_project(c + 1, r, h2_sc, chunk_bufs(c + 1))
            mlp_rows(c, r)
    x1_blocks = []
    for r in range(n_rb):
        x1_blocks.append(out_project(r))
        mlp_rows(last, r)
    for r in range(n_rb + 1):
        if r < n_rb:
            normalize(r, x1_blocks[r])
        if r >= 1:
            up_project(0, r - 1, h2n_sc, first_bufs)


def _final(x, o_mla, o_ca, mod, g_mlp, w_out, w_up, conv_w, conv_b, w_down):
    bsz, seq, _ = x.shape
    tm, halo = FIN_TM, FIN_HALO
    tps = FIN_TILES_PER_SEQ
    assert seq == tps * tm
    n_tiles = bsz * tps

    def staged(j):
        t = jnp.minimum(j, n_tiles - 1)
        return t // tps, t % tps

    def finished(j):
        t = jnp.maximum(j - 1, 0)
        return t // tps, t % tps

    def tok(j):
        b, i = staged(j)
        return b, i, 0

    def prev(j):
        b, i = staged(j)
        return b, jnp.maximum(i * (tm // halo) - 1, 0), 0

    const2 = lambda j: (0, 0)
    d_mix = o_mla.shape[-1]
    resident = dict(pipeline_mode=pl.Buffered(1))
    return pl.pallas_call(
        _final_kernel,
        grid=(n_tiles + 1,),
        in_specs=[
            pl.BlockSpec((1, tm, D_MODEL), tok),
            pl.BlockSpec((1, halo, D_MODEL), prev),
            pl.BlockSpec((1, tm, d_mix), tok),
            pl.BlockSpec((1, halo, d_mix), prev),
            pl.BlockSpec((1, tm, d_mix), tok),
            pl.BlockSpec((1, halo, d_mix), prev),
            pl.BlockSpec((1, N_MOD, D_MODEL), lambda j: (staged(j)[0], 0, 0)),
            pl.BlockSpec((1, N_MOD, D_MODEL), lambda j: (finished(j)[0], 0, 0)),
            pl.BlockSpec((1, D_MODEL), const2),
            pl.BlockSpec((2 * d_mix, D_MODEL), const2, **resident),
            pl.BlockSpec((D_MODEL, 2 * D_FF), const2, **resident),
            pl.BlockSpec((3, 2 * D_FF), const2),
            pl.BlockSpec((1, 2 * D_FF), const2),
            pl.BlockSpec((D_FF, D_MODEL), const2, **resident),
        ],
        out_specs=pl.BlockSpec((1, tm, D_MODEL), lambda j: (*finished(j), 0)),
        out_shape=jax.ShapeDtypeStruct((bsz, seq, D_MODEL), F32),
        scratch_shapes=[
            pltpu.VMEM((halo + tm, D_MODEL), BF16),
            pltpu.VMEM((halo + tm, D_MODEL), BF16),
            pltpu.VMEM((tm, D_MODEL), F32),
        ] + [pltpu.VMEM((halo + tm, FF_CHUNK), F32)] * 6 + [
            pltpu.VMEM((tm, D_MODEL), F32),
        ],
        compiler_params=_compiler_params(("arbitrary",)),
        name="final",
    )(x, x, o_mla, o_mla, o_ca, o_ca, mod, mod, g_mlp, w_out, w_up, conv_w, conv_b, w_down)


def _prep_layer(w_in, w_q_up, w_kv_up, g_mla_q, g_mla_k, g_ca_q, g_ca_k):
    half = MLA_ROPE // 2
    c0 = MLA_Q_RANK + MLA_KV_RANK
    k1 = w_in[:, c0:c0 + half]
    k2 = w_in[:, c0 + half:c0 + MLA_ROPE]
    z64 = jnp.zeros((D_MODEL, MLA_NOPE), w_in.dtype)
    c_cv = c0 + MLA_ROPE + 2 * CA_HEADS * CA_HEAD_DIM
    w_in_ext = jnp.concatenate(
        [w_in[:, :c0], z64, k1, k2, k1, k2, z64, k2, k1, k2, k1, w_in[:, c0 + MLA_ROPE:c_cv]], axis=1).astype(BF16)
    w_cvt = w_in[:, c_cv:].T.astype(BF16)

    wq = w_q_up.reshape(MLA_Q_RANK, MLA_HEADS, MLA_QK)
    x1 = wq[..., MLA_NOPE:MLA_NOPE + half]
    x2 = wq[..., MLA_NOPE + half:]
    w_q_ext = jnp.concatenate([wq, x2, x1], axis=-1).reshape(MLA_Q_RANK, MLA_HEADS * LANES).astype(BF16)

    wkv = w_kv_up.reshape(MLA_KV_RANK, MLA_HEADS, MLA_NOPE + MLA_V)
    wk = jnp.concatenate([wkv[..., :MLA_NOPE], jnp.zeros((MLA_KV_RANK, MLA_HEADS, LANES - MLA_NOPE), wkv.dtype)], axis=-1)
    w_kv_ext = jnp.concatenate(
        [wk.reshape(MLA_KV_RANK, MLA_HEADS * LANES), wkv[..., MLA_NOPE:].reshape(MLA_KV_RANK, MLA_HEADS * MLA_V)],
        axis=1).astype(BF16)

    gq1, gq2 = g_mla_q[MLA_NOPE:MLA_NOPE + half], g_mla_q[MLA_NOPE + half:]
    gk1, gk2 = g_mla_k[MLA_NOPE:MLA_NOPE + half], g_mla_k[MLA_NOPE + half:]
    z = jnp.zeros((MLA_NOPE,), F32)
    rows = [
        jnp.concatenate([g_mla_q, gq2, gq1]) * (MLA_QK ** -0.5 * LOG2E),
        jnp.concatenate([g_mla_k[:MLA_NOPE], z]),
        jnp.concatenate([z, gk1, gk2, gk1, gk2]),
        jnp.concatenate([z, gk2, gk1, gk2, gk1]),
        jnp.concatenate([g_ca_q, g_ca_q]) * (CA_HEAD_DIM ** -0.5 * LOG2E),
        jnp.concatenate([g_ca_k, g_ca_k]),
        jnp.zeros((LANES,), F32),
        jnp.zeros((LANES,), F32),
    ]
    gvec = jnp.stack(rows)
    return w_in_ext, w_q_ext, w_kv_ext, w_cvt, gvec


def kernel(x, c, positions, w_ada, b_ada, g_attn_norm, w_in, g_q_latent, g_kv_latent, w_q_up, w_kv_up, g_mla_q, g_mla_k, g_ca_q, g_ca_k, rel_bias, w_out, g_mlp_norm, w_up, conv_w, conv_b, w_down):
    bsz, seq, _ = x.shape
    depth = w_ada.shape[0]

    tc, ts = _rope_tables(positions)

    for l in range(depth):
        w_in_ext, w_q_ext, w_kv_ext, w_cvt, gvec = _prep_layer(
            w_in[l], w_q_up[l], w_kv_up[l], g_mla_q[l], g_mla_k[l], g_ca_q[l], g_ca_k[l])
        mod = _ada(c, w_ada[l], b_ada[l]).reshape(bsz, N_MOD, D_MODEL)
        bias_mask = _bias_tiles(rel_bias[l])
        q, k, v, cq, ck, cvt = _proj(
            x, mod, tc, ts, g_attn_norm[l].reshape(1, -1), g_q_latent[l].reshape(1, -1),
            g_kv_latent[l].reshape(1, -1), gvec, w_in_ext, w_q_ext, w_kv_ext, w_cvt)
        o_mla = _mla(q, k, v)
        o_ca = _ca(cq, ck, cvt, bias_mask)
        x = _final(x, o_mla, o_ca, mod, g_mlp_norm[l].reshape(1, -1), w_out[l].astype(BF16),
                   w_up[l].astype(BF16), conv_w[l], conv_b[l].reshape(1, -1), w_down[l].astype(BF16))
    return x
```

```python
import math

import numpy as np
import jax
import jax.numpy as jnp
from jax import lax
from jax.experimental import pallas as pl
from jax.experimental.pallas import tpu as pltpu

D_MODEL = 1024
CHUNK = 64
LEFT_CHUNKS = 8
MLA_HEADS = 8
MLA_Q_RANK = 256
MLA_KV_RANK = 128
MLA_NOPE = 64
MLA_ROPE = 32
MLA_QK = MLA_NOPE + MLA_ROPE
MLA_V = 64
ROPE_THETA = 10000.0
CA_HEADS = 8
CA_HEAD_DIM = 64
REL_CLIP = 128
D_FF = 2816
N_MOD = 6
EPS = 1e-6
NEG_INF = -1e30
LOG2E = math.log2(math.e)

LANES = 128
BF16_SUBLANES = 16
VMEM_LIMIT_BYTES = 56 * 1024 * 1024

PROJ_TM = 1024
MLA_TQ = 256
CA_TQ = 256
CA_WIN = CA_TQ + LEFT_CHUNKS * CHUNK
CA_ROLL_W = 1024
REDUCE_ROWS = 64
ATTN_LOOKAHEAD = 4
FIN_TM = 512
FIN_HALO = BF16_SUBLANES
FIN_ROW_BLOCK = 128
FF_CHUNK = 256
N_FF_CHUNKS = D_FF // FF_CHUNK

_C_QLAT = 0
_C_KVLAT = _C_QLAT + MLA_Q_RANK
_C_KA = _C_KVLAT + MLA_KV_RANK
_C_KB = _C_KA + LANES
_C_CQ = _C_KB + LANES
_C_CK = _C_CQ + CA_HEADS * CA_HEAD_DIM
D_IN_EXT = _C_CK + CA_HEADS * CA_HEAD_DIM

F32 = jnp.float32
BF16 = jnp.bfloat16


def _dot(a, b):
    return jnp.dot(a, b, preferred_element_type=F32)


def _dot_nt(a, b):
    return lax.dot_general(a, b, (((1,), (1,)), ((), ())), preferred_element_type=F32)


def _compiler_params(semantics):
    return pltpu.CompilerParams(dimension_semantics=semantics, vmem_limit_bytes=VMEM_LIMIT_BYTES)


def _ada_kernel(c_ref, w_ref, b_ref, o_ref):
    c = c_ref[...]
    s = c / (1.0 + jnp.exp(-c))
    o_ref[...] = _dot(s.astype(BF16), w_ref[...].astype(BF16)) + b_ref[...]


def _ada(c, w_ada, b_ada):
    bsz = c.shape[0]
    n_out = w_ada.shape[1]
    tn = D_MODEL
    return pl.pallas_call(
        _ada_kernel,
        grid=(n_out // tn,),
        in_specs=[
            pl.BlockSpec((bsz, D_MODEL), lambda j: (0, 0)),
            pl.BlockSpec((D_MODEL, tn), lambda j: (0, j)),
            pl.BlockSpec((1, tn), lambda j: (0, j)),
        ],
        out_specs=pl.BlockSpec((bsz, tn), lambda j: (0, j)),
        out_shape=jax.ShapeDtypeStruct((bsz, n_out), F32),
        compiler_params=_compiler_params(("arbitrary",)),
        name="ada",
    )(c, w_ada, b_ada.reshape(1, n_out))


def _expand(d, e_ref):
    hi = d.astype(BF16)
    r1 = d - hi.astype(F32)
    mid = r1.astype(BF16)
    lo = (r1 - mid.astype(F32)).astype(BF16)
    e = e_ref[...]
    return _dot(hi, e) + _dot(mid, e) + _dot(lo, e)


def _rope_kernel(pos_ref, rep_ref, inv_ref, ec_ref, es_ref, base_ref, tc_ref, ts_ref):
    rows = pos_ref.shape[0]
    tok_per_row = ec_ref.shape[1] // LANES
    pos = _expand(pos_ref[...].astype(F32), rep_ref)
    ang = pos * inv_ref[...]
    tc_wide = _expand(jnp.cos(ang), ec_ref) + base_ref[...]
    ts_wide = _expand(jnp.sin(ang), es_ref)
    for t in range(tok_per_row):
        tc_ref[pl.ds(t, rows, stride=tok_per_row), :] = tc_wide[:, t * LANES:(t + 1) * LANES]
        ts_ref[pl.ds(t, rows, stride=tok_per_row), :] = ts_wide[:, t * LANES:(t + 1) * LANES]


def _rope_tables(positions):
    half = MLA_ROPE // 2
    bsz, seq = positions.shape
    n_tok = bsz * seq
    tok_per_row = LANES // half
    rows = n_tok // tok_per_row
    inv = jnp.power(ROPE_THETA, -jnp.arange(half, dtype=F32) / half)
    inv_t = jnp.tile(inv, tok_per_row).reshape(1, LANES)
    spread = np.repeat(np.eye(tok_per_row, dtype=np.float32), half, axis=1)

    src = np.arange(LANES)
    sel_c = np.zeros((LANES, tok_per_row * LANES), np.float32)
    sel_s = np.zeros((LANES, tok_per_row * LANES), np.float32)
    for rep, sign in enumerate((-1.0, 1.0, -1.0, 1.0)):
        dst = (src // half) * LANES + MLA_NOPE + rep * half + src % half
        sel_c[src, dst] = 1.0
        sel_s[src, dst] = sign
    base = np.tile((np.arange(LANES) < MLA_NOPE).astype(np.float32), tok_per_row).reshape(1, -1)

    tr = 512
    wide = tok_per_row * LANES
    const = lambda i: (0, 0)
    tc, ts = pl.pallas_call(
        _rope_kernel,
        grid=(rows // tr,),
        in_specs=[
            pl.BlockSpec((tr, tok_per_row), lambda i: (i, 0)),
            pl.BlockSpec((tok_per_row, LANES), const),
            pl.BlockSpec((1, LANES), const),
            pl.BlockSpec((LANES, wide), const),
            pl.BlockSpec((LANES, wide), const),
            pl.BlockSpec((1, wide), const),
        ],
        out_specs=[pl.BlockSpec((tr * tok_per_row, LANES), lambda i: (i, 0))] * 2,
        out_shape=[jax.ShapeDtypeStruct((n_tok, LANES), F32)] * 2,
        compiler_params=_compiler_params(("arbitrary",)),
        name="rope",
    )(positions.reshape(rows, tok_per_row), jnp.asarray(spread, BF16), inv_t,
      jnp.asarray(sel_c, BF16), jnp.asarray(sel_s, BF16), jnp.asarray(base))
    return tc.reshape(bsz, seq, LANES), ts.reshape(bsz, seq, LANES)


def _bias_kernel(y_ref, o_ref):
    tq, win = CA_TQ, CA_WIN
    full = jnp.broadcast_to(y_ref[0], (tq, CA_ROLL_W))
    rolled = pltpu.roll(full, 0, 1, stride=1, stride_axis=0)
    q_chunk = lax.broadcasted_iota(jnp.int32, (tq, 1), 0) // CHUNK
    k_chunk = lax.broadcasted_iota(jnp.int32, (1, win), 1) // CHUNK
    valid = jnp.logical_and(k_chunk >= q_chunk, k_chunk <= q_chunk + LEFT_CHUNKS)
    o_ref[0] = jnp.where(valid, rolled[:, :win] * LOG2E, NEG_INF).T


def _bias_tiles(rel_bias):
    n_heads = rel_bias.shape[0]
    t = jnp.arange(CA_ROLL_W)
    d = jnp.where(t < CA_WIN, t, t - CA_ROLL_W)
    idx = jnp.clip(LEFT_CHUNKS * CHUNK - d, -REL_CLIP, REL_CLIP) + REL_CLIP
    y = rel_bias[:, idx].reshape(n_heads, 1, CA_ROLL_W)
    return pl.pallas_call(
        _bias_kernel,
        grid=(n_heads,),
        in_specs=[pl.BlockSpec((1, 1, CA_ROLL_W), lambda h: (h, 0, 0))],
        out_specs=pl.BlockSpec((1, CA_WIN, CA_TQ), lambda h: (h, 0, 0)),
        out_shape=jax.ShapeDtypeStruct((n_heads, CA_WIN, CA_TQ), F32),
        compiler_params=_compiler_params(("arbitrary",)),
        name="bias",
    )(y)


def _proj_kernel(x_ref, mod_ref, tc_ref, ts_ref, gattn_ref, gql_ref, gkvl_ref, gv_ref,
                 win_ref, wq_ref, wkv_ref, wcvt_ref,
                 q_ref, k_ref, v_ref, cq_ref, ck_ref, cvt_ref):
    x = x_ref[0]
    mod = mod_ref[0]
    sh, sc = mod[0:1, :], mod[1:2, :]
    y = x * lax.rsqrt(jnp.mean(x * x, axis=-1, keepdims=True) + EPS) * gattn_ref[...]
    h = (y * (1.0 + sc) + sh).astype(BF16)
    proj = _dot(h, win_ref[...])

    lane = lax.broadcasted_iota(jnp.int32, (1, LANES), 1)
    gv = gv_ref[...]
    g_q, g_kn, g_ka, g_kb, g_cq, g_ck = (gv[r:r + 1, :] for r in range(6))

    tc = tc_ref[0]
    ts = ts_ref[0]
    tq = jnp.where(lane < MLA_QK, tc, ts)

    ql = proj[:, _C_QLAT:_C_QLAT + MLA_Q_RANK]
    qn = (ql * lax.rsqrt(jnp.mean(ql * ql, axis=-1, keepdims=True) + EPS) * gql_ref[...]).astype(BF16)
    qp = _dot(qn, wq_ref[...])
    qmul = g_q * tq
    for hd in range(MLA_HEADS):
        qh = qp[:, hd * LANES:(hd + 1) * LANES]
        ss = jnp.sum(jnp.where(lane < MLA_QK, qh * qh, 0.0), axis=-1, keepdims=True)
        r = lax.rsqrt(ss * (1.0 / MLA_QK) + EPS)
        q_ref[0, :, hd * LANES:(hd + 1) * LANES] = (qh * r * qmul).astype(BF16)

    kvl = proj[:, _C_KVLAT:_C_KVLAT + MLA_KV_RANK]
    kvn = (kvl * lax.rsqrt(jnp.mean(kvl * kvl, axis=-1, keepdims=True) + EPS) * gkvl_ref[...]).astype(BF16)
    kvp = _dot(kvn, wkv_ref[...])
    ka = proj[:, _C_KA:_C_KA + LANES]
    kb = proj[:, _C_KB:_C_KB + LANES]
    krot = ka * g_ka * tc + kb * g_kb * ts
    ss_rope = jnp.sum(jnp.where(lane < MLA_QK, ka * ka, 0.0), axis=-1, keepdims=True)
    for hd in range(MLA_HEADS):
        kh = kvp[:, hd * LANES:(hd + 1) * LANES]
        ss = jnp.sum(kh * kh, axis=-1, keepdims=True) + ss_rope
        r = lax.rsqrt(ss * (1.0 / MLA_QK) + EPS)
        k_ref[0, :, hd * LANES:(hd + 1) * LANES] = ((kh * g_kn + krot) * r).astype(BF16)
    v_ref[0] = kvp[:, MLA_HEADS * LANES:].astype(BF16)

    lo = lane < CA_HEAD_DIM
    for src, gain, dst in ((_C_CQ, g_cq, cq_ref), (_C_CK, g_ck, ck_ref)):
        for p in range(CA_HEADS // 2):
            xx = proj[:, src + p * LANES:src + (p + 1) * LANES]
            x2 = xx * xx
            s_all = jnp.sum(x2, axis=-1, keepdims=True)
            s_lo = jnp.sum(jnp.where(lo, x2, 0.0), axis=-1, keepdims=True)
            r_lo = lax.rsqrt(s_lo * (1.0 / CA_HEAD_DIM) + EPS)
            r_hi = lax.rsqrt((s_all - s_lo) * (1.0 / CA_HEAD_DIM) + EPS)
            dst[0, :, p * LANES:(p + 1) * LANES] = (xx * jnp.where(lo, r_lo, r_hi) * gain).astype(BF16)
    cvt_ref[0] = _dot_nt(wcvt_ref[...], h).astype(BF16)


def _proj(x, mod, tc, ts, g_attn, g_ql, g_kvl, gvec, w_in_ext, w_q_ext, w_kv_ext, w_cvt):
    bsz, seq, _ = x.shape
    tm = PROJ_TM
    tok = lambda b, i: (b, i, 0)
    tok_t = lambda b, i: (b, 0, i)
    const2 = lambda b, i: (0, 0)
    wide = MLA_HEADS * LANES
    narrow = CA_HEADS * CA_HEAD_DIM
    row_major = lambda w: (jax.ShapeDtypeStruct((bsz, seq, w), BF16), pl.BlockSpec((1, tm, w), tok))
    feat_major = lambda w: (jax.ShapeDtypeStruct((bsz, w, seq), BF16), pl.BlockSpec((1, w, tm), tok_t))
    outs = [row_major(wide), row_major(wide), row_major(MLA_HEADS * MLA_V),
            row_major(narrow), row_major(narrow), feat_major(narrow)]
    out_shapes = [o[0] for o in outs]
    out_specs = [o[1] for o in outs]
    return pl.pallas_call(
        _proj_kernel,
        grid=(bsz, seq // tm),
        in_specs=[
            pl.BlockSpec((1, tm, D_MODEL), tok),
            pl.BlockSpec((1, N_MOD, D_MODEL), lambda b, i: (b, 0, 0)),
            pl.BlockSpec((1, tm, LANES), tok),
            pl.BlockSpec((1, tm, LANES), tok),
            pl.BlockSpec((1, D_MODEL), const2),
            pl.BlockSpec((1, MLA_Q_RANK), const2),
            pl.BlockSpec((1, MLA_KV_RANK), const2),
            pl.BlockSpec((8, LANES), const2),
            pl.BlockSpec((D_MODEL, D_IN_EXT), const2),
            pl.BlockSpec((MLA_Q_RANK, wide), const2),
            pl.BlockSpec((MLA_KV_RANK, wide + MLA_HEADS * MLA_V), const2),
            pl.BlockSpec((narrow, D_MODEL), const2),
        ],
        out_specs=out_specs,
        out_shape=out_shapes,
        compiler_params=_compiler_params(("parallel", "parallel")),
        name="proj",
    )(x, mod, tc, ts, g_attn, g_ql, g_kvl, gvec, w_in_ext, w_q_ext, w_kv_ext, w_cvt)


def _softmax_pv_t(score_parts, value_parts):
    def fold(x, op):
        return op(x.reshape(x.shape[0] // REDUCE_ROWS, REDUCE_ROWS, x.shape[1]), axis=0)

    m = fold(score_parts[0], jnp.max)
    for s in score_parts[1:]:
        m = jnp.maximum(m, fold(s, jnp.max))
    m = m.max(axis=0, keepdims=True)
    l = None
    acc = None
    for s, vt in zip(score_parts, value_parts):
        p = jnp.exp2(s - m)
        ps = fold(p, jnp.sum).sum(axis=0, keepdims=True)
        pv = _dot(vt, p.astype(BF16))
        l = ps if l is None else l + ps
        acc = pv if acc is None else acc + pv
    return acc / l


def _attention_pipeline(units, lookahead):
    pending = [scores() for scores, _ in units[:lookahead]]
    for n, (_, finish) in enumerate(units):
        if n + lookahead < len(units):
            pending.append(units[n + lookahead][0]())
        finish(pending.pop(0))


def _pair_units(n_tiles, scores, finish, store):
    units = []
    outs = []

    def make(i, hh):
        def finish_unit(parts):
            outs.append(finish(i, parts))
            if hh == 1:
                store(i, list(outs))
                outs.clear()
        return (lambda: scores(i, hh)), finish_unit

    for i in range(n_tiles):
        for hh in range(2):
            units.append(make(i, hh))
    return units


def _store_pair(o_ref, lo, hi, outs, head_rows):
    row = lax.broadcasted_iota(jnp.int32, (LANES, 1), 0)
    o_t = jnp.where(row < head_rows, outs[0], outs[1])
    o_ref[0, lo:hi, :] = o_t.T.astype(BF16)


def _softmax_pv(score_parts, value_parts):
    m = score_parts[0].max(axis=-1, keepdims=True)
    for s in score_parts[1:]:
        m = jnp.maximum(m, s.max(axis=-1, keepdims=True))
    l = None
    acc = None
    for s, vb in zip(score_parts, value_parts):
        p = jnp.exp2(s - m)
        ps = jnp.sum(p, axis=-1, keepdims=True)
        pv = _dot(p.astype(BF16), vb)
        l = ps if l is None else l + ps
        acc = pv if acc is None else acc + pv
    return acc / l


def _mla_units(q_ref, k_ref, v_ref, o_ref):
    t = MLA_TQ
    seq = q_ref.shape[1]
    lane = lax.broadcasted_iota(jnp.int32, (1, LANES), 1)
    qry_chunk = lax.broadcasted_iota(jnp.int32, (t, 1), 0) // CHUNK
    key_chunk = lax.broadcasted_iota(jnp.int32, (1, t), 1) // CHUNK
    diag_mask = key_chunk <= qry_chunk

    def scores(i, hh):
        lo = i * t
        hs = slice(hh * LANES, (hh + 1) * LANES)
        q = q_ref[0, lo:lo + t, hs]
        parts = [jnp.where(diag_mask, _dot_nt(q, k_ref[0, lo:lo + t, hs]), NEG_INF)]
        if i > 0:
            parts.insert(0, _dot_nt(q, k_ref[0, 0:lo, hs]))
        return parts

    def finish(i, parts):
        lo = i * t
        vals = [v_ref[0, lo:lo + t, :]]
        if i > 0:
            vals.insert(0, v_ref[0, 0:lo, :])
        return _softmax_pv(parts, vals)

    def store(i, outs):
        o_ref[0, i * t:(i + 1) * t, :] = jnp.where(lane < MLA_V, outs[0], outs[1]).astype(BF16)

    return _pair_units(seq // t, scores, finish, store)


def _ca_units(q_ref, k_ref, vt_ref, bm_ref, o_ref):
    tq, win = CA_TQ, CA_WIN
    seq = q_ref.shape[1]
    lane = lax.broadcasted_iota(jnp.int32, (1, LANES), 1)

    def window(i):
        hi = (i + 1) * tq
        k_lo = max(hi - win, 0)
        return k_lo, hi, k_lo - (hi - win)

    def scores(i, hh):
        k_lo, hi, c_lo = window(i)
        q = q_ref[0, i * tq:hi, :]
        head_lanes = (lane < CA_HEAD_DIM) if hh == 0 else (lane >= CA_HEAD_DIM)
        qm = jnp.where(head_lanes, q, jnp.zeros_like(q))
        return [_dot_nt(k_ref[0, k_lo:hi, :], qm) + bm_ref[hh, c_lo:, :]]

    def finish(i, parts):
        k_lo, hi, _ = window(i)
        return _softmax_pv_t(parts, [vt_ref[0, :, k_lo:hi]])

    def store(i, outs):
        _store_pair(o_ref, i * tq, (i + 1) * tq, outs, CA_HEAD_DIM)

    return _pair_units(seq // tq, scores, finish, store)


def _attn_kernel(q_ref, k_ref, v_ref, cq_ref, ck_ref, cvt_ref, bm_ref, om_ref, oc_ref):
    mla = _mla_units(q_ref, k_ref, v_ref, om_ref)
    ca = _ca_units(cq_ref, ck_ref, cvt_ref, bm_ref, oc_ref)
    assert len(mla) == len(ca)
    units = [u for pair in zip(mla, ca) for u in pair]
    _attention_pipeline(units, ATTN_LOOKAHEAD)


def _attn(q, k, v, cq, ck, cvt, bias_mask):
    bsz, seq, _ = q.shape
    pairs = MLA_HEADS // 2
    assert CA_HEADS // 2 == pairs
    tok = lambda b, p: (b, 0, p)
    wide = pl.BlockSpec((1, seq, 2 * LANES), tok)
    narrow = pl.BlockSpec((1, seq, LANES), tok)
    return pl.pallas_call(
        _attn_kernel,
        grid=(bsz, pairs),
        in_specs=[
            wide, wide, narrow,
            narrow, narrow,
            pl.BlockSpec((1, LANES, seq), lambda b, p: (b, p, 0)),
            pl.BlockSpec((2, CA_WIN, CA_TQ), lambda b, p: (p, 0, 0)),
        ],
        out_specs=[narrow, narrow],
        out_shape=[jax.ShapeDtypeStruct((bsz, seq, MLA_HEADS * MLA_V), BF16),
                   jax.ShapeDtypeStruct((bsz, seq, CA_HEADS * CA_HEAD_DIM), BF16)],
        compiler_params=_compiler_params(("parallel", "parallel")),
        name="attn",
    )(q, k, v, cq, ck, cvt, bias_mask)


def _final_kernel(x_ref, xh_ref, om_ref, omh_ref, oc_ref, och_ref, mod_ref, gmlp_ref,
                  wo_ref, wu_ref, cw_ref, cb_ref, wd_ref,
                  out_ref, h2_sc, ug0_sc, ug1_sc, uv0_sc, uv1_sc, acc_sc):
    i = pl.program_id(1)
    tm = FIN_TM
    halo = FIN_HALO
    mod = mod_ref[0]
    g_a, sh_m, sc_m, g_m = mod[2:3, :], mod[3:4, :], mod[4:5, :], mod[5:6, :]

    u_bufs = ((ug0_sc, uv0_sc), (ug1_sc, uv1_sc))
    rb = FIN_ROW_BLOCK
    n_rb = tm // rb

    def up_rows(r):
        return (0, halo + rb) if r == 0 else (halo + r * rb, halo + (r + 1) * rb)

    x_ext = jnp.concatenate([xh_ref[0], x_ref[0]], axis=0)
    o_ext = jnp.concatenate(
        [jnp.concatenate([omh_ref[0], och_ref[0]], axis=-1), jnp.concatenate([om_ref[0], oc_ref[0]], axis=-1)], axis=0)
    x1 = x_ext + g_a * _dot(o_ext, wo_ref[...])
    out_ref[0] = x1[halo:]
    y = x1 * lax.rsqrt(jnp.mean(x1 * x1, axis=-1, keepdims=True) + EPS) * gmlp_ref[...]
    h2 = y * (1.0 + sc_m) + sh_m
    h2_sc[:halo] = jnp.where(i > 0, h2[:halo], 0.0).astype(BF16)
    h2_sc[halo:] = h2[halo:].astype(BF16)

    def up_project(c, r):
        ug_sc, uv_sc = u_bufs[c % 2]
        lo, hi = up_rows(r)
        h2b = h2_sc[lo:hi, :]
        ug_sc[lo:hi, :] = _dot(h2b, wu_ref[:, c * FF_CHUNK:(c + 1) * FF_CHUNK])
        uv_sc[lo:hi, :] = _dot(h2b, wu_ref[:, D_FF + c * FF_CHUNK:D_FF + (c + 1) * FF_CHUNK])

    def conv(u_sc, col, r):
        cw = cw_ref[:, col:col + FF_CHUNK]
        lo = halo + r * rb
        return (u_sc[lo - 2:lo - 2 + rb, :] * cw[0:1, :] + u_sc[lo - 1:lo - 1 + rb, :] * cw[1:2, :]
                + u_sc[lo:lo + rb, :] * cw[2:3, :] + cb_ref[:, col:col + FF_CHUNK])

    def mlp_rows(c, r):
        ug_sc, uv_sc = u_bufs[c % 2]
        gcol, vcol = c * FF_CHUNK, D_FF + c * FF_CHUNK
        gate = conv(ug_sc, gcol, r)
        val = conv(uv_sc, vcol, r)
        act = (gate / (1.0 + jnp.exp(-gate)) * val).astype(BF16)
        down = _dot(act, wd_ref[gcol:gcol + FF_CHUNK, :])
        rows = slice(r * rb, (r + 1) * rb)
        if c == 0:
            acc_sc[rows, :] = down
        elif c < N_FF_CHUNKS - 1:
            acc_sc[rows, :] += down
        else:
            out_ref[0, rows, :] += g_m * (acc_sc[rows, :] + down)

    for r in range(n_rb):
        up_project(0, r)
    for c in range(N_FF_CHUNKS):
        for r in range(n_rb):
            if c + 1 < N_FF_CHUNKS:
                up_project(c + 1, r)
            mlp_rows(c, r)


def _final(x, o_mla, o_ca, mod, g_mlp, w_out, w_up, conv_w, conv_b, w_down):
    bsz, seq, _ = x.shape
    tm, halo = FIN_TM, FIN_HALO
    tok = lambda b, i: (b, i, 0)
    prev = lambda b, i: (b, jnp.maximum(i * (tm // halo) - 1, 0), 0)
    const2 = lambda b, i: (0, 0)
    d_mix = o_mla.shape[-1]
    resident = dict(pipeline_mode=pl.Buffered(1))
    return pl.pallas_call(
        _final_kernel,
        grid=(bsz, seq // tm),
        in_specs=[
            pl.BlockSpec((1, tm, D_MODEL), tok),
            pl.BlockSpec((1, halo, D_MODEL), prev),
            pl.BlockSpec((1, tm, d_mix), tok),
            pl.BlockSpec((1, halo, d_mix), prev),
            pl.BlockSpec((1, tm, d_mix), tok),
            pl.BlockSpec((1, halo, d_mix), prev),
            pl.BlockSpec((1, N_MOD, D_MODEL), lambda b, i: (b, 0, 0)),
            pl.BlockSpec((1, D_MODEL), const2),
            pl.BlockSpec((2 * d_mix, D_MODEL), const2, **resident),
            pl.BlockSpec((D_MODEL, 2 * D_FF), const2, **resident),
            pl.BlockSpec((3, 2 * D_FF), const2),
            pl.BlockSpec((1, 2 * D_FF), const2),
            pl.BlockSpec((D_FF, D_MODEL), const2, **resident),
        ],
        out_specs=pl.BlockSpec((1, tm, D_MODEL), tok),
        out_shape=jax.ShapeDtypeStruct((bsz, seq, D_MODEL), F32),
        scratch_shapes=[
            pltpu.VMEM((halo + tm, D_MODEL), BF16),
            pltpu.VMEM((halo + tm, FF_CHUNK), F32),
            pltpu.VMEM((halo + tm, FF_CHUNK), F32),
            pltpu.VMEM((halo + tm, FF_CHUNK), F32),
            pltpu.VMEM((halo + tm, FF_CHUNK), F32),
            pltpu.VMEM((tm, D_MODEL), F32),
        ],
        compiler_params=_compiler_params(("parallel", "arbitrary")),
        name="final",
    )(x, x, o_mla, o_mla, o_ca, o_ca, mod, g_mlp, w_out, w_up, conv_w, conv_b, w_down)


def _prep_layer(w_in, w_q_up, w_kv_up, g_mla_q, g_mla_k, g_ca_q, g_ca_k):
    half = MLA_ROPE // 2
    c0 = MLA_Q_RANK + MLA_KV_RANK
    k1 = w_in[:, c0:c0 + half]
    k2 = w_in[:, c0 + half:c0 + MLA_ROPE]
    z64 = jnp.zeros((D_MODEL, MLA_NOPE), w_in.dtype)
    c_cv = c0 + MLA_ROPE + 2 * CA_HEADS * CA_HEAD_DIM
    w_in_ext = jnp.concatenate(
        [w_in[:, :c0], z64, k1, k2, k1, k2, z64, k2, k1, k2, k1, w_in[:, c0 + MLA_ROPE:c_cv]], axis=1).astype(BF16)
    w_cvt = w_in[:, c_cv:].T.astype(BF16)

    wq = w_q_up.reshape(MLA_Q_RANK, MLA_HEADS, MLA_QK)
    x1 = wq[..., MLA_NOPE:MLA_NOPE + half]
    x2 = wq[..., MLA_NOPE + half:]
    w_q_ext = jnp.concatenate([wq, x2, x1], axis=-1).reshape(MLA_Q_RANK, MLA_HEADS * LANES).astype(BF16)

    wkv = w_kv_up.reshape(MLA_KV_RANK, MLA_HEADS, MLA_NOPE + MLA_V)
    wk = jnp.concatenate([wkv[..., :MLA_NOPE], jnp.zeros((MLA_KV_RANK, MLA_HEADS, LANES - MLA_NOPE), wkv.dtype)], axis=-1)
    w_kv_ext = jnp.concatenate(
        [wk.reshape(MLA_KV_RANK, MLA_HEADS * LANES), wkv[..., MLA_NOPE:].reshape(MLA_KV_RANK, MLA_HEADS * MLA_V)],
        axis=1).astype(BF16)

    gq1, gq2 = g_mla_q[MLA_NOPE:MLA_NOPE + half], g_mla_q[MLA_NOPE + half:]
    gk1, gk2 = g_mla_k[MLA_NOPE:MLA_NOPE + half], g_mla_k[MLA_NOPE + half:]
    z = jnp.zeros((MLA_NOPE,), F32)
    rows = [
        jnp.concatenate([g_mla_q, gq2, gq1]) * (MLA_QK ** -0.5 * LOG2E),
        jnp.concatenate([g_mla_k[:MLA_NOPE], z]),
        jnp.concatenate([z, gk1, gk2, gk1, gk2]),
        jnp.concatenate([z, gk2, gk1, gk2, gk1]),
        jnp.concatenate([g_ca_q, g_ca_q]) * (CA_HEAD_DIM ** -0.5 * LOG2E),
        jnp.concatenate([g_ca_k, g_ca_k]),
        jnp.zeros((LANES,), F32),
        jnp.zeros((LANES,), F32),
    ]
    gvec = jnp.stack(rows)
    return w_in_ext, w_q_ext, w_kv_ext, w_cvt, gvec


def kernel(x, c, positions, w_ada, b_ada, g_attn_norm, w_in, g_q_latent, g_kv_latent, w_q_up, w_kv_up, g_mla_q, g_mla_k, g_ca_q, g_ca_k, rel_bias, w_out, g_mlp_norm, w_up, conv_w, conv_b, w_down):
    bsz, seq, _ = x.shape
    depth = w_ada.shape[0]

    tc, ts = _rope_tables(positions)

    for l in range(depth):
        w_in_ext, w_q_ext, w_kv_ext, w_cvt, gvec = _prep_layer(
            w_in[l], w_q_up[l], w_kv_up[l], g_mla_q[l], g_mla_k[l], g_ca_q[l], g_ca_k[l])
        mod = _ada(c, w_ada[l], b_ada[l]).reshape(bsz, N_MOD, D_MODEL)
        bias_mask = _bias_tiles(rel_bias[l])
        q, k, v, cq, ck, cvt = _proj(
            x, mod, tc, ts, g_attn_norm[l].reshape(1, -1), g_q_latent[l].reshape(1, -1),
            g_kv_latent[l].reshape(1, -1), gvec, w_in_ext, w_q_ext, w_kv_ext, w_cvt)
        o_mla, o_ca = _attn(q, k, v, cq, ck, cvt, bias_mask)
        x = _final(x, o_mla, o_ca, mod, g_mlp_norm[l].reshape(1, -1), w_out[l].astype(BF16),
                   w_up[l].astype(BF16), conv_w[l], conv_b[l].reshape(1, -1), w_down[l].astype(BF16))
    return x
```

```python
import math

import numpy as np
import jax
import jax.numpy as jnp
from jax import lax
from jax.experimental import pallas as pl
from jax.experimental.pallas import tpu as pltpu

D_MODEL = 1024
CHUNK = 64
LEFT_CHUNKS = 8
MLA_HEADS = 8
MLA_Q_RANK = 256
MLA_KV_RANK = 128
MLA_NOPE = 64
MLA_ROPE = 32
MLA_QK = MLA_NOPE + MLA_ROPE
MLA_V = 64
ROPE_THETA = 10000.0
CA_HEADS = 8
CA_HEAD_DIM = 64
REL_CLIP = 128
D_FF = 2816
N_MOD = 6
EPS = 1e-6
NEG_INF = -1e30
LOG2E = math.log2(math.e)

LANES = 128
BF16_SUBLANES = 16
VMEM_LIMIT_BYTES = 56 * 1024 * 1024

PROJ_TM = 1024
MLA_TQ = 256
CA_TQ = 256
CA_WIN = CA_TQ + LEFT_CHUNKS * CHUNK
CA_ROLL_W = 1024
REDUCE_ROWS = 64
ATTN_LOOKAHEAD = 6
FIN_TM = 512
FIN_HALO = BF16_SUBLANES
FIN_ROW_BLOCK = 128
FF_CHUNK = 256
N_FF_CHUNKS = D_FF // FF_CHUNK

_C_QLAT = 0
_C_KVLAT = _C_QLAT + MLA_Q_RANK
_C_KA = _C_KVLAT + MLA_KV_RANK
_C_KB = _C_KA + LANES
_C_CQ = _C_KB + LANES
_C_CK = _C_CQ + CA_HEADS * CA_HEAD_DIM
D_IN_EXT = _C_CK + CA_HEADS * CA_HEAD_DIM

F32 = jnp.float32
BF16 = jnp.bfloat16


def _dot(a, b):
    return jnp.dot(a, b, preferred_element_type=F32)


def _dot_nt(a, b):
    return lax.dot_general(a, b, (((1,), (1,)), ((), ())), preferred_element_type=F32)


def _compiler_params(semantics):
    return pltpu.CompilerParams(dimension_semantics=semantics, vmem_limit_bytes=VMEM_LIMIT_BYTES)


def _ada_kernel(c_ref, w_ref, b_ref, o_ref):
    c = c_ref[...]
    s = c / (1.0 + jnp.exp(-c))
    o_ref[...] = _dot(s.astype(BF16), w_ref[...].astype(BF16)) + b_ref[...]


def _ada(c, w_ada, b_ada):
    bsz = c.shape[0]
    n_out = w_ada.shape[1]
    tn = D_MODEL
    return pl.pallas_call(
        _ada_kernel,
        grid=(n_out // tn,),
        in_specs=[
            pl.BlockSpec((bsz, D_MODEL), lambda j: (0, 0)),
            pl.BlockSpec((D_MODEL, tn), lambda j: (0, j)),
            pl.BlockSpec((1, tn), lambda j: (0, j)),
        ],
        out_specs=pl.BlockSpec((bsz, tn), lambda j: (0, j)),
        out_shape=jax.ShapeDtypeStruct((bsz, n_out), F32),
        compiler_params=_compiler_params(("arbitrary",)),
        name="ada",
    )(c, w_ada, b_ada.reshape(1, n_out))


def _expand(d, e_ref):
    hi = d.astype(BF16)
    r1 = d - hi.astype(F32)
    mid = r1.astype(BF16)
    lo = (r1 - mid.astype(F32)).astype(BF16)
    e = e_ref[...]
    return _dot(hi, e) + _dot(mid, e) + _dot(lo, e)


def _rope_kernel(pos_ref, rep_ref, inv_ref, ec_ref, es_ref, base_ref, tc_ref, ts_ref):
    rows = pos_ref.shape[0]
    tok_per_row = ec_ref.shape[1] // LANES
    pos = _expand(pos_ref[...].astype(F32), rep_ref)
    ang = pos * inv_ref[...]
    tc_wide = _expand(jnp.cos(ang), ec_ref) + base_ref[...]
    ts_wide = _expand(jnp.sin(ang), es_ref)
    for t in range(tok_per_row):
        tc_ref[pl.ds(t, rows, stride=tok_per_row), :] = tc_wide[:, t * LANES:(t + 1) * LANES]
        ts_ref[pl.ds(t, rows, stride=tok_per_row), :] = ts_wide[:, t * LANES:(t + 1) * LANES]


def _rope_tables(positions):
    half = MLA_ROPE // 2
    bsz, seq = positions.shape
    n_tok = bsz * seq
    tok_per_row = LANES // half
    rows = n_tok // tok_per_row
    inv = jnp.power(ROPE_THETA, -jnp.arange(half, dtype=F32) / half)
    inv_t = jnp.tile(inv, tok_per_row).reshape(1, LANES)
    spread = np.repeat(np.eye(tok_per_row, dtype=np.float32), half, axis=1)

    src = np.arange(LANES)
    sel_c = np.zeros((LANES, tok_per_row * LANES), np.float32)
    sel_s = np.zeros((LANES, tok_per_row * LANES), np.float32)
    for rep, sign in enumerate((-1.0, 1.0, -1.0, 1.0)):
        dst = (src // half) * LANES + MLA_NOPE + rep * half + src % half
        sel_c[src, dst] = 1.0
        sel_s[src, dst] = sign
    base = np.tile((np.arange(LANES) < MLA_NOPE).astype(np.float32), tok_per_row).reshape(1, -1)

    tr = 512
    wide = tok_per_row * LANES
    const = lambda i: (0, 0)
    tc, ts = pl.pallas_call(
        _rope_kernel,
        grid=(rows // tr,),
        in_specs=[
            pl.BlockSpec((tr, tok_per_row), lambda i: (i, 0)),
            pl.BlockSpec((tok_per_row, LANES), const),
            pl.BlockSpec((1, LANES), const),
            pl.BlockSpec((LANES, wide), const),
            pl.BlockSpec((LANES, wide), const),
            pl.BlockSpec((1, wide), const),
        ],
        out_specs=[pl.BlockSpec((tr * tok_per_row, LANES), lambda i: (i, 0))] * 2,
        out_shape=[jax.ShapeDtypeStruct((n_tok, LANES), F32)] * 2,
        compiler_params=_compiler_params(("arbitrary",)),
        name="rope",
    )(positions.reshape(rows, tok_per_row), jnp.asarray(spread, BF16), inv_t,
      jnp.asarray(sel_c, BF16), jnp.asarray(sel_s, BF16), jnp.asarray(base))
    return tc.reshape(bsz, seq, LANES), ts.reshape(bsz, seq, LANES)


def _bias_kernel(y_ref, o_ref):
    tq, win = CA_TQ, CA_WIN
    full = jnp.broadcast_to(y_ref[0], (tq, CA_ROLL_W))
    rolled = pltpu.roll(full, 0, 1, stride=1, stride_axis=0)
    q_chunk = lax.broadcasted_iota(jnp.int32, (tq, 1), 0) // CHUNK
    k_chunk = lax.broadcasted_iota(jnp.int32, (1, win), 1) // CHUNK
    valid = jnp.logical_and(k_chunk >= q_chunk, k_chunk <= q_chunk + LEFT_CHUNKS)
    o_ref[0] = jnp.where(valid, rolled[:, :win] * LOG2E, NEG_INF).T


def _bias_tiles(rel_bias):
    n_heads = rel_bias.shape[0]
    t = jnp.arange(CA_ROLL_W)
    d = jnp.where(t < CA_WIN, t, t - CA_ROLL_W)
    idx = jnp.clip(LEFT_CHUNKS * CHUNK - d, -REL_CLIP, REL_CLIP) + REL_CLIP
    y = rel_bias[:, idx].reshape(n_heads, 1, CA_ROLL_W)
    return pl.pallas_call(
        _bias_kernel,
        grid=(n_heads,),
        in_specs=[pl.BlockSpec((1, 1, CA_ROLL_W), lambda h: (h, 0, 0))],
        out_specs=pl.BlockSpec((1, CA_WIN, CA_TQ), lambda h: (h, 0, 0)),
        out_shape=jax.ShapeDtypeStruct((n_heads, CA_WIN, CA_TQ), F32),
        compiler_params=_compiler_params(("arbitrary",)),
        name="bias",
    )(y)


def _proj_kernel(x_ref, mod_ref, tc_ref, ts_ref, gattn_ref, gql_ref, gkvl_ref, gv_ref,
                 win_ref, wq_ref, wkv_ref, wcvt_ref,
                 q_ref, k_ref, v_ref, cq_ref, ck_ref, cvt_ref):
    x = x_ref[0]
    mod = mod_ref[0]
    sh, sc = mod[0:1, :], mod[1:2, :]
    y = x * lax.rsqrt(jnp.mean(x * x, axis=-1, keepdims=True) + EPS) * gattn_ref[...]
    h = (y * (1.0 + sc) + sh).astype(BF16)
    proj = _dot(h, win_ref[...])

    lane = lax.broadcasted_iota(jnp.int32, (1, LANES), 1)
    gv = gv_ref[...]
    g_q, g_kn, g_ka, g_kb, g_cq, g_ck = (gv[r:r + 1, :] for r in range(6))

    tc = tc_ref[0]
    ts = ts_ref[0]
    tq = jnp.where(lane < MLA_QK, tc, ts)

    ql = proj[:, _C_QLAT:_C_QLAT + MLA_Q_RANK]
    qn = (ql * lax.rsqrt(jnp.mean(ql * ql, axis=-1, keepdims=True) + EPS) * gql_ref[...]).astype(BF16)
    qp = _dot(qn, wq_ref[...])
    qmul = g_q * tq
    for hd in range(MLA_HEADS):
        qh = qp[:, hd * LANES:(hd + 1) * LANES]
        ss = jnp.sum(jnp.where(lane < MLA_QK, qh * qh, 0.0), axis=-1, keepdims=True)
        r = lax.rsqrt(ss * (1.0 / MLA_QK) + EPS)
        q_ref[0, :, hd * LANES:(hd + 1) * LANES] = (qh * r * qmul).astype(BF16)

    kvl = proj[:, _C_KVLAT:_C_KVLAT + MLA_KV_RANK]
    kvn = (kvl * lax.rsqrt(jnp.mean(kvl * kvl, axis=-1, keepdims=True) + EPS) * gkvl_ref[...]).astype(BF16)
    kvp = _dot(kvn, wkv_ref[...])
    ka = proj[:, _C_KA:_C_KA + LANES]
    kb = proj[:, _C_KB:_C_KB + LANES]
    krot = ka * g_ka * tc + kb * g_kb * ts
    ss_rope = jnp.sum(jnp.where(lane < MLA_QK, ka * ka, 0.0), axis=-1, keepdims=True)
    for hd in range(MLA_HEADS):
        kh = kvp[:, hd * LANES:(hd + 1) * LANES]
        ss = jnp.sum(kh * kh, axis=-1, keepdims=True) + ss_rope
        r = lax.rsqrt(ss * (1.0 / MLA_QK) + EPS)
        k_ref[0, :, hd * LANES:(hd + 1) * LANES] = ((kh * g_kn + krot) * r).astype(BF16)
    v_ref[0] = kvp[:, MLA_HEADS * LANES:].astype(BF16)

    lo = lane < CA_HEAD_DIM
    for src, gain, dst in ((_C_CQ, g_cq, cq_ref), (_C_CK, g_ck, ck_ref)):
        for p in range(CA_HEADS // 2):
            xx = proj[:, src + p * LANES:src + (p + 1) * LANES]
            x2 = xx * xx
            s_all = jnp.sum(x2, axis=-1, keepdims=True)
            s_lo = jnp.sum(jnp.where(lo, x2, 0.0), axis=-1, keepdims=True)
            r_lo = lax.rsqrt(s_lo * (1.0 / CA_HEAD_DIM) + EPS)
            r_hi = lax.rsqrt((s_all - s_lo) * (1.0 / CA_HEAD_DIM) + EPS)
            dst[0, :, p * LANES:(p + 1) * LANES] = (xx * jnp.where(lo, r_lo, r_hi) * gain).astype(BF16)
    cvt_ref[0] = _dot_nt(wcvt_ref[...], h).astype(BF16)


def _proj(x, mod, tc, ts, g_attn, g_ql, g_kvl, gvec, w_in_ext, w_q_ext, w_kv_ext, w_cvt):
    bsz, seq, _ = x.shape
    tm = PROJ_TM
    tok = lambda b, i: (b, i, 0)
    tok_t = lambda b, i: (b, 0, i)
    const2 = lambda b, i: (0, 0)
    wide = MLA_HEADS * LANES
    narrow = CA_HEADS * CA_HEAD_DIM
    row_major = lambda w: (jax.ShapeDtypeStruct((bsz, seq, w), BF16), pl.BlockSpec((1, tm, w), tok))
    feat_major = lambda w: (jax.ShapeDtypeStruct((bsz, w, seq), BF16), pl.BlockSpec((1, w, tm), tok_t))
    outs = [row_major(wide), row_major(wide), row_major(MLA_HEADS * MLA_V),
            row_major(narrow), row_major(narrow), feat_major(narrow)]
    out_shapes = [o[0] for o in outs]
    out_specs = [o[1] for o in outs]
    return pl.pallas_call(
        _proj_kernel,
        grid=(bsz, seq // tm),
        in_specs=[
            pl.BlockSpec((1, tm, D_MODEL), tok),
            pl.BlockSpec((1, N_MOD, D_MODEL), lambda b, i: (b, 0, 0)),
            pl.BlockSpec((1, tm, LANES), tok),
            pl.BlockSpec((1, tm, LANES), tok),
            pl.BlockSpec((1, D_MODEL), const2),
            pl.BlockSpec((1, MLA_Q_RANK), const2),
            pl.BlockSpec((1, MLA_KV_RANK), const2),
            pl.BlockSpec((8, LANES), const2),
            pl.BlockSpec((D_MODEL, D_IN_EXT), const2),
            pl.BlockSpec((MLA_Q_RANK, wide), const2),
            pl.BlockSpec((MLA_KV_RANK, wide + MLA_HEADS * MLA_V), const2),
            pl.BlockSpec((narrow, D_MODEL), const2),
        ],
        out_specs=out_specs,
        out_shape=out_shapes,
        compiler_params=_compiler_params(("parallel", "parallel")),
        name="proj",
    )(x, mod, tc, ts, g_attn, g_ql, g_kvl, gvec, w_in_ext, w_q_ext, w_kv_ext, w_cvt)


def _softmax_pv_t(score_parts, value_parts):
    def fold(x, op):
        return op(x.reshape(x.shape[0] // REDUCE_ROWS, REDUCE_ROWS, x.shape[1]), axis=0)

    m = fold(score_parts[0], jnp.max)
    for s in score_parts[1:]:
        m = jnp.maximum(m, fold(s, jnp.max))
    m = m.max(axis=0, keepdims=True)
    l = None
    acc = None
    for s, vt in zip(score_parts, value_parts):
        p = jnp.exp2(s - m)
        ps = fold(p, jnp.sum).sum(axis=0, keepdims=True)
        pv = _dot(vt, p.astype(BF16))
        l = ps if l is None else l + ps
        acc = pv if acc is None else acc + pv
    return acc / l


def _attention_pipeline(units, lookahead):
    pending = [scores() for scores, _ in units[:lookahead]]
    for n, (_, finish) in enumerate(units):
        if n + lookahead < len(units):
            pending.append(units[n + lookahead][0]())
        finish(pending.pop(0))


def _pair_units(n_tiles, scores, finish, store):
    units = []
    outs = []

    def make(i, hh):
        def finish_unit(parts):
            outs.append(finish(i, parts))
            if hh == 1:
                store(i, list(outs))
                outs.clear()
        return (lambda: scores(i, hh)), finish_unit

    for i in range(n_tiles):
        for hh in range(2):
            units.append(make(i, hh))
    return units


def _store_pair(o_ref, lo, hi, outs, head_rows):
    row = lax.broadcasted_iota(jnp.int32, (LANES, 1), 0)
    o_t = jnp.where(row < head_rows, outs[0], outs[1])
    o_ref[0, lo:hi, :] = o_t.T.astype(BF16)


def _softmax_pv(score_parts, value_parts):
    m = score_parts[0].max(axis=-1, keepdims=True)
    for s in score_parts[1:]:
        m = jnp.maximum(m, s.max(axis=-1, keepdims=True))
    l = None
    acc = None
    for s, vb in zip(score_parts, value_parts):
        p = jnp.exp2(s - m)
        ps = jnp.sum(p, axis=-1, keepdims=True)
        pv = _dot(p.astype(BF16), vb)
        l = ps if l is None else l + ps
        acc = pv if acc is None else acc + pv
    return acc / l


def _mla_units(q_ref, k_ref, v_ref, o_ref):
    t = MLA_TQ
    seq = q_ref.shape[1]
    lane = lax.broadcasted_iota(jnp.int32, (1, LANES), 1)
    qry_chunk = lax.broadcasted_iota(jnp.int32, (t, 1), 0) // CHUNK
    key_chunk = lax.broadcasted_iota(jnp.int32, (1, t), 1) // CHUNK
    diag_mask = key_chunk <= qry_chunk

    def scores(i, hh):
        lo = i * t
        hs = slice(hh * LANES, (hh + 1) * LANES)
        q = q_ref[0, lo:lo + t, hs]
        parts = [jnp.where(diag_mask, _dot_nt(q, k_ref[0, lo:lo + t, hs]), NEG_INF)]
        if i > 0:
            parts.insert(0, _dot_nt(q, k_ref[0, 0:lo, hs]))
        return parts

    def finish(i, parts):
        lo = i * t
        vals = [v_ref[0, lo:lo + t, :]]
        if i > 0:
            vals.insert(0, v_ref[0, 0:lo, :])
        return _softmax_pv(parts, vals)

    def store(i, outs):
        o_ref[0, i * t:(i + 1) * t, :] = jnp.where(lane < MLA_V, outs[0], outs[1]).astype(BF16)

    return _pair_units(seq // t, scores, finish, store)


def _ca_units(q_ref, k_ref, vt_ref, bm_ref, o_ref):
    tq, win = CA_TQ, CA_WIN
    seq = q_ref.shape[1]
    lane = lax.broadcasted_iota(jnp.int32, (1, LANES), 1)

    def window(i):
        hi = (i + 1) * tq
        k_lo = max(hi - win, 0)
        return k_lo, hi, k_lo - (hi - win)

    def scores(i, hh):
        k_lo, hi, c_lo = window(i)
        q = q_ref[0, i * tq:hi, :]
        head_lanes = (lane < CA_HEAD_DIM) if hh == 0 else (lane >= CA_HEAD_DIM)
        qm = jnp.where(head_lanes, q, jnp.zeros_like(q))
        return [_dot_nt(k_ref[0, k_lo:hi, :], qm) + bm_ref[hh, c_lo:, :]]

    def finish(i, parts):
        k_lo, hi, _ = window(i)
        return _softmax_pv_t(parts, [vt_ref[0, :, k_lo:hi]])

    def store(i, outs):
        _store_pair(o_ref, i * tq, (i + 1) * tq, outs, CA_HEAD_DIM)

    return _pair_units(seq // tq, scores, finish, store)


def _attn_kernel(q_ref, k_ref, v_ref, cq_ref, ck_ref, cvt_ref, bm_ref, om_ref, oc_ref):
    mla = _mla_units(q_ref, k_ref, v_ref, om_ref)
    ca = _ca_units(cq_ref, ck_ref, cvt_ref, bm_ref, oc_ref)
    assert len(mla) == len(ca)
    units = [u for pair in zip(mla, ca) for u in pair]
    _attention_pipeline(units, ATTN_LOOKAHEAD)


def _attn(q, k, v, cq, ck, cvt, bias_mask):
    bsz, seq, _ = q.shape
    pairs = MLA_HEADS // 2
    assert CA_HEADS // 2 == pairs
    tok = lambda b, p: (b, 0, p)
    wide = pl.BlockSpec((1, seq, 2 * LANES), tok)
    narrow = pl.BlockSpec((1, seq, LANES), tok)
    return pl.pallas_call(
        _attn_kernel,
        grid=(bsz, pairs),
        in_specs=[
            wide, wide, narrow,
            narrow, narrow,
            pl.BlockSpec((1, LANES, seq), lambda b, p: (b, p, 0)),
            pl.BlockSpec((2, CA_WIN, CA_TQ), lambda b, p: (p, 0, 0)),
        ],
        out_specs=[narrow, narrow],
        out_shape=[jax.ShapeDtypeStruct((bsz, seq, MLA_HEADS * MLA_V), BF16),
                   jax.ShapeDtypeStruct((bsz, seq, CA_HEADS * CA_HEAD_DIM), BF16)],
        compiler_params=_compiler_params(("parallel", "parallel")),
        name="attn",
    )(q, k, v, cq, ck, cvt, bias_mask)


def _final_kernel(x_ref, xh_ref, om_ref, omh_ref, oc_ref, och_ref, mod_ref, gmlp_ref,
                  wo_ref, wu_ref, cw_ref, cb_ref, wd_ref,
                  out_ref, h2_sc, ug0_sc, ug1_sc, uv0_sc, uv1_sc, acc_sc):
    i = pl.program_id(1)
    tm = FIN_TM
    halo = FIN_HALO
    mod = mod_ref[0]
    g_a, sh_m, sc_m, g_m = mod[2:3, :], mod[3:4, :], mod[4:5, :], mod[5:6, :]

    u_bufs = ((ug0_sc, uv0_sc), (ug1_sc, uv1_sc))
    rb = FIN_ROW_BLOCK
    n_rb = tm // rb

    def up_rows(r):
        return (0, halo + rb) if r == 0 else (halo + r * rb, halo + (r + 1) * rb)

    x_ext = jnp.concatenate([xh_ref[0], x_ref[0]], axis=0)
    o_ext = jnp.concatenate(
        [jnp.concatenate([omh_ref[0], och_ref[0]], axis=-1), jnp.concatenate([om_ref[0], oc_ref[0]], axis=-1)], axis=0)
    x1 = x_ext + g_a * _dot(o_ext, wo_ref[...])
    out_ref[0] = x1[halo:]
    y = x1 * lax.rsqrt(jnp.mean(x1 * x1, axis=-1, keepdims=True) + EPS) * gmlp_ref[...]
    h2 = y * (1.0 + sc_m) + sh_m
    h2_sc[:halo] = jnp.where(i > 0, h2[:halo], 0.0).astype(BF16)
    h2_sc[halo:] = h2[halo:].astype(BF16)

    def up_project(c, r):
        ug_sc, uv_sc = u_bufs[c % 2]
        lo, hi = up_rows(r)
        h2b = h2_sc[lo:hi, :]
        ug_sc[lo:hi, :] = _dot(h2b, wu_ref[:, c * FF_CHUNK:(c + 1) * FF_CHUNK])
        uv_sc[lo:hi, :] = _dot(h2b, wu_ref[:, D_FF + c * FF_CHUNK:D_FF + (c + 1) * FF_CHUNK])

    def conv(u_sc, col, r):
        cw = cw_ref[:, col:col + FF_CHUNK]
        lo = halo + r * rb
        return (u_sc[lo - 2:lo - 2 + rb, :] * cw[0:1, :] + u_sc[lo - 1:lo - 1 + rb, :] * cw[1:2, :]
                + u_sc[lo:lo + rb, :] * cw[2:3, :] + cb_ref[:, col:col + FF_CHUNK])

    def mlp_rows(c, r):
        ug_sc, uv_sc = u_bufs[c % 2]
        gcol, vcol = c * FF_CHUNK, D_FF + c * FF_CHUNK
        gate = conv(ug_sc, gcol, r)
        val = conv(uv_sc, vcol, r)
        act = (gate / (1.0 + jnp.exp(-gate)) * val).astype(BF16)
        down = _dot(act, wd_ref[gcol:gcol + FF_CHUNK, :])
        rows = slice(r * rb, (r + 1) * rb)
        if c == 0:
            acc_sc[rows, :] = down
        elif c < N_FF_CHUNKS - 1:
            acc_sc[rows, :] += down
        else:
            out_ref[0, rows, :] += g_m * (acc_sc[rows, :] + down)

    for r in range(n_rb):
        up_project(0, r)
    for c in range(N_FF_CHUNKS):
        for r in range(n_rb):
            if c + 1 < N_FF_CHUNKS:
                up_project(c + 1, r)
            mlp_rows(c, r)


def _final(x, o_mla, o_ca, mod, g_mlp, w_out, w_up, conv_w, conv_b, w_down):
    bsz, seq, _ = x.shape
    tm, halo = FIN_TM, FIN_HALO
    tok = lambda b, i: (b, i, 0)
    prev = lambda b, i: (b, jnp.maximum(i * (tm // halo) - 1, 0), 0)
    const2 = lambda b, i: (0, 0)
    d_mix = o_mla.shape[-1]
    resident = dict(pipeline_mode=pl.Buffered(1))
    return pl.pallas_call(
        _final_kernel,
        grid=(bsz, seq // tm),
        in_specs=[
            pl.BlockSpec((1, tm, D_MODEL), tok),
            pl.BlockSpec((1, halo, D_MODEL), prev),
            pl.BlockSpec((1, tm, d_mix), tok),
            pl.BlockSpec((1, halo, d_mix), prev),
            pl.BlockSpec((1, tm, d_mix), tok),
            pl.BlockSpec((1, halo, d_mix), prev),
            pl.BlockSpec((1, N_MOD, D_MODEL), lambda b, i: (b, 0, 0)),
            pl.BlockSpec((1, D_MODEL), const2),
            pl.BlockSpec((2 * d_mix, D_MODEL), const2, **resident),
            pl.BlockSpec((D_MODEL, 2 * D_FF), const2, **resident),
            pl.BlockSpec((3, 2 * D_FF), const2),
            pl.BlockSpec((1, 2 * D_FF), const2),
            pl.BlockSpec((D_FF, D_MODEL), const2, **resident),
        ],
        out_specs=pl.BlockSpec((1, tm, D_MODEL), tok),
        out_shape=jax.ShapeDtypeStruct((bsz, seq, D_MODEL), F32),
        scratch_shapes=[
            pltpu.VMEM((halo + tm, D_MODEL), BF16),
            pltpu.VMEM((halo + tm, FF_CHUNK), F32),
            pltpu.VMEM((halo + tm, FF_CHUNK), F32),
            pltpu.VMEM((halo + tm, FF_CHUNK), F32),
            pltpu.VMEM((halo + tm, FF_CHUNK), F32),
            pltpu.VMEM((tm, D_MODEL), F32),
        ],
        compiler_params=_compiler_params(("parallel", "arbitrary")),
        name="final",
    )(x, x, o_mla, o_mla, o_ca, o_ca, mod, g_mlp, w_out, w_up, conv_w, conv_b, w_down)


def _prep_layer(w_in, w_q_up, w_kv_up, g_mla_q, g_mla_k, g_ca_q, g_ca_k):
    half = MLA_ROPE // 2
    c0 = MLA_Q_RANK + MLA_KV_RANK
    k1 = w_in[:, c0:c0 + half]
    k2 = w_in[:, c0 + half:c0 + MLA_ROPE]
    z64 = jnp.zeros((D_MODEL, MLA_NOPE), w_in.dtype)
    c_cv = c0 + MLA_ROPE + 2 * CA_HEADS * CA_HEAD_DIM
    w_in_ext = jnp.concatenate(
        [w_in[:, :c0], z64, k1, k2, k1, k2, z64, k2, k1, k2, k1, w_in[:, c0 + MLA_ROPE:c_cv]], axis=1).astype(BF16)
    w_cvt = w_in[:, c_cv:].T.astype(BF16)

    wq = w_q_up.reshape(MLA_Q_RANK, MLA_HEADS, MLA_QK)
    x1 = wq[..., MLA_NOPE:MLA_NOPE + half]
    x2 = wq[..., MLA_NOPE + half:]
    w_q_ext = jnp.concatenate([wq, x2, x1], axis=-1).reshape(MLA_Q_RANK, MLA_HEADS * LANES).astype(BF16)

    wkv = w_kv_up.reshape(MLA_KV_RANK, MLA_HEADS, MLA_NOPE + MLA_V)
    wk = jnp.concatenate([wkv[..., :MLA_NOPE], jnp.zeros((MLA_KV_RANK, MLA_HEADS, LANES - MLA_NOPE), wkv.dtype)], axis=-1)
    w_kv_ext = jnp.concatenate(
        [wk.reshape(MLA_KV_RANK, MLA_HEADS * LANES), wkv[..., MLA_NOPE:].reshape(MLA_KV_RANK, MLA_HEADS * MLA_V)],
        axis=1).astype(BF16)

    gq1, gq2 = g_mla_q[MLA_NOPE:MLA_NOPE + half], g_mla_q[MLA_NOPE + half:]
    gk1, gk2 = g_mla_k[MLA_NOPE:MLA_NOPE + half], g_mla_k[MLA_NOPE + half:]
    z = jnp.zeros((MLA_NOPE,), F32)
    rows = [
        jnp.concatenate([g_mla_q, gq2, gq1]) * (MLA_QK ** -0.5 * LOG2E),
        jnp.concatenate([g_mla_k[:MLA_NOPE], z]),
        jnp.concatenate([z, gk1, gk2, gk1, gk2]),
        jnp.concatenate([z, gk2, gk1, gk2, gk1]),
        jnp.concatenate([g_ca_q, g_ca_q]) * (CA_HEAD_DIM ** -0.5 * LOG2E),
        jnp.concatenate([g_ca_k, g_ca_k]),
        jnp.zeros((LANES,), F32),
        jnp.zeros((LANES,), F32),
    ]
    gvec = jnp.stack(rows)
    return w_in_ext, w_q_ext, w_kv_ext, w_cvt, gvec


def kernel(x, c, positions, w_ada, b_ada, g_attn_norm, w_in, g_q_latent, g_kv_latent, w_q_up, w_kv_up, g_mla_q, g_mla_k, g_ca_q, g_ca_k, rel_bias, w_out, g_mlp_norm, w_up, conv_w, conv_b, w_down):
    bsz, seq, _ = x.shape
    depth = w_ada.shape[0]

    tc, ts = _rope_tables(positions)

    for l in range(depth):
        w_in_ext, w_q_ext, w_kv_ext, w_cvt, gvec = _prep_layer(
            w_in[l], w_q_up[l], w_kv_up[l], g_mla_q[l], g_mla_k[l], g_ca_q[l], g_ca_k[l])
        mod = _ada(c, w_ada[l], b_ada[l]).reshape(bsz, N_MOD, D_MODEL)
        bias_mask = _bias_tiles(rel_bias[l])
        q, k, v, cq, ck, cvt = _proj(
            x, mod, tc, ts, g_attn_norm[l].reshape(1, -1), g_q_latent[l].reshape(1, -1),
            g_kv_latent[l].reshape(1, -1), gvec, w_in_ext, w_q_ext, w_kv_ext, w_cvt)
        o_mla, o_ca = _attn(q, k, v, cq, ck, cvt, bias_mask)
        x = _final(x, o_mla, o_ca, mod, g_mlp_norm[l].reshape(1, -1), w_out[l].astype(BF16),
                   w_up[l].astype(BF16), conv_w[l], conv_b[l].reshape(1, -1), w_down[l].astype(BF16))
    return x
```

```python
import math

import numpy as np
import jax
import jax.numpy as jnp
from jax import lax
from jax.experimental import pallas as pl
from jax.experimental.pallas import tpu as pltpu

D_MODEL = 1024
CHUNK = 64
LEFT_CHUNKS = 8
MLA_HEADS = 8
MLA_Q_RANK = 256
MLA_KV_RANK = 128
MLA_NOPE = 64
MLA_ROPE = 32
MLA_QK = MLA_NOPE + MLA_ROPE
MLA_V = 64
ROPE_THETA = 10000.0
CA_HEADS = 8
CA_HEAD_DIM = 64
REL_CLIP = 128
D_FF = 2816
N_MOD = 6
EPS = 1e-6
NEG_INF = -1e30
LOG2E = math.log2(math.e)

LANES = 128
BF16_SUBLANES = 16
VMEM_LIMIT_BYTES = 56 * 1024 * 1024

PROJ_TM = 1024
MLA_TQ = 256
CA_TQ = 256
CA_WIN = CA_TQ + LEFT_CHUNKS * CHUNK
CA_ROLL_W = 1024
REDUCE_ROWS = 64
ATTN_LOOKAHEAD = 6
FIN_TM = 512
FIN_HALO = BF16_SUBLANES
FIN_ROW_BLOCK = 128
FF_CHUNK = 256
N_FF_CHUNKS = D_FF // FF_CHUNK

_C_QLAT = 0
_C_KVLAT = _C_QLAT + MLA_Q_RANK
_C_KA = _C_KVLAT + MLA_KV_RANK
_C_KB = _C_KA + LANES
_C_CQ = _C_KB + LANES
_C_CK = _C_CQ + CA_HEADS * CA_HEAD_DIM
D_IN_EXT = _C_CK + CA_HEADS * CA_HEAD_DIM

F32 = jnp.float32
BF16 = jnp.bfloat16


def _dot(a, b):
    return jnp.dot(a, b, preferred_element_type=F32)


def _dot_nt(a, b):
    return lax.dot_general(a, b, (((1,), (1,)), ((), ())), preferred_element_type=F32)


def _compiler_params(semantics):
    return pltpu.CompilerParams(dimension_semantics=semantics, vmem_limit_bytes=VMEM_LIMIT_BYTES)


def _ada_kernel(c_ref, w_ref, b_ref, o_ref):
    c = c_ref[...]
    s = c / (1.0 + jnp.exp(-c))
    o_ref[...] = _dot(s.astype(BF16), w_ref[...].astype(BF16)) + b_ref[...]


def _ada(c, w_ada, b_ada):
    bsz = c.shape[0]
    n_out = w_ada.shape[1]
    tn = D_MODEL
    return pl.pallas_call(
        _ada_kernel,
        grid=(n_out // tn,),
        in_specs=[
            pl.BlockSpec((bsz, D_MODEL), lambda j: (0, 0)),
            pl.BlockSpec((D_MODEL, tn), lambda j: (0, j)),
            pl.BlockSpec((1, tn), lambda j: (0, j)),
        ],
        out_specs=pl.BlockSpec((bsz, tn), lambda j: (0, j)),
        out_shape=jax.ShapeDtypeStruct((bsz, n_out), F32),
        compiler_params=_compiler_params(("arbitrary",)),
        name="ada",
    )(c, w_ada, b_ada.reshape(1, n_out))


def _expand(d, e_ref):
    hi = d.astype(BF16)
    r1 = d - hi.astype(F32)
    mid = r1.astype(BF16)
    lo = (r1 - mid.astype(F32)).astype(BF16)
    e = e_ref[...]
    return _dot(hi, e) + _dot(mid, e) + _dot(lo, e)


def _rope_kernel(pos_ref, rep_ref, inv_ref, ec_ref, es_ref, base_ref, tc_ref, ts_ref):
    rows = pos_ref.shape[0]
    tok_per_row = ec_ref.shape[1] // LANES
    pos = _expand(pos_ref[...].astype(F32), rep_ref)
    ang = pos * inv_ref[...]
    tc_wide = _expand(jnp.cos(ang), ec_ref) + base_ref[...]
    ts_wide = _expand(jnp.sin(ang), es_ref)
    for t in range(tok_per_row):
        tc_ref[pl.ds(t, rows, stride=tok_per_row), :] = tc_wide[:, t * LANES:(t + 1) * LANES]
        ts_ref[pl.ds(t, rows, stride=tok_per_row), :] = ts_wide[:, t * LANES:(t + 1) * LANES]


def _rope_tables(positions):
    half = MLA_ROPE // 2
    bsz, seq = positions.shape
    n_tok = bsz * seq
    tok_per_row = LANES // half
    rows = n_tok // tok_per_row
    inv = jnp.power(ROPE_THETA, -jnp.arange(half, dtype=F32) / half)
    inv_t = jnp.tile(inv, tok_per_row).reshape(1, LANES)
    spread = np.repeat(np.eye(tok_per_row, dtype=np.float32), half, axis=1)

    src = np.arange(LANES)
    sel_c = np.zeros((LANES, tok_per_row * LANES), np.float32)
    sel_s = np.zeros((LANES, tok_per_row * LANES), np.float32)
    for rep, sign in enumerate((-1.0, 1.0, -1.0, 1.0)):
        dst = (src // half) * LANES + MLA_NOPE + rep * half + src % half
        sel_c[src, dst] = 1.0
        sel_s[src, dst] = sign
    base = np.tile((np.arange(LANES) < MLA_NOPE).astype(np.float32), tok_per_row).reshape(1, -1)

    tr = 512
    wide = tok_per_row * LANES
    const = lambda i: (0, 0)
    tc, ts = pl.pallas_call(
        _rope_kernel,
        grid=(rows // tr,),
        in_specs=[
            pl.BlockSpec((tr, tok_per_row), lambda i: (i, 0)),
            pl.BlockSpec((tok_per_row, LANES), const),
            pl.BlockSpec((1, LANES), const),
            pl.BlockSpec((LANES, wide), const),
            pl.BlockSpec((LANES, wide), const),
            pl.BlockSpec((1, wide), const),
        ],
        out_specs=[pl.BlockSpec((tr * tok_per_row, LANES), lambda i: (i, 0))] * 2,
        out_shape=[jax.ShapeDtypeStruct((n_tok, LANES), F32)] * 2,
        compiler_params=_compiler_params(("arbitrary",)),
        name="rope",
    )(positions.reshape(rows, tok_per_row), jnp.asarray(spread, BF16), inv_t,
      jnp.asarray(sel_c, BF16), jnp.asarray(sel_s, BF16), jnp.asarray(base))
    return tc.reshape(bsz, seq, LANES), ts.reshape(bsz, seq, LANES)


def _bias_kernel(y_ref, o_ref):
    tq, win = CA_TQ, CA_WIN
    full = jnp.broadcast_to(y_ref[0], (tq, CA_ROLL_W))
    rolled = pltpu.roll(full, 0, 1, stride=1, stride_axis=0)
    q_chunk = lax.broadcasted_iota(jnp.int32, (tq, 1), 0) // CHUNK
    k_chunk = lax.broadcasted_iota(jnp.int32, (1, win), 1) // CHUNK
    valid = jnp.logical_and(k_chunk >= q_chunk, k_chunk <= q_chunk + LEFT_CHUNKS)
    o_ref[0] = jnp.where(valid, rolled[:, :win] * LOG2E, NEG_INF).T


def _bias_tiles(rel_bias):
    n_heads = rel_bias.shape[0]
    t = jnp.arange(CA_ROLL_W)
    d = jnp.where(t < CA_WIN, t, t - CA_ROLL_W)
    idx = jnp.clip(LEFT_CHUNKS * CHUNK - d, -REL_CLIP, REL_CLIP) + REL_CLIP
    y = rel_bias[:, idx].reshape(n_heads, 1, CA_ROLL_W)
    return pl.pallas_call(
        _bias_kernel,
        grid=(n_heads,),
        in_specs=[pl.BlockSpec((1, 1, CA_ROLL_W), lambda h: (h, 0, 0))],
        out_specs=pl.BlockSpec((1, CA_WIN, CA_TQ), lambda h: (h, 0, 0)),
        out_shape=jax.ShapeDtypeStruct((n_heads, CA_WIN, CA_TQ), F32),
        compiler_params=_compiler_params(("arbitrary",)),
        name="bias",
    )(y)


def _proj_kernel(x_ref, mod_ref, tc_ref, ts_ref, gattn_ref, gql_ref, gkvl_ref, gv_ref,
                 win_ref, wq_ref, wkv_ref, wcvt_ref,
                 q_ref, k_ref, v_ref, cq_ref, ck_ref, cvt_ref):
    x = x_ref[0]
    mod = mod_ref[0]
    sh, sc = mod[0:1, :], mod[1:2, :]
    y = x * lax.rsqrt(jnp.mean(x * x, axis=-1, keepdims=True) + EPS) * gattn_ref[...]
    h = (y * (1.0 + sc) + sh).astype(BF16)
    proj = _dot(h, win_ref[...])

    lane = lax.broadcasted_iota(jnp.int32, (1, LANES), 1)
    gv = gv_ref[...]
    g_q, g_kn, g_ka, g_kb, g_cq, g_ck = (gv[r:r + 1, :] for r in range(6))

    tc = tc_ref[0]
    ts = ts_ref[0]
    tq = jnp.where(lane < MLA_QK, tc, ts)

    ql = proj[:, _C_QLAT:_C_QLAT + MLA_Q_RANK]
    qn = (ql * lax.rsqrt(jnp.mean(ql * ql, axis=-1, keepdims=True) + EPS) * gql_ref[...]).astype(BF16)
    qp = _dot(qn, wq_ref[...])
    qmul = g_q * tq
    for hd in range(MLA_HEADS):
        qh = qp[:, hd * LANES:(hd + 1) * LANES]
        ss = jnp.sum(jnp.where(lane < MLA_QK, qh * qh, 0.0), axis=-1, keepdims=True)
        r = lax.rsqrt(ss * (1.0 / MLA_QK) + EPS)
        q_ref[0, :, hd * LANES:(hd + 1) * LANES] = (qh * r * qmul).astype(BF16)

    kvl = proj[:, _C_KVLAT:_C_KVLAT + MLA_KV_RANK]
    kvn = (kvl * lax.rsqrt(jnp.mean(kvl * kvl, axis=-1, keepdims=True) + EPS) * gkvl_ref[...]).astype(BF16)
    kvp = _dot(kvn, wkv_ref[...])
    ka = proj[:, _C_KA:_C_KA + LANES]
    kb = proj[:, _C_KB:_C_KB + LANES]
    krot = ka * g_ka * tc + kb * g_kb * ts
    ss_rope = jnp.sum(jnp.where(lane < MLA_QK, ka * ka, 0.0), axis=-1, keepdims=True)
    for hd in range(MLA_HEADS):
        kh = kvp[:, hd * LANES:(hd + 1) * LANES]
        ss = jnp.sum(kh * kh, axis=-1, keepdims=True) + ss_rope
        r = lax.rsqrt(ss * (1.0 / MLA_QK) + EPS)
        k_ref[0, :, hd * LANES:(hd + 1) * LANES] = ((kh * g_kn + krot) * r).astype(BF16)
    v_ref[0] = kvp[:, MLA_HEADS * LANES:].astype(BF16)

    lo = lane < CA_HEAD_DIM
    for src, gain, dst in ((_C_CQ, g_cq, cq_ref), (_C_CK, g_ck, ck_ref)):
        for p in range(CA_HEADS // 2):
            xx = proj[:, src + p * LANES:src + (p + 1) * LANES]
            x2 = xx * xx
            s_all = jnp.sum(x2, axis=-1, keepdims=True)
            s_lo = jnp.sum(jnp.where(lo, x2, 0.0), axis=-1, keepdims=True)
            r_lo = lax.rsqrt(s_lo * (1.0 / CA_HEAD_DIM) + EPS)
            r_hi = lax.rsqrt((s_all - s_lo) * (1.0 / CA_HEAD_DIM) + EPS)
            dst[0, :, p * LANES:(p + 1) * LANES] = (xx * jnp.where(lo, r_lo, r_hi) * gain).astype(BF16)
    cvt_ref[0] = _dot_nt(wcvt_ref[...], h).astype(BF16)


def _proj(x, mod, tc, ts, g_attn, g_ql, g_kvl, gvec, w_in_ext, w_q_ext, w_kv_ext, w_cvt):
    bsz, seq, _ = x.shape
    tm = PROJ_TM
    tok = lambda b, i: (b, i, 0)
    tok_t = lambda b, i: (b, 0, i)
    const2 = lambda b, i: (0, 0)
    wide = MLA_HEADS * LANES
    narrow = CA_HEADS * CA_HEAD_DIM
    row_major = lambda w: (jax.ShapeDtypeStruct((bsz, seq, w), BF16), pl.BlockSpec((1, tm, w), tok))
    feat_major = lambda w: (jax.ShapeDtypeStruct((bsz, w, seq), BF16), pl.BlockSpec((1, w, tm), tok_t))
    outs = [row_major(wide), row_major(wide), row_major(MLA_HEADS * MLA_V),
            row_major(narrow), row_major(narrow), feat_major(narrow)]
    out_shapes = [o[0] for o in outs]
    out_specs = [o[1] for o in outs]
    return pl.pallas_call(
        _proj_kernel,
        grid=(bsz, seq // tm),
        in_specs=[
            pl.BlockSpec((1, tm, D_MODEL), tok),
            pl.BlockSpec((1, N_MOD, D_MODEL), lambda b, i: (b, 0, 0)),
            pl.BlockSpec((1, tm, LANES), tok),
            pl.BlockSpec((1, tm, LANES), tok),
            pl.BlockSpec((1, D_MODEL), const2),
            pl.BlockSpec((1, MLA_Q_RANK), const2),
            pl.BlockSpec((1, MLA_KV_RANK), const2),
            pl.BlockSpec((8, LANES), const2),
            pl.BlockSpec((D_MODEL, D_IN_EXT), const2),
            pl.BlockSpec((MLA_Q_RANK, wide), const2),
            pl.BlockSpec((MLA_KV_RANK, wide + MLA_HEADS * MLA_V), const2),
            pl.BlockSpec((narrow, D_MODEL), const2),
        ],
        out_specs=out_specs,
        out_shape=out_shapes,
        compiler_params=_compiler_params(("parallel", "parallel")),
        name="proj",
    )(x, mod, tc, ts, g_attn, g_ql, g_kvl, gvec, w_in_ext, w_q_ext, w_kv_ext, w_cvt)


def _softmax_pv_t(score_parts, value_parts):
    def fold(x, op):
        return op(x.reshape(x.shape[0] // REDUCE_ROWS, REDUCE_ROWS, x.shape[1]), axis=0)

    m = fold(score_parts[0], jnp.max)
    for s in score_parts[1:]:
        m = jnp.maximum(m, fold(s, jnp.max))
    m = m.max(axis=0, keepdims=True)
    l = None
    acc = None
    for s, vt in zip(score_parts, value_parts):
        p = jnp.exp2(s - m)
        ps = fold(p, jnp.sum).sum(axis=0, keepdims=True)
        pv = _dot(vt, p.astype(BF16))
        l = ps if l is None else l + ps
        acc = pv if acc is None else acc + pv
    return acc / l


def _attention_pipeline(units, lookahead):
    pending = [scores() for scores, _ in units[:lookahead]]
    for n, (_, finish) in enumerate(units):
        if n + lookahead < len(units):
            pending.append(units[n + lookahead][0]())
        finish(pending.pop(0))


def _pair_units(n_tiles, scores, finish, store):
    units = []
    outs = []

    def make(i, hh):
        def finish_unit(parts):
            outs.append(finish(i, parts))
            if hh == 1:
                store(i, list(outs))
                outs.clear()
        return (lambda: scores(i, hh)), finish_unit

    for i in range(n_tiles):
        for hh in range(2):
            units.append(make(i, hh))
    return units


def _store_pair(o_ref, lo, hi, outs, head_rows):
    row = lax.broadcasted_iota(jnp.int32, (LANES, 1), 0)
    o_t = jnp.where(row < head_rows, outs[0], outs[1])
    o_ref[0, lo:hi, :] = o_t.T.astype(BF16)


def _softmax_pv(score_parts, value_parts):
    m = score_parts[0].max(axis=-1, keepdims=True)
    for s in score_parts[1:]:
        m = jnp.maximum(m, s.max(axis=-1, keepdims=True))
    l = None
    acc = None
    for s, vb in zip(score_parts, value_parts):
        p = jnp.exp2(s - m)
        ps = jnp.sum(p, axis=-1, keepdims=True)
        pv = _dot(p.astype(BF16), vb)
        l = ps if l is None else l + ps
        acc = pv if acc is None else acc + pv
    return acc / l


def _mla_units(q_ref, k_ref, v_ref, o_ref):
    t = MLA_TQ
    seq = q_ref.shape[1]
    lane = lax.broadcasted_iota(jnp.int32, (1, LANES), 1)
    qry_chunk = lax.broadcasted_iota(jnp.int32, (t, 1), 0) // CHUNK
    key_chunk = lax.broadcasted_iota(jnp.int32, (1, t), 1) // CHUNK
    diag_mask = key_chunk <= qry_chunk

    def scores(i, hh):
        lo = i * t
        hs = slice(hh * LANES, (hh + 1) * LANES)
        q = q_ref[0, lo:lo + t, hs]
        parts = [jnp.where(diag_mask, _dot_nt(q, k_ref[0, lo:lo + t, hs]), NEG_INF)]
        if i > 0:
            parts.insert(0, _dot_nt(q, k_ref[0, 0:lo, hs]))
        return parts

    def finish(i, parts):
        lo = i * t
        vals = [v_ref[0, lo:lo + t, :]]
        if i > 0:
            vals.insert(0, v_ref[0, 0:lo, :])
        return _softmax_pv(parts, vals)

    def store(i, outs):
        o_ref[0, i * t:(i + 1) * t, :] = jnp.where(lane < MLA_V, outs[0], outs[1]).astype(BF16)

    return _pair_units(seq // t, scores, finish, store)


def _ca_units(q_ref, k_ref, vt_ref, bm_ref, o_ref):
    tq, win = CA_TQ, CA_WIN
    seq = q_ref.shape[1]
    lane = lax.broadcasted_iota(jnp.int32, (1, LANES), 1)

    def window(i):
        hi = (i + 1) * tq
        k_lo = max(hi - win, 0)
        return k_lo, hi, k_lo - (hi - win)

    def scores(i, hh):
        k_lo, hi, c_lo = window(i)
        q = q_ref[0, i * tq:hi, :]
        head_lanes = (lane < CA_HEAD_DIM) if hh == 0 else (lane >= CA_HEAD_DIM)
        qm = jnp.where(head_lanes, q, jnp.zeros_like(q))
        return [_dot_nt(k_ref[0, k_lo:hi, :], qm) + bm_ref[hh, c_lo:, :]]

    def finish(i, parts):
        k_lo, hi, _ = window(i)
        return _softmax_pv_t(parts, [vt_ref[0, :, k_lo:hi]])

    def store(i, outs):
        _store_pair(o_ref, i * tq, (i + 1) * tq, outs, CA_HEAD_DIM)

    return _pair_units(seq // tq, scores, finish, store)


def _attn_kernel(q_ref, k_ref, v_ref, cq_ref, ck_ref, cvt_ref, bm_ref, om_ref, oc_ref):
    mla = _mla_units(q_ref, k_ref, v_ref, om_ref)
    ca = _ca_units(cq_ref, ck_ref, cvt_ref, bm_ref, oc_ref)
    assert len(mla) == len(ca)
    units = [u for pair in zip(mla, ca) for u in pair]
    _attention_pipeline(units, ATTN_LOOKAHEAD)


def _attn(q, k, v, cq, ck, cvt, bias_mask):
    bsz, seq, _ = q.shape
    pairs = MLA_HEADS // 2
    assert CA_HEADS // 2 == pairs
    tok = lambda b, p: (b, 0, p)
    wide = pl.BlockSpec((1, seq, 2 * LANES), tok)
    narrow = pl.BlockSpec((1, seq, LANES), tok)
    return pl.pallas_call(
        _attn_kernel,
        grid=(bsz, pairs),
        in_specs=[
            wide, wide, narrow,
            narrow, narrow,
            pl.BlockSpec((1, LANES, seq), lambda b, p: (b, p, 0)),
            pl.BlockSpec((2, CA_WIN, CA_TQ), lambda b, p: (p, 0, 0)),
        ],
        out_specs=[narrow, narrow],
        out_shape=[jax.ShapeDtypeStruct((bsz, seq, MLA_HEADS * MLA_V), BF16),
                   jax.ShapeDtypeStruct((bsz, seq, CA_HEADS * CA_HEAD_DIM), BF16)],
        compiler_params=_compiler_params(("parallel", "parallel")),
        name="attn",
    )(q, k, v, cq, ck, cvt, bias_mask)


def _final_kernel(x_ref, xh_ref, om_ref, omh_ref, oc_ref, och_ref, mod_ref, gmlp_ref,
                  wo_ref, wu_ref, cw_ref, cb_ref, wd_ref,
                  out_ref, h2_sc, ug0_sc, ug1_sc, uv0_sc, uv1_sc, act_sc, acc_sc):
    i = pl.program_id(1)
    tm = FIN_TM
    halo = FIN_HALO
    mod = mod_ref[0]
    g_a, sh_m, sc_m, g_m = mod[2:3, :], mod[3:4, :], mod[4:5, :], mod[5:6, :]

    u_bufs = ((ug0_sc, uv0_sc), (ug1_sc, uv1_sc))
    rb = FIN_ROW_BLOCK
    n_rb = tm // rb

    def up_rows(r):
        return (0, halo + rb) if r == 0 else (halo + r * rb, halo + (r + 1) * rb)

    x_ext = jnp.concatenate([xh_ref[0], x_ref[0]], axis=0)
    o_ext = jnp.concatenate(
        [jnp.concatenate([omh_ref[0], och_ref[0]], axis=-1), jnp.concatenate([om_ref[0], oc_ref[0]], axis=-1)], axis=0)
    x1 = x_ext + g_a * _dot(o_ext, wo_ref[...])
    out_ref[0] = x1[halo:]
    y = x1 * lax.rsqrt(jnp.mean(x1 * x1, axis=-1, keepdims=True) + EPS) * gmlp_ref[...]
    h2 = y * (1.0 + sc_m) + sh_m
    h2_sc[:halo] = jnp.where(i > 0, h2[:halo], 0.0).astype(BF16)
    h2_sc[halo:] = h2[halo:].astype(BF16)

    def up_project(c, r):
        ug_sc, uv_sc = u_bufs[c % 2]
        lo, hi = up_rows(r)
        h2b = h2_sc[lo:hi, :]
        ug_sc[lo:hi, :] = _dot(h2b, wu_ref[:, c * FF_CHUNK:(c + 1) * FF_CHUNK])
        uv_sc[lo:hi, :] = _dot(h2b, wu_ref[:, D_FF + c * FF_CHUNK:D_FF + (c + 1) * FF_CHUNK])

    def conv(u_sc, col, r):
        cw = cw_ref[:, col:col + FF_CHUNK]
        lo = halo + r * rb
        return (u_sc[lo - 2:lo - 2 + rb, :] * cw[0:1, :] + u_sc[lo - 1:lo - 1 + rb, :] * cw[1:2, :]
                + u_sc[lo:lo + rb, :] * cw[2:3, :] + cb_ref[:, col:col + FF_CHUNK])

    def activate(c, r):
        ug_sc, uv_sc = u_bufs[c % 2]
        gcol, vcol = c * FF_CHUNK, D_FF + c * FF_CHUNK
        gate = conv(ug_sc, gcol, r)
        val = conv(uv_sc, vcol, r)
        act_sc[r * rb:(r + 1) * rb, :] = (gate / (1.0 + jnp.exp(-gate)) * val).astype(BF16)

    def down_project(c):
        down = _dot(act_sc[...], wd_ref[c * FF_CHUNK:(c + 1) * FF_CHUNK, :])
        if c == 0:
            acc_sc[...] = down
        elif c < N_FF_CHUNKS - 1:
            acc_sc[...] += down
        else:
            out_ref[0] += g_m * (acc_sc[...] + down)

    for r in range(n_rb):
        up_project(0, r)
    for c in range(N_FF_CHUNKS):
        for r in range(n_rb):
            if c + 1 < N_FF_CHUNKS:
                up_project(c + 1, r)
            activate(c, r)
        down_project(c)


def _final(x, o_mla, o_ca, mod, g_mlp, w_out, w_up, conv_w, conv_b, w_down):
    bsz, seq, _ = x.shape
    tm, halo = FIN_TM, FIN_HALO
    tok = lambda b, i: (b, i, 0)
    prev = lambda b, i: (b, jnp.maximum(i * (tm // halo) - 1, 0), 0)
    const2 = lambda b, i: (0, 0)
    d_mix = o_mla.shape[-1]
    resident = dict(pipeline_mode=pl.Buffered(1))
    return pl.pallas_call(
        _final_kernel,
        grid=(bsz, seq // tm),
        in_specs=[
            pl.BlockSpec((1, tm, D_MODEL), tok),
            pl.BlockSpec((1, halo, D_MODEL), prev),
            pl.BlockSpec((1, tm, d_mix), tok),
            pl.BlockSpec((1, halo, d_mix), prev),
            pl.BlockSpec((1, tm, d_mix), tok),
            pl.BlockSpec((1, halo, d_mix), prev),
            pl.BlockSpec((1, N_MOD, D_MODEL), lambda b, i: (b, 0, 0)),
            pl.BlockSpec((1, D_MODEL), const2),
            pl.BlockSpec((2 * d_mix, D_MODEL), const2, **resident),
            pl.BlockSpec((D_MODEL, 2 * D_FF), const2, **resident),
            pl.BlockSpec((3, 2 * D_FF), const2),
            pl.BlockSpec((1, 2 * D_FF), const2),
            pl.BlockSpec((D_FF, D_MODEL), const2, **resident),
        ],
        out_specs=pl.BlockSpec((1, tm, D_MODEL), tok),
        out_shape=jax.ShapeDtypeStruct((bsz, seq, D_MODEL), F32),
        scratch_shapes=[
            pltpu.VMEM((halo + tm, D_MODEL), BF16),
            pltpu.VMEM((halo + tm, FF_CHUNK), F32),
            pltpu.VMEM((halo + tm, FF_CHUNK), F32),
            pltpu.VMEM((halo + tm, FF_CHUNK), F32),
            pltpu.VMEM((halo + tm, FF_CHUNK), F32),
            pltpu.VMEM((tm, FF_CHUNK), BF16),
            pltpu.VMEM((tm, D_MODEL), F32),
        ],
        compiler_params=_compiler_params(("parallel", "arbitrary")),
        name="final",
    )(x, x, o_mla, o_mla, o_ca, o_ca, mod, g_mlp, w_out, w_up, conv_w, conv_b, w_down)


def _prep_layer(w_in, w_q_up, w_kv_up, g_mla_q, g_mla_k, g_ca_q, g_ca_k):
    half = MLA_ROPE // 2
    c0 = MLA_Q_RANK + MLA_KV_RANK
    k1 = w_in[:, c0:c0 + half]
    k2 = w_in[:, c0 + half:c0 + MLA_ROPE]
    z64 = jnp.zeros((D_MODEL, MLA_NOPE), w_in.dtype)
    c_cv = c0 + MLA_ROPE + 2 * CA_HEADS * CA_HEAD_DIM
    w_in_ext = jnp.concatenate(
        [w_in[:, :c0], z64, k1, k2, k1, k2, z64, k2, k1, k2, k1, w_in[:, c0 + MLA_ROPE:c_cv]], axis=1).astype(BF16)
    w_cvt = w_in[:, c_cv:].T.astype(BF16)

    wq = w_q_up.reshape(MLA_Q_RANK, MLA_HEADS, MLA_QK)
    x1 = wq[..., MLA_NOPE:MLA_NOPE + half]
    x2 = wq[..., MLA_NOPE + half:]
    w_q_ext = jnp.concatenate([wq, x2, x1], axis=-1).reshape(MLA_Q_RANK, MLA_HEADS * LANES).astype(BF16)

    wkv = w_kv_up.reshape(MLA_KV_RANK, MLA_HEADS, MLA_NOPE + MLA_V)
    wk = jnp.concatenate([wkv[..., :MLA_NOPE], jnp.zeros((MLA_KV_RANK, MLA_HEADS, LANES - MLA_NOPE), wkv.dtype)], axis=-1)
    w_kv_ext = jnp.concatenate(
        [wk.reshape(MLA_KV_RANK, MLA_HEADS * LANES), wkv[..., MLA_NOPE:].reshape(MLA_KV_RANK, MLA_HEADS * MLA_V)],
        axis=1).astype(BF16)

    gq1, gq2 = g_mla_q[MLA_NOPE:MLA_NOPE + half], g_mla_q[MLA_NOPE + half:]
    gk1, gk2 = g_mla_k[MLA_NOPE:MLA_NOPE + half], g_mla_k[MLA_NOPE + half:]
    z = jnp.zeros((MLA_NOPE,), F32)
    rows = [
        jnp.concatenate([g_mla_q, gq2, gq1]) * (MLA_QK ** -0.5 * LOG2E),
        jnp.concatenate([g_mla_k[:MLA_NOPE], z]),
        jnp.concatenate([z, gk1, gk2, gk1, gk2]),
        jnp.concatenate([z, gk2, gk1, gk2, gk1]),
        jnp.concatenate([g_ca_q, g_ca_q]) * (CA_HEAD_DIM ** -0.5 * LOG2E),
        jnp.concatenate([g_ca_k, g_ca_k]),
        jnp.zeros((LANES,), F32),
        jnp.zeros((LANES,), F32),
    ]
    gvec = jnp.stack(rows)
    return w_in_ext, w_q_ext, w_kv_ext, w_cvt, gvec


def kernel(x, c, positions, w_ada, b_ada, g_attn_norm, w_in, g_q_latent, g_kv_latent, w_q_up, w_kv_up, g_mla_q, g_mla_k, g_ca_q, g_ca_k, rel_bias, w_out, g_mlp_norm, w_up, conv_w, conv_b, w_down):
    bsz, seq, _ = x.shape
    depth = w_ada.shape[0]

    tc, ts = _rope_tables(positions)

    for l in range(depth):
        w_in_ext, w_q_ext, w_kv_ext, w_cvt, gvec = _prep_layer(
            w_in[l], w_q_up[l], w_kv_up[l], g_mla_q[l], g_mla_k[l], g_ca_q[l], g_ca_k[l])
        mod = _ada(c, w_ada[l], b_ada[l]).reshape(bsz, N_MOD, D_MODEL)
        bias_mask = _bias_tiles(rel_bias[l])
        q, k, v, cq, ck, cvt = _proj(
            x, mod, tc, ts, g_attn_norm[l].reshape(1, -1), g_q_latent[l].reshape(1, -1),
            g_kv_latent[l].reshape(1, -1), gvec, w_in_ext, w_q_ext, w_kv_ext, w_cvt)
        o_mla, o_ca = _attn(q, k, v, cq, ck, cvt, bias_mask)
        x = _final(x, o_mla, o_ca, mod, g_mlp_norm[l].reshape(1, -1), w_out[l].astype(BF16),
                   w_up[l].astype(BF16), conv_w[l], conv_b[l].reshape(1, -1), w_down[l].astype(BF16))
    return x
```

```python
import math

import numpy as np
import jax
import jax.numpy as jnp
from jax import lax
from jax.experimental import pallas as pl
from jax.experimental.pallas import tpu as pltpu

D_MODEL = 1024
CHUNK = 64
LEFT_CHUNKS = 8
MLA_HEADS = 8
MLA_Q_RANK = 256
MLA_KV_RANK = 128
MLA_NOPE = 64
MLA_ROPE = 32
MLA_QK = MLA_NOPE + MLA_ROPE
MLA_V = 64
ROPE_THETA = 10000.0
CA_HEADS = 8
CA_HEAD_DIM = 64
REL_CLIP = 128
D_FF = 2816
N_MOD = 6
EPS = 1e-6
NEG_INF = -1e30
LOG2E = math.log2(math.e)

LANES = 128
BF16_SUBLANES = 16
VMEM_LIMIT_BYTES = 56 * 1024 * 1024

PROJ_TM = 1024
MLA_TQ = 256
CA_TQ = 256
CA_WIN = CA_TQ + LEFT_CHUNKS * CHUNK
CA_ROLL_W = 1024
REDUCE_ROWS = 64
ATTN_LOOKAHEAD = 6
FIN_TM = 512
FIN_HALO = BF16_SUBLANES
FIN_ROW_BLOCK = 128
FIN_UP_BLOCK = 256
FF_CHUNK = 256
N_FF_CHUNKS = D_FF // FF_CHUNK

_C_QLAT = 0
_C_KVLAT = _C_QLAT + MLA_Q_RANK
_C_KA = _C_KVLAT + MLA_KV_RANK
_C_KB = _C_KA + LANES
_C_CQ = _C_KB + LANES
_C_CK = _C_CQ + CA_HEADS * CA_HEAD_DIM
D_IN_EXT = _C_CK + CA_HEADS * CA_HEAD_DIM

F32 = jnp.float32
BF16 = jnp.bfloat16


def _dot(a, b):
    return jnp.dot(a, b, preferred_element_type=F32)


def _dot_nt(a, b):
    return lax.dot_general(a, b, (((1,), (1,)), ((), ())), preferred_element_type=F32)


def _compiler_params(semantics):
    return pltpu.CompilerParams(dimension_semantics=semantics, vmem_limit_bytes=VMEM_LIMIT_BYTES)


def _ada_kernel(c_ref, w_ref, b_ref, o_ref):
    c = c_ref[...]
    s = c / (1.0 + jnp.exp(-c))
    o_ref[...] = _dot(s.astype(BF16), w_ref[...].astype(BF16)) + b_ref[...]


def _ada(c, w_ada, b_ada):
    bsz = c.shape[0]
    n_out = w_ada.shape[1]
    tn = D_MODEL
    return pl.pallas_call(
        _ada_kernel,
        grid=(n_out // tn,),
        in_specs=[
            pl.BlockSpec((bsz, D_MODEL), lambda j: (0, 0)),
            pl.BlockSpec((D_MODEL, tn), lambda j: (0, j)),
            pl.BlockSpec((1, tn), lambda j: (0, j)),
        ],
        out_specs=pl.BlockSpec((bsz, tn), lambda j: (0, j)),
        out_shape=jax.ShapeDtypeStruct((bsz, n_out), F32),
        compiler_params=_compiler_params(("arbitrary",)),
        name="ada",
    )(c, w_ada, b_ada.reshape(1, n_out))


def _expand(d, e_ref):
    hi = d.astype(BF16)
    r1 = d - hi.astype(F32)
    mid = r1.astype(BF16)
    lo = (r1 - mid.astype(F32)).astype(BF16)
    e = e_ref[...]
    return _dot(hi, e) + _dot(mid, e) + _dot(lo, e)


def _rope_kernel(pos_ref, rep_ref, inv_ref, ec_ref, es_ref, base_ref, tc_ref, ts_ref):
    rows = pos_ref.shape[0]
    tok_per_row = ec_ref.shape[1] // LANES
    pos = _expand(pos_ref[...].astype(F32), rep_ref)
    ang = pos * inv_ref[...]
    tc_wide = _expand(jnp.cos(ang), ec_ref) + base_ref[...]
    ts_wide = _expand(jnp.sin(ang), es_ref)
    for t in range(tok_per_row):
        tc_ref[pl.ds(t, rows, stride=tok_per_row), :] = tc_wide[:, t * LANES:(t + 1) * LANES]
        ts_ref[pl.ds(t, rows, stride=tok_per_row), :] = ts_wide[:, t * LANES:(t + 1) * LANES]


def _rope_tables(positions):
    half = MLA_ROPE // 2
    bsz, seq = positions.shape
    n_tok = bsz * seq
    tok_per_row = LANES // half
    rows = n_tok // tok_per_row
    inv = jnp.power(ROPE_THETA, -jnp.arange(half, dtype=F32) / half)
    inv_t = jnp.tile(inv, tok_per_row).reshape(1, LANES)
    spread = np.repeat(np.eye(tok_per_row, dtype=np.float32), half, axis=1)

    src = np.arange(LANES)
    sel_c = np.zeros((LANES, tok_per_row * LANES), np.float32)
    sel_s = np.zeros((LANES, tok_per_row * LANES), np.float32)
    for rep, sign in enumerate((-1.0, 1.0, -1.0, 1.0)):
        dst = (src // half) * LANES + MLA_NOPE + rep * half + src % half
        sel_c[src, dst] = 1.0
        sel_s[src, dst] = sign
    base = np.tile((np.arange(LANES) < MLA_NOPE).astype(np.float32), tok_per_row).reshape(1, -1)

    tr = 512
    wide = tok_per_row * LANES
    const = lambda i: (0, 0)
    tc, ts = pl.pallas_call(
        _rope_kernel,
        grid=(rows // tr,),
        in_specs=[
            pl.BlockSpec((tr, tok_per_row), lambda i: (i, 0)),
            pl.BlockSpec((tok_per_row, LANES), const),
            pl.BlockSpec((1, LANES), const),
            pl.BlockSpec((LANES, wide), const),
            pl.BlockSpec((LANES, wide), const),
            pl.BlockSpec((1, wide), const),
        ],
        out_specs=[pl.BlockSpec((tr * tok_per_row, LANES), lambda i: (i, 0))] * 2,
        out_shape=[jax.ShapeDtypeStruct((n_tok, LANES), F32)] * 2,
        compiler_params=_compiler_params(("arbitrary",)),
        name="rope",
    )(positions.reshape(rows, tok_per_row), jnp.asarray(spread, BF16), inv_t,
      jnp.asarray(sel_c, BF16), jnp.asarray(sel_s, BF16), jnp.asarray(base))
    return tc.reshape(bsz, seq, LANES), ts.reshape(bsz, seq, LANES)


def _bias_kernel(y_ref, o_ref):
    tq, win = CA_TQ, CA_WIN
    full = jnp.broadcast_to(y_ref[0], (tq, CA_ROLL_W))
    rolled = pltpu.roll(full, 0, 1, stride=1, stride_axis=0)
    q_chunk = lax.broadcasted_iota(jnp.int32, (tq, 1), 0) // CHUNK
    k_chunk = lax.broadcasted_iota(jnp.int32, (1, win), 1) // CHUNK
    valid = jnp.logical_and(k_chunk >= q_chunk, k_chunk <= q_chunk + LEFT_CHUNKS)
    o_ref[0] = jnp.where(valid, rolled[:, :win] * LOG2E, NEG_INF).T


def _bias_tiles(rel_bias):
    n_heads = rel_bias.shape[0]
    t = jnp.arange(CA_ROLL_W)
    d = jnp.where(t < CA_WIN, t, t - CA_ROLL_W)
    idx = jnp.clip(LEFT_CHUNKS * CHUNK - d, -REL_CLIP, REL_CLIP) + REL_CLIP
    y = rel_bias[:, idx].reshape(n_heads, 1, CA_ROLL_W)
    return pl.pallas_call(
        _bias_kernel,
        grid=(n_heads,),
        in_specs=[pl.BlockSpec((1, 1, CA_ROLL_W), lambda h: (h, 0, 0))],
        out_specs=pl.BlockSpec((1, CA_WIN, CA_TQ), lambda h: (h, 0, 0)),
        out_shape=jax.ShapeDtypeStruct((n_heads, CA_WIN, CA_TQ), F32),
        compiler_params=_compiler_params(("arbitrary",)),
        name="bias",
    )(y)


def _proj_kernel(x_ref, mod_ref, tc_ref, ts_ref, gattn_ref, gql_ref, gkvl_ref, gv_ref,
                 win_ref, wq_ref, wkv_ref, wcvt_ref,
                 q_ref, k_ref, v_ref, cq_ref, ck_ref, cvt_ref):
    x = x_ref[0]
    mod = mod_ref[0]
    sh, sc = mod[0:1, :], mod[1:2, :]
    y = x * lax.rsqrt(jnp.mean(x * x, axis=-1, keepdims=True) + EPS) * gattn_ref[...]
    h = (y * (1.0 + sc) + sh).astype(BF16)
    proj = _dot(h, win_ref[...])

    lane = lax.broadcasted_iota(jnp.int32, (1, LANES), 1)
    gv = gv_ref[...]
    g_q, g_kn, g_ka, g_kb, g_cq, g_ck = (gv[r:r + 1, :] for r in range(6))

    tc = tc_ref[0]
    ts = ts_ref[0]
    tq = jnp.where(lane < MLA_QK, tc, ts)

    ql = proj[:, _C_QLAT:_C_QLAT + MLA_Q_RANK]
    qn = (ql * lax.rsqrt(jnp.mean(ql * ql, axis=-1, keepdims=True) + EPS) * gql_ref[...]).astype(BF16)
    qp = _dot(qn, wq_ref[...])
    qmul = g_q * tq
    for hd in range(MLA_HEADS):
        qh = qp[:, hd * LANES:(hd + 1) * LANES]
        ss = jnp.sum(jnp.where(lane < MLA_QK, qh * qh, 0.0), axis=-1, keepdims=True)
        r = lax.rsqrt(ss * (1.0 / MLA_QK) + EPS)
        q_ref[0, :, hd * LANES:(hd + 1) * LANES] = (qh * r * qmul).astype(BF16)

    kvl = proj[:, _C_KVLAT:_C_KVLAT + MLA_KV_RANK]
    kvn = (kvl * lax.rsqrt(jnp.mean(kvl * kvl, axis=-1, keepdims=True) + EPS) * gkvl_ref[...]).astype(BF16)
    kvp = _dot(kvn, wkv_ref[...])
    ka = proj[:, _C_KA:_C_KA + LANES]
    kb = proj[:, _C_KB:_C_KB + LANES]
    krot = ka * g_ka * tc + kb * g_kb * ts
    ss_rope = jnp.sum(jnp.where(lane < MLA_QK, ka * ka, 0.0), axis=-1, keepdims=True)
    for hd in range(MLA_HEADS):
        kh = kvp[:, hd * LANES:(hd + 1) * LANES]
        ss = jnp.sum(kh * kh, axis=-1, keepdims=True) + ss_rope
        r = lax.rsqrt(ss * (1.0 / MLA_QK) + EPS)
        k_ref[0, :, hd * LANES:(hd + 1) * LANES] = ((kh * g_kn + krot) * r).astype(BF16)
    v_ref[0] = kvp[:, MLA_HEADS * LANES:].astype(BF16)

    lo = lane < CA_HEAD_DIM
    for src, gain, dst in ((_C_CQ, g_cq, cq_ref), (_C_CK, g_ck, ck_ref)):
        for p in range(CA_HEADS // 2):
            xx = proj[:, src + p * LANES:src + (p + 1) * LANES]
            x2 = xx * xx
            s_all = jnp.sum(x2, axis=-1, keepdims=True)
            s_lo = jnp.sum(jnp.where(lo, x2, 0.0), axis=-1, keepdims=True)
            r_lo = lax.rsqrt(s_lo * (1.0 / CA_HEAD_DIM) + EPS)
            r_hi = lax.rsqrt((s_all - s_lo) * (1.0 / CA_HEAD_DIM) + EPS)
            dst[0, :, p * LANES:(p + 1) * LANES] = (xx * jnp.where(lo, r_lo, r_hi) * gain).astype(BF16)
    cvt_ref[0] = _dot_nt(wcvt_ref[...], h).astype(BF16)


def _proj(x, mod, tc, ts, g_attn, g_ql, g_kvl, gvec, w_in_ext, w_q_ext, w_kv_ext, w_cvt):
    bsz, seq, _ = x.shape
    tm = PROJ_TM
    tok = lambda b, i: (b, i, 0)
    tok_t = lambda b, i: (b, 0, i)
    const2 = lambda b, i: (0, 0)
    wide = MLA_HEADS * LANES
    narrow = CA_HEADS * CA_HEAD_DIM
    row_major = lambda w: (jax.ShapeDtypeStruct((bsz, seq, w), BF16), pl.BlockSpec((1, tm, w), tok))
    feat_major = lambda w: (jax.ShapeDtypeStruct((bsz, w, seq), BF16), pl.BlockSpec((1, w, tm), tok_t))
    outs = [row_major(wide), row_major(wide), row_major(MLA_HEADS * MLA_V),
            row_major(narrow), row_major(narrow), feat_major(narrow)]
    out_shapes = [o[0] for o in outs]
    out_specs = [o[1] for o in outs]
    return pl.pallas_call(
        _proj_kernel,
        grid=(bsz, seq // tm),
        in_specs=[
            pl.BlockSpec((1, tm, D_MODEL), tok),
            pl.BlockSpec((1, N_MOD, D_MODEL), lambda b, i: (b, 0, 0)),
            pl.BlockSpec((1, tm, LANES), tok),
            pl.BlockSpec((1, tm, LANES), tok),
            pl.BlockSpec((1, D_MODEL), const2),
            pl.BlockSpec((1, MLA_Q_RANK), const2),
            pl.BlockSpec((1, MLA_KV_RANK), const2),
            pl.BlockSpec((8, LANES), const2),
            pl.BlockSpec((D_MODEL, D_IN_EXT), const2),
            pl.BlockSpec((MLA_Q_RANK, wide), const2),
            pl.BlockSpec((MLA_KV_RANK, wide + MLA_HEADS * MLA_V), const2),
            pl.BlockSpec((narrow, D_MODEL), const2),
        ],
        out_specs=out_specs,
        out_shape=out_shapes,
        compiler_params=_compiler_params(("parallel", "parallel")),
        name="proj",
    )(x, mod, tc, ts, g_attn, g_ql, g_kvl, gvec, w_in_ext, w_q_ext, w_kv_ext, w_cvt)


def _softmax_pv_t(score_parts, value_parts):
    def fold(x, op):
        return op(x.reshape(x.shape[0] // REDUCE_ROWS, REDUCE_ROWS, x.shape[1]), axis=0)

    m = fold(score_parts[0], jnp.max)
    for s in score_parts[1:]:
        m = jnp.maximum(m, fold(s, jnp.max))
    m = m.max(axis=0, keepdims=True)
    l = None
    acc = None
    for s, vt in zip(score_parts, value_parts):
        p = jnp.exp2(s - m)
        ps = fold(p, jnp.sum).sum(axis=0, keepdims=True)
        pv = _dot(vt, p.astype(BF16))
        l = ps if l is None else l + ps
        acc = pv if acc is None else acc + pv
    return acc / l


def _attention_pipeline(units, lookahead):
    pending = [scores() for scores, _ in units[:lookahead]]
    for n, (_, finish) in enumerate(units):
        if n + lookahead < len(units):
            pending.append(units[n + lookahead][0]())
        finish(pending.pop(0))


def _pair_units(n_tiles, scores, finish, store):
    units = []
    outs = []

    def make(i, hh):
        def finish_unit(parts):
            outs.append(finish(i, parts))
            if hh == 1:
                store(i, list(outs))
                outs.clear()
        return (lambda: scores(i, hh)), finish_unit

    for i in range(n_tiles):
        for hh in range(2):
            units.append(make(i, hh))
    return units


def _store_pair(o_ref, lo, hi, outs, head_rows):
    row = lax.broadcasted_iota(jnp.int32, (LANES, 1), 0)
    o_t = jnp.where(row < head_rows, outs[0], outs[1])
    o_ref[0, lo:hi, :] = o_t.T.astype(BF16)


def _softmax_pv(score_parts, value_parts):
    m = score_parts[0].max(axis=-1, keepdims=True)
    for s in score_parts[1:]:
        m = jnp.maximum(m, s.max(axis=-1, keepdims=True))
    l = None
    acc = None
    for s, vb in zip(score_parts, value_parts):
        p = jnp.exp2(s - m)
        ps = jnp.sum(p, axis=-1, keepdims=True)
        pv = _dot(p.astype(BF16), vb)
        l = ps if l is None else l + ps
        acc = pv if acc is None else acc + pv
    return acc / l


def _mla_units(q_ref, k_ref, v_ref, o_ref):
    t = MLA_TQ
    seq = q_ref.shape[1]
    lane = lax.broadcasted_iota(jnp.int32, (1, LANES), 1)
    qry_chunk = lax.broadcasted_iota(jnp.int32, (t, 1), 0) // CHUNK
    key_chunk = lax.broadcasted_iota(jnp.int32, (1, t), 1) // CHUNK
    diag_mask = key_chunk <= qry_chunk

    def scores(i, hh):
        lo = i * t
        hs = slice(hh * LANES, (hh + 1) * LANES)
        q = q_ref[0, lo:lo + t, hs]
        parts = [jnp.where(diag_mask, _dot_nt(q, k_ref[0, lo:lo + t, hs]), NEG_INF)]
        if i > 0:
            parts.insert(0, _dot_nt(q, k_ref[0, 0:lo, hs]))
        return parts

    def finish(i, parts):
        lo = i * t
        vals = [v_ref[0, lo:lo + t, :]]
        if i > 0:
            vals.insert(0, v_ref[0, 0:lo, :])
        return _softmax_pv(parts, vals)

    def store(i, outs):
        o_ref[0, i * t:(i + 1) * t, :] = jnp.where(lane < MLA_V, outs[0], outs[1]).astype(BF16)

    return _pair_units(seq // t, scores, finish, store)


def _ca_units(q_ref, k_ref, vt_ref, bm_ref, o_ref):
    tq, win = CA_TQ, CA_WIN
    seq = q_ref.shape[1]
    lane = lax.broadcasted_iota(jnp.int32, (1, LANES), 1)

    def window(i):
        hi = (i + 1) * tq
        k_lo = max(hi - win, 0)
        return k_lo, hi, k_lo - (hi - win)

    def scores(i, hh):
        k_lo, hi, c_lo = window(i)
        q = q_ref[0, i * tq:hi, :]
        head_lanes = (lane < CA_HEAD_DIM) if hh == 0 else (lane >= CA_HEAD_DIM)
        qm = jnp.where(head_lanes, q, jnp.zeros_like(q))
        return [_dot_nt(k_ref[0, k_lo:hi, :], qm) + bm_ref[hh, c_lo:, :]]

    def finish(i, parts):
        k_lo, hi, _ = window(i)
        return _softmax_pv_t(parts, [vt_ref[0, :, k_lo:hi]])

    def store(i, outs):
        _store_pair(o_ref, i * tq, (i + 1) * tq, outs, CA_HEAD_DIM)

    return _pair_units(seq // tq, scores, finish, store)


def _attn_kernel(q_ref, k_ref, v_ref, cq_ref, ck_ref, cvt_ref, bm_ref, om_ref, oc_ref):
    mla = _mla_units(q_ref, k_ref, v_ref, om_ref)
    ca = _ca_units(cq_ref, ck_ref, cvt_ref, bm_ref, oc_ref)
    assert len(mla) == len(ca)
    units = [u for pair in zip(mla, ca) for u in pair]
    _attention_pipeline(units, ATTN_LOOKAHEAD)


def _attn(q, k, v, cq, ck, cvt, bias_mask):
    bsz, seq, _ = q.shape
    pairs = MLA_HEADS // 2
    assert CA_HEADS // 2 == pairs
    tok = lambda b, p: (b, 0, p)
    wide = pl.BlockSpec((1, seq, 2 * LANES), tok)
    narrow = pl.BlockSpec((1, seq, LANES), tok)
    return pl.pallas_call(
        _attn_kernel,
        grid=(bsz, pairs),
        in_specs=[
            wide, wide, narrow,
            narrow, narrow,
            pl.BlockSpec((1, LANES, seq), lambda b, p: (b, p, 0)),
            pl.BlockSpec((2, CA_WIN, CA_TQ), lambda b, p: (p, 0, 0)),
        ],
        out_specs=[narrow, narrow],
        out_shape=[jax.ShapeDtypeStruct((bsz, seq, MLA_HEADS * MLA_V), BF16),
                   jax.ShapeDtypeStruct((bsz, seq, CA_HEADS * CA_HEAD_DIM), BF16)],
        compiler_params=_compiler_params(("parallel", "parallel")),
        name="attn",
    )(q, k, v, cq, ck, cvt, bias_mask)


def _final_kernel(x_ref, xh_ref, om_ref, omh_ref, oc_ref, och_ref, mod_ref, gmlp_ref,
                  wo_ref, wu_ref, cw_ref, cb_ref, wd_ref,
                  out_ref, h2_sc, ug0_sc, ug1_sc, uv0_sc, uv1_sc, act_sc, acc_sc):
    i = pl.program_id(1)
    tm = FIN_TM
    halo = FIN_HALO
    mod = mod_ref[0]
    g_a, sh_m, sc_m, g_m = mod[2:3, :], mod[3:4, :], mod[4:5, :], mod[5:6, :]

    u_bufs = ((ug0_sc, uv0_sc), (ug1_sc, uv1_sc))
    rb = FIN_ROW_BLOCK
    n_rb = tm // rb

    ub = FIN_UP_BLOCK

    def up_rows(r):
        return (0, halo + ub) if r == 0 else (halo + r * ub, halo + (r + 1) * ub)

    x_ext = jnp.concatenate([xh_ref[0], x_ref[0]], axis=0)
    o_ext = jnp.concatenate(
        [jnp.concatenate([omh_ref[0], och_ref[0]], axis=-1), jnp.concatenate([om_ref[0], oc_ref[0]], axis=-1)], axis=0)
    x1 = x_ext + g_a * _dot(o_ext, wo_ref[...])
    out_ref[0] = x1[halo:]
    y = x1 * lax.rsqrt(jnp.mean(x1 * x1, axis=-1, keepdims=True) + EPS) * gmlp_ref[...]
    h2 = y * (1.0 + sc_m) + sh_m
    h2_sc[:halo] = jnp.where(i > 0, h2[:halo], 0.0).astype(BF16)
    h2_sc[halo:] = h2[halo:].astype(BF16)

    def up_project(c, r):
        ug_sc, uv_sc = u_bufs[c % 2]
        lo, hi = up_rows(r)
        h2b = h2_sc[lo:hi, :]
        ug_sc[lo:hi, :] = _dot(h2b, wu_ref[:, c * FF_CHUNK:(c + 1) * FF_CHUNK])
        uv_sc[lo:hi, :] = _dot(h2b, wu_ref[:, D_FF + c * FF_CHUNK:D_FF + (c + 1) * FF_CHUNK])

    def conv(u_sc, col, r):
        cw = cw_ref[:, col:col + FF_CHUNK]
        lo = halo + r * rb
        return (u_sc[lo - 2:lo - 2 + rb, :] * cw[0:1, :] + u_sc[lo - 1:lo - 1 + rb, :] * cw[1:2, :]
                + u_sc[lo:lo + rb, :] * cw[2:3, :] + cb_ref[:, col:col + FF_CHUNK])

    def activate(c, r):
        ug_sc, uv_sc = u_bufs[c % 2]
        gcol, vcol = c * FF_CHUNK, D_FF + c * FF_CHUNK
        gate = conv(ug_sc, gcol, r)
        val = conv(uv_sc, vcol, r)
        act_sc[r * rb:(r + 1) * rb, :] = (gate / (1.0 + jnp.exp(-gate)) * val).astype(BF16)

    def down_project(c):
        down = _dot(act_sc[...], wd_ref[c * FF_CHUNK:(c + 1) * FF_CHUNK, :])
        if c == 0:
            acc_sc[...] = down
        elif c < N_FF_CHUNKS - 1:
            acc_sc[...] += down
        else:
            out_ref[0] += g_m * (acc_sc[...] + down)

    for r in range(tm // ub):
        up_project(0, r)
    for c in range(N_FF_CHUNKS):
        for r in range(n_rb):
            if c + 1 < N_FF_CHUNKS and (r * rb) % ub == 0:
                up_project(c + 1, (r * rb) // ub)
            activate(c, r)
        down_project(c)


def _final(x, o_mla, o_ca, mod, g_mlp, w_out, w_up, conv_w, conv_b, w_down):
    bsz, seq, _ = x.shape
    tm, halo = FIN_TM, FIN_HALO
    tok = lambda b, i: (b, i, 0)
    prev = lambda b, i: (b, jnp.maximum(i * (tm // halo) - 1, 0), 0)
    const2 = lambda b, i: (0, 0)
    d_mix = o_mla.shape[-1]
    resident = dict(pipeline_mode=pl.Buffered(1))
    return pl.pallas_call(
        _final_kernel,
        grid=(bsz, seq // tm),
        in_specs=[
            pl.BlockSpec((1, tm, D_MODEL), tok),
            pl.BlockSpec((1, halo, D_MODEL), prev),
            pl.BlockSpec((1, tm, d_mix), tok),
            pl.BlockSpec((1, halo, d_mix), prev),
            pl.BlockSpec((1, tm, d_mix), tok),
            pl.BlockSpec((1, halo, d_mix), prev),
            pl.BlockSpec((1, N_MOD, D_MODEL), lambda b, i: (b, 0, 0)),
            pl.BlockSpec((1, D_MODEL), const2),
            pl.BlockSpec((2 * d_mix, D_MODEL), const2, **resident),
            pl.BlockSpec((D_MODEL, 2 * D_FF), const2, **resident),
            pl.BlockSpec((3, 2 * D_FF), const2),
            pl.BlockSpec((1, 2 * D_FF), const2),
            pl.BlockSpec((D_FF, D_MODEL), const2, **resident),
        ],
        out_specs=pl.BlockSpec((1, tm, D_MODEL), tok),
        out_shape=jax.ShapeDtypeStruct((bsz, seq, D_MODEL), F32),
        scratch_shapes=[
            pltpu.VMEM((halo + tm, D_MODEL), BF16),
            pltpu.VMEM((halo + tm, FF_CHUNK), F32),
            pltpu.VMEM((halo + tm, FF_CHUNK), F32),
            pltpu.VMEM((halo + tm, FF_CHUNK), F32),
            pltpu.VMEM((halo + tm, FF_CHUNK), F32),
            pltpu.VMEM((tm, FF_CHUNK), BF16),
            pltpu.VMEM((tm, D_MODEL), F32),
        ],
        compiler_params=_compiler_params(("parallel", "arbitrary")),
        name="final",
    )(x, x, o_mla, o_mla, o_ca, o_ca, mod, g_mlp, w_out, w_up, conv_w, conv_b, w_down)


def _prep_layer(w_in, w_q_up, w_kv_up, g_mla_q, g_mla_k, g_ca_q, g_ca_k):
    half = MLA_ROPE // 2
    c0 = MLA_Q_RANK + MLA_KV_RANK
    k1 = w_in[:, c0:c0 + half]
    k2 = w_in[:, c0 + half:c0 + MLA_ROPE]
    z64 = jnp.zeros((D_MODEL, MLA_NOPE), w_in.dtype)
    c_cv = c0 + MLA_ROPE + 2 * CA_HEADS * CA_HEAD_DIM
    w_in_ext = jnp.concatenate(
        [w_in[:, :c0], z64, k1, k2, k1, k2, z64, k2, k1, k2, k1, w_in[:, c0 + MLA_ROPE:c_cv]], axis=1).astype(BF16)
    w_cvt = w_in[:, c_cv:].T.astype(BF16)

    wq = w_q_up.reshape(MLA_Q_RANK, MLA_HEADS, MLA_QK)
    x1 = wq[..., MLA_NOPE:MLA_NOPE + half]
    x2 = wq[..., MLA_NOPE + half:]
    w_q_ext = jnp.concatenate([wq, x2, x1], axis=-1).reshape(MLA_Q_RANK, MLA_HEADS * LANES).astype(BF16)

    wkv = w_kv_up.reshape(MLA_KV_RANK, MLA_HEADS, MLA_NOPE + MLA_V)
    wk = jnp.concatenate([wkv[..., :MLA_NOPE], jnp.zeros((MLA_KV_RANK, MLA_HEADS, LANES - MLA_NOPE), wkv.dtype)], axis=-1)
    w_kv_ext = jnp.concatenate(
        [wk.reshape(MLA_KV_RANK, MLA_HEADS * LANES), wkv[..., MLA_NOPE:].reshape(MLA_KV_RANK, MLA_HEADS * MLA_V)],
        axis=1).astype(BF16)

    gq1, gq2 = g_mla_q[MLA_NOPE:MLA_NOPE + half], g_mla_q[MLA_NOPE + half:]
    gk1, gk2 = g_mla_k[MLA_NOPE:MLA_NOPE + half], g_mla_k[MLA_NOPE + half:]
    z = jnp.zeros((MLA_NOPE,), F32)
    rows = [
        jnp.concatenate([g_mla_q, gq2, gq1]) * (MLA_QK ** -0.5 * LOG2E),
        jnp.concatenate([g_mla_k[:MLA_NOPE], z]),
        jnp.concatenate([z, gk1, gk2, gk1, gk2]),
        jnp.concatenate([z, gk2, gk1, gk2, gk1]),
        jnp.concatenate([g_ca_q, g_ca_q]) * (CA_HEAD_DIM ** -0.5 * LOG2E),
        jnp.concatenate([g_ca_k, g_ca_k]),
        jnp.zeros((LANES,), F32),
        jnp.zeros((LANES,), F32),
    ]
    gvec = jnp.stack(rows)
    return w_in_ext, w_q_ext, w_kv_ext, w_cvt, gvec


def kernel(x, c, positions, w_ada, b_ada, g_attn_norm, w_in, g_q_latent, g_kv_latent, w_q_up, w_kv_up, g_mla_q, g_mla_k, g_ca_q, g_ca_k, rel_bias, w_out, g_mlp_norm, w_up, conv_w, conv_b, w_down):
    bsz, seq, _ = x.shape
    depth = w_ada.shape[0]

    tc, ts = _rope_tables(positions)

    for l in range(depth):
        w_in_ext, w_q_ext, w_kv_ext, w_cvt, gvec = _prep_layer(
            w_in[l], w_q_up[l], w_kv_up[l], g_mla_q[l], g_mla_k[l], g_ca_q[l], g_ca_k[l])
        mod = _ada(c, w_ada[l], b_ada[l]).reshape(bsz, N_MOD, D_MODEL)
        bias_mask = _bias_tiles(rel_bias[l])
        q, k, v, cq, ck, cvt = _proj(
            x, mod, tc, ts, g_attn_norm[l].reshape(1, -1), g_q_latent[l].reshape(1, -1),
            g_kv_latent[l].reshape(1, -1), gvec, w_in_ext, w_q_ext, w_kv_ext, w_cvt)
        o_mla, o_ca = _attn(q, k, v, cq, ck, cvt, bias_mask)
        x = _final(x, o_mla, o_ca, mod, g_mlp_norm[l].reshape(1, -1), w_out[l].astype(BF16),
                   w_up[l].astype(BF16), conv_w[l], conv_b[l].reshape(1, -1), w_down[l].astype(BF16))
    return x
```

```python
import math

import numpy as np
import jax
import jax.numpy as jnp
from jax import lax
from jax.experimental import pallas as pl
from jax.experimental.pallas import tpu as pltpu

D_MODEL = 1024
CHUNK = 64
LEFT_CHUNKS = 8
MLA_HEADS = 8
MLA_Q_RANK = 256
MLA_KV_RANK = 128
MLA_NOPE = 64
MLA_ROPE = 32
MLA_QK = MLA_NOPE + MLA_ROPE
MLA_V = 64
ROPE_THETA = 10000.0
CA_HEADS = 8
CA_HEAD_DIM = 64
REL_CLIP = 128
D_FF = 2816
N_MOD = 6
EPS = 1e-6
NEG_INF = -1e30
LOG2E = math.log2(math.e)

LANES = 128
BF16_SUBLANES = 16
VMEM_LIMIT_BYTES = 56 * 1024 * 1024

PROJ_TM = 1024
MLA_TQ = 256
CA_TQ = 256
CA_WIN = CA_TQ + LEFT_CHUNKS * CHUNK
CA_ROLL_W = 1024
REDUCE_ROWS = 64
ATTN_LOOKAHEAD = 6
FIN_TM = 512
FIN_HALO = BF16_SUBLANES
FIN_ROW_BLOCK = 128
FF_CHUNK = 256
N_FF_CHUNKS = D_FF // FF_CHUNK

_C_QLAT = 0
_C_KVLAT = _C_QLAT + MLA_Q_RANK
_C_KA = _C_KVLAT + MLA_KV_RANK
_C_KB = _C_KA + LANES
_C_CQ = _C_KB + LANES
_C_CK = _C_CQ + CA_HEADS * CA_HEAD_DIM
D_IN_EXT = _C_CK + CA_HEADS * CA_HEAD_DIM

F32 = jnp.float32
BF16 = jnp.bfloat16


def _dot(a, b):
    return jnp.dot(a, b, preferred_element_type=F32)


def _dot_nt(a, b):
    return lax.dot_general(a, b, (((1,), (1,)), ((), ())), preferred_element_type=F32)


def _compiler_params(semantics):
    return pltpu.CompilerParams(dimension_semantics=semantics, vmem_limit_bytes=VMEM_LIMIT_BYTES)


def _ada_kernel(c_ref, w_ref, b_ref, o_ref):
    c = c_ref[...]
    s = c / (1.0 + jnp.exp(-c))
    o_ref[...] = _dot(s.astype(BF16), w_ref[...].astype(BF16)) + b_ref[...]


def _ada(c, w_ada, b_ada):
    bsz = c.shape[0]
    n_out = w_ada.shape[1]
    tn = D_MODEL
    return pl.pallas_call(
        _ada_kernel,
        grid=(n_out // tn,),
        in_specs=[
            pl.BlockSpec((bsz, D_MODEL), lambda j: (0, 0)),
            pl.BlockSpec((D_MODEL, tn), lambda j: (0, j)),
            pl.BlockSpec((1, tn), lambda j: (0, j)),
        ],
        out_specs=pl.BlockSpec((bsz, tn), lambda j: (0, j)),
        out_shape=jax.ShapeDtypeStruct((bsz, n_out), F32),
        compiler_params=_compiler_params(("arbitrary",)),
        name="ada",
    )(c, w_ada, b_ada.reshape(1, n_out))


def _expand(d, e_ref):
    hi = d.astype(BF16)
    r1 = d - hi.astype(F32)
    mid = r1.astype(BF16)
    lo = (r1 - mid.astype(F32)).astype(BF16)
    e = e_ref[...]
    return _dot(hi, e) + _dot(mid, e) + _dot(lo, e)


def _rope_kernel(pos_ref, rep_ref, inv_ref, ec_ref, es_ref, base_ref, tc_ref, ts_ref):
    rows = pos_ref.shape[0]
    tok_per_row = ec_ref.shape[1] // LANES
    pos = _expand(pos_ref[...].astype(F32), rep_ref)
    ang = pos * inv_ref[...]
    tc_wide = _expand(jnp.cos(ang), ec_ref) + base_ref[...]
    ts_wide = _expand(jnp.sin(ang), es_ref)
    for t in range(tok_per_row):
        tc_ref[pl.ds(t, rows, stride=tok_per_row), :] = tc_wide[:, t * LANES:(t + 1) * LANES]
        ts_ref[pl.ds(t, rows, stride=tok_per_row), :] = ts_wide[:, t * LANES:(t + 1) * LANES]


def _rope_tables(positions):
    half = MLA_ROPE // 2
    bsz, seq = positions.shape
    n_tok = bsz * seq
    tok_per_row = LANES // half
    rows = n_tok // tok_per_row
    inv = jnp.power(ROPE_THETA, -jnp.arange(half, dtype=F32) / half)
    inv_t = jnp.tile(inv, tok_per_row).reshape(1, LANES)
    spread = np.repeat(np.eye(tok_per_row, dtype=np.float32), half, axis=1)

    src = np.arange(LANES)
    sel_c = np.zeros((LANES, tok_per_row * LANES), np.float32)
    sel_s = np.zeros((LANES, tok_per_row * LANES), np.float32)
    for rep, sign in enumerate((-1.0, 1.0, -1.0, 1.0)):
        dst = (src // half) * LANES + MLA_NOPE + rep * half + src % half
        sel_c[src, dst] = 1.0
        sel_s[src, dst] = sign
    base = np.tile((np.arange(LANES) < MLA_NOPE).astype(np.float32), tok_per_row).reshape(1, -1)

    tr = 512
    wide = tok_per_row * LANES
    const = lambda i: (0, 0)
    tc, ts = pl.pallas_call(
        _rope_kernel,
        grid=(rows // tr,),
        in_specs=[
            pl.BlockSpec((tr, tok_per_row), lambda i: (i, 0)),
            pl.BlockSpec((tok_per_row, LANES), const),
            pl.BlockSpec((1, LANES), const),
            pl.BlockSpec((LANES, wide), const),
            pl.BlockSpec((LANES, wide), const),
            pl.BlockSpec((1, wide), const),
        ],
        out_specs=[pl.BlockSpec((tr * tok_per_row, LANES), lambda i: (i, 0))] * 2,
        out_shape=[jax.ShapeDtypeStruct((n_tok, LANES), F32)] * 2,
        compiler_params=_compiler_params(("arbitrary",)),
        name="rope",
    )(positions.reshape(rows, tok_per_row), jnp.asarray(spread, BF16), inv_t,
      jnp.asarray(sel_c, BF16), jnp.asarray(sel_s, BF16), jnp.asarray(base))
    return tc.reshape(bsz, seq, LANES), ts.reshape(bsz, seq, LANES)


def _bias_kernel(y_ref, o_ref):
    tq, win = CA_TQ, CA_WIN
    full = jnp.broadcast_to(y_ref[0], (tq, CA_ROLL_W))
    rolled = pltpu.roll(full, 0, 1, stride=1, stride_axis=0)
    q_chunk = lax.broadcasted_iota(jnp.int32, (tq, 1), 0) // CHUNK
    k_chunk = lax.broadcasted_iota(jnp.int32, (1, win), 1) // CHUNK
    valid = jnp.logical_and(k_chunk >= q_chunk, k_chunk <= q_chunk + LEFT_CHUNKS)
    o_ref[0] = jnp.where(valid, rolled[:, :win] * LOG2E, NEG_INF).T


def _bias_tiles(rel_bias):
    n_heads = rel_bias.shape[0]
    t = jnp.arange(CA_ROLL_W)
    d = jnp.where(t < CA_WIN, t, t - CA_ROLL_W)
    idx = jnp.clip(LEFT_CHUNKS * CHUNK - d, -REL_CLIP, REL_CLIP) + REL_CLIP
    y = rel_bias[:, idx].reshape(n_heads, 1, CA_ROLL_W)
    return pl.pallas_call(
        _bias_kernel,
        grid=(n_heads,),
        in_specs=[pl.BlockSpec((1, 1, CA_ROLL_W), lambda h: (h, 0, 0))],
        out_specs=pl.BlockSpec((1, CA_WIN, CA_TQ), lambda h: (h, 0, 0)),
        out_shape=jax.ShapeDtypeStruct((n_heads, CA_WIN, CA_TQ), F32),
        compiler_params=_compiler_params(("arbitrary",)),
        name="bias",
    )(y)


def _proj_kernel(x_ref, mod_ref, tc_ref, ts_ref, gattn_ref, gql_ref, gkvl_ref, gv_ref,
                 win_ref, wq_ref, wkv_ref, wcvt_ref,
                 q_ref, k_ref, v_ref, cq_ref, ck_ref, cvt_ref):
    x = x_ref[0]
    mod = mod_ref[0]
    sh, sc = mod[0:1, :], mod[1:2, :]
    y = x * lax.rsqrt(jnp.mean(x * x, axis=-1, keepdims=True) + EPS) * gattn_ref[...]
    h = (y * (1.0 + sc) + sh).astype(BF16)
    proj = _dot(h, win_ref[...])

    lane = lax.broadcasted_iota(jnp.int32, (1, LANES), 1)
    gv = gv_ref[...]
    g_q, g_kn, g_ka, g_kb, g_cq, g_ck = (gv[r:r + 1, :] for r in range(6))

    tc = tc_ref[0]
    ts = ts_ref[0]
    tq = jnp.where(lane < MLA_QK, tc, ts)

    ql = proj[:, _C_QLAT:_C_QLAT + MLA_Q_RANK]
    qn = (ql * lax.rsqrt(jnp.mean(ql * ql, axis=-1, keepdims=True) + EPS) * gql_ref[...]).astype(BF16)
    qp = _dot(qn, wq_ref[...])
    qmul = g_q * tq
    for hd in range(MLA_HEADS):
        qh = qp[:, hd * LANES:(hd + 1) * LANES]
        ss = jnp.sum(jnp.where(lane < MLA_QK, qh * qh, 0.0), axis=-1, keepdims=True)
        r = lax.rsqrt(ss * (1.0 / MLA_QK) + EPS)
        q_ref[0, :, hd * LANES:(hd + 1) * LANES] = (qh * r * qmul).astype(BF16)

    kvl = proj[:, _C_KVLAT:_C_KVLAT + MLA_KV_RANK]
    kvn = (kvl * lax.rsqrt(jnp.mean(kvl * kvl, axis=-1, keepdims=True) + EPS) * gkvl_ref[...]).astype(BF16)
    kvp = _dot(kvn, wkv_ref[...])
    ka = proj[:, _C_KA:_C_KA + LANES]
    kb = proj[:, _C_KB:_C_KB + LANES]
    krot = ka * g_ka * tc + kb * g_kb * ts
    ss_rope = jnp.sum(jnp.where(lane < MLA_QK, ka * ka, 0.0), axis=-1, keepdims=True)
    for hd in range(MLA_HEADS):
        kh = kvp[:, hd * LANES:(hd + 1) * LANES]
        ss = jnp.sum(kh * kh, axis=-1, keepdims=True) + ss_rope
        r = lax.rsqrt(ss * (1.0 / MLA_QK) + EPS)
        k_ref[0, :, hd * LANES:(hd + 1) * LANES] = ((kh * g_kn + krot) * r).astype(BF16)
    v_ref[0] = kvp[:, MLA_HEADS * LANES:].astype(BF16)

    lo = lane < CA_HEAD_DIM
    for src, gain, dst in ((_C_CQ, g_cq, cq_ref), (_C_CK, g_ck, ck_ref)):
        for p in range(CA_HEADS // 2):
            xx = proj[:, src + p * LANES:src + (p + 1) * LANES]
            x2 = xx * xx
            s_all = jnp.sum(x2, axis=-1, keepdims=True)
            s_lo = jnp.sum(jnp.where(lo, x2, 0.0), axis=-1, keepdims=True)
            r_lo = lax.rsqrt(s_lo * (1.0 / CA_HEAD_DIM) + EPS)
            r_hi = lax.rsqrt((s_all - s_lo) * (1.0 / CA_HEAD_DIM) + EPS)
            dst[0, :, p * LANES:(p + 1) * LANES] = (xx * jnp.where(lo, r_lo, r_hi) * gain).astype(BF16)
    cvt_ref[0] = _dot_nt(wcvt_ref[...], h).astype(BF16)


def _proj(x, mod, tc, ts, g_attn, g_ql, g_kvl, gvec, w_in_ext, w_q_ext, w_kv_ext, w_cvt):
    bsz, seq, _ = x.shape
    tm = PROJ_TM
    tok = lambda b, i: (b, i, 0)
    tok_t = lambda b, i: (b, 0, i)
    const2 = lambda b, i: (0, 0)
    wide = MLA_HEADS * LANES
    narrow = CA_HEADS * CA_HEAD_DIM
    row_major = lambda w: (jax.ShapeDtypeStruct((bsz, seq, w), BF16), pl.BlockSpec((1, tm, w), tok))
    feat_major = lambda w: (jax.ShapeDtypeStruct((bsz, w, seq), BF16), pl.BlockSpec((1, w, tm), tok_t))
    outs = [row_major(wide), row_major(wide), row_major(MLA_HEADS * MLA_V),
            row_major(narrow), row_major(narrow), feat_major(narrow)]
    out_shapes = [o[0] for o in outs]
    out_specs = [o[1] for o in outs]
    return pl.pallas_call(
        _proj_kernel,
        grid=(bsz, seq // tm),
        in_specs=[
            pl.BlockSpec((1, tm, D_MODEL), tok),
            pl.BlockSpec((1, N_MOD, D_MODEL), lambda b, i: (b, 0, 0)),
            pl.BlockSpec((1, tm, LANES), tok),
            pl.BlockSpec((1, tm, LANES), tok),
            pl.BlockSpec((1, D_MODEL), const2),
            pl.BlockSpec((1, MLA_Q_RANK), const2),
            pl.BlockSpec((1, MLA_KV_RANK), const2),
            pl.BlockSpec((8, LANES), const2),
            pl.BlockSpec((D_MODEL, D_IN_EXT), const2),
            pl.BlockSpec((MLA_Q_RANK, wide), const2),
            pl.BlockSpec((MLA_KV_RANK, wide + MLA_HEADS * MLA_V), const2),
            pl.BlockSpec((narrow, D_MODEL), const2),
        ],
        out_specs=out_specs,
        out_shape=out_shapes,
        compiler_params=_compiler_params(("parallel", "parallel")),
        name="proj",
    )(x, mod, tc, ts, g_attn, g_ql, g_kvl, gvec, w_in_ext, w_q_ext, w_kv_ext, w_cvt)


def _softmax_pv_t(score_parts, value_parts):
    def fold(x, op):
        return op(x.reshape(x.shape[0] // REDUCE_ROWS, REDUCE_ROWS, x.shape[1]), axis=0)

    m = fold(score_parts[0], jnp.max)
    for s in score_parts[1:]:
        m = jnp.maximum(m, fold(s, jnp.max))
    m = m.max(axis=0, keepdims=True)
    l = None
    acc = None
    for s, vt in zip(score_parts, value_parts):
        p = jnp.exp2(s - m)
        ps = fold(p, jnp.sum).sum(axis=0, keepdims=True)
        pv = _dot(vt, p.astype(BF16))
        l = ps if l is None else l + ps
        acc = pv if acc is None else acc + pv
    return acc / l


def _attention_pipeline(units, lookahead):
    pending = [scores() for scores, _ in units[:lookahead]]
    for n, (_, finish) in enumerate(units):
        if n + lookahead < len(units):
            pending.append(units[n + lookahead][0]())
        finish(pending.pop(0))


def _pair_units(n_tiles, scores, finish, store):
    units = []
    outs = []

    def make(i, hh):
        def finish_unit(parts):
            outs.append(finish(i, parts))
            if hh == 1:
                store(i, list(outs))
                outs.clear()
        return (lambda: scores(i, hh)), finish_unit

    for i in range(n_tiles):
        for hh in range(2):
            units.append(make(i, hh))
    return units


def _store_pair(o_ref, lo, hi, outs, head_rows):
    row = lax.broadcasted_iota(jnp.int32, (LANES, 1), 0)
    o_t = jnp.where(row < head_rows, outs[0], outs[1])
    o_ref[0, lo:hi, :] = o_t.T.astype(BF16)


def _softmax_pv(score_parts, value_parts):
    m = score_parts[0].max(axis=-1, keepdims=True)
    for s in score_parts[1:]:
        m = jnp.maximum(m, s.max(axis=-1, keepdims=True))
    l = None
    acc = None
    for s, vb in zip(score_parts, value_parts):
        p = jnp.exp2(s - m)
        ps = jnp.sum(p, axis=-1, keepdims=True)
        pv = _dot(p.astype(BF16), vb)
        l = ps if l is None else l + ps
        acc = pv if acc is None else acc + pv
    return acc / l


def _mla_units(q_ref, k_ref, v_ref, o_ref):
    t = MLA_TQ
    seq = q_ref.shape[1]
    lane = lax.broadcasted_iota(jnp.int32, (1, LANES), 1)
    qry_chunk = lax.broadcasted_iota(jnp.int32, (t, 1), 0) // CHUNK
    key_chunk = lax.broadcasted_iota(jnp.int32, (1, t), 1) // CHUNK
    diag_mask = key_chunk <= qry_chunk

    def scores(i, hh):
        lo = 2 * i * t
        hs = slice(hh * LANES, (hh + 1) * LANES)
        q_a = q_ref[0, lo:lo + t, hs]
        q_b = q_ref[0, lo + t:lo + 2 * t, hs]
        k_a = k_ref[0, lo:lo + t, hs]
        k_b = k_ref[0, lo + t:lo + 2 * t, hs]
        parts_a = [jnp.where(diag_mask, _dot_nt(q_a, k_a), NEG_INF)]
        parts_b = [_dot_nt(q_b, k_a), jnp.where(diag_mask, _dot_nt(q_b, k_b), NEG_INF)]
        if i > 0:
            main = _dot_nt(q_ref[0, lo:lo + 2 * t, hs], k_ref[0, 0:lo, hs])
            parts_a.insert(0, main[:t])
            parts_b.insert(0, main[t:])
        return parts_a, parts_b

    def finish(i, parts):
        lo = 2 * i * t
        parts_a, parts_b = parts
        vals_a = [v_ref[0, lo:lo + t, :]]
        vals_b = [v_ref[0, lo:lo + t, :], v_ref[0, lo + t:lo + 2 * t, :]]
        if i > 0:
            vals_a.insert(0, v_ref[0, 0:lo, :])
            vals_b.insert(0, v_ref[0, 0:lo, :])
        return _softmax_pv(parts_a, vals_a), _softmax_pv(parts_b, vals_b)

    def store(i, outs):
        lo = 2 * i * t
        for half in range(2):
            o_ref[0, lo + half * t:lo + (half + 1) * t, :] = jnp.where(
                lane < MLA_V, outs[0][half], outs[1][half]).astype(BF16)

    return _pair_units(seq // (2 * t), scores, finish, store)


def _ca_units(q_ref, k_ref, vt_ref, bm_ref, o_ref):
    tq, win = CA_TQ, CA_WIN
    seq = q_ref.shape[1]
    lane = lax.broadcasted_iota(jnp.int32, (1, LANES), 1)

    def window(i):
        hi = (i + 1) * tq
        k_lo = max(hi - win, 0)
        return k_lo, hi, k_lo - (hi - win)

    def scores(i, hh):
        k_lo, hi, c_lo = window(i)
        q = q_ref[0, i * tq:hi, :]
        head_lanes = (lane < CA_HEAD_DIM) if hh == 0 else (lane >= CA_HEAD_DIM)
        qm = jnp.where(head_lanes, q, jnp.zeros_like(q))
        return [_dot_nt(k_ref[0, k_lo:hi, :], qm) + bm_ref[hh, c_lo:, :]]

    def finish(i, parts):
        k_lo, hi, _ = window(i)
        return _softmax_pv_t(parts, [vt_ref[0, :, k_lo:hi]])

    def store(i, outs):
        _store_pair(o_ref, i * tq, (i + 1) * tq, outs, CA_HEAD_DIM)

    return _pair_units(seq // tq, scores, finish, store)


def _attn_kernel(q_ref, k_ref, v_ref, cq_ref, ck_ref, cvt_ref, bm_ref, om_ref, oc_ref):
    mla = _mla_units(q_ref, k_ref, v_ref, om_ref)
    ca = _ca_units(cq_ref, ck_ref, cvt_ref, bm_ref, oc_ref)
    assert 2 * len(mla) == len(ca)
    units = []
    for n, u in enumerate(mla):
        units += [u, ca[2 * n], ca[2 * n + 1]]
    _attention_pipeline(units, ATTN_LOOKAHEAD)


def _attn(q, k, v, cq, ck, cvt, bias_mask):
    bsz, seq, _ = q.shape
    pairs = MLA_HEADS // 2
    assert CA_HEADS // 2 == pairs
    tok = lambda b, p: (b, 0, p)
    wide = pl.BlockSpec((1, seq, 2 * LANES), tok)
    narrow = pl.BlockSpec((1, seq, LANES), tok)
    return pl.pallas_call(
        _attn_kernel,
        grid=(bsz, pairs),
        in_specs=[
            wide, wide, narrow,
            narrow, narrow,
            pl.BlockSpec((1, LANES, seq), lambda b, p: (b, p, 0)),
            pl.BlockSpec((2, CA_WIN, CA_TQ), lambda b, p: (p, 0, 0)),
        ],
        out_specs=[narrow, narrow],
        out_shape=[jax.ShapeDtypeStruct((bsz, seq, MLA_HEADS * MLA_V), BF16),
                   jax.ShapeDtypeStruct((bsz, seq, CA_HEADS * CA_HEAD_DIM), BF16)],
        compiler_params=_compiler_params(("parallel", "parallel")),
        name="attn",
    )(q, k, v, cq, ck, cvt, bias_mask)


def _final_kernel(x_ref, xh_ref, om_ref, omh_ref, oc_ref, och_ref, mod_ref, gmlp_ref,
                  wo_ref, wu_ref, cw_ref, cb_ref, wd_ref,
                  out_ref, h2_sc, ug0_sc, ug1_sc, uv0_sc, uv1_sc, act_sc, acc_sc):
    i = pl.program_id(1)
    tm = FIN_TM
    halo = FIN_HALO
    mod = mod_ref[0]
    g_a, sh_m, sc_m, g_m = mod[2:3, :], mod[3:4, :], mod[4:5, :], mod[5:6, :]

    u_bufs = ((ug0_sc, uv0_sc), (ug1_sc, uv1_sc))
    rb = FIN_ROW_BLOCK
    n_rb = tm // rb

    def up_rows(r):
        return (0, halo + rb) if r == 0 else (halo + r * rb, halo + (r + 1) * rb)

    x_ext = jnp.concatenate([xh_ref[0], x_ref[0]], axis=0)
    o_ext = jnp.concatenate(
        [jnp.concatenate([omh_ref[0], och_ref[0]], axis=-1), jnp.concatenate([om_ref[0], oc_ref[0]], axis=-1)], axis=0)
    x1 = x_ext + g_a * _dot(o_ext, wo_ref[...])
    out_ref[0] = x1[halo:]
    y = x1 * lax.rsqrt(jnp.mean(x1 * x1, axis=-1, keepdims=True) + EPS) * gmlp_ref[...]
    h2 = y * (1.0 + sc_m) + sh_m
    h2_sc[:halo] = jnp.where(i > 0, h2[:halo], 0.0).astype(BF16)
    h2_sc[halo:] = h2[halo:].astype(BF16)

    def up_project(c, r):
        ug_sc, uv_sc = u_bufs[c % 2]
        lo, hi = up_rows(r)
        h2b = h2_sc[lo:hi, :]
        ug_sc[lo:hi, :] = _dot(h2b, wu_ref[:, c * FF_CHUNK:(c + 1) * FF_CHUNK])
        uv_sc[lo:hi, :] = _dot(h2b, wu_ref[:, D_FF + c * FF_CHUNK:D_FF + (c + 1) * FF_CHUNK])

    def conv(u_sc, col, r):
        cw = cw_ref[:, col:col + FF_CHUNK]
        lo = halo + r * rb
        return (u_sc[lo - 2:lo - 2 + rb, :] * cw[0:1, :] + u_sc[lo - 1:lo - 1 + rb, :] * cw[1:2, :]
                + u_sc[lo:lo + rb, :] * cw[2:3, :] + cb_ref[:, col:col + FF_CHUNK])

    def activate(c, r):
        ug_sc, uv_sc = u_bufs[c % 2]
        gcol, vcol = c * FF_CHUNK, D_FF + c * FF_CHUNK
        gate = conv(ug_sc, gcol, r)
        val = conv(uv_sc, vcol, r)
        act_sc[r * rb:(r + 1) * rb, :] = (gate / (1.0 + jnp.exp(-gate)) * val).astype(BF16)

    def down_project(c):
        down = _dot(act_sc[...], wd_ref[c * FF_CHUNK:(c + 1) * FF_CHUNK, :])
        if c == 0:
            acc_sc[...] = down
        elif c < N_FF_CHUNKS - 1:
            acc_sc[...] += down
        else:
            out_ref[0] += g_m * (acc_sc[...] + down)

    for r in range(n_rb):
        up_project(0, r)
    for c in range(N_FF_CHUNKS):
        for r in range(n_rb):
            if c + 1 < N_FF_CHUNKS:
                up_project(c + 1, r)
            activate(c, r)
        down_project(c)


def _final(x, o_mla, o_ca, mod, g_mlp, w_out, w_up, conv_w, conv_b, w_down):
    bsz, seq, _ = x.shape
    tm, halo = FIN_TM, FIN_HALO
    tok = lambda b, i: (b, i, 0)
    prev = lambda b, i: (b, jnp.maximum(i * (tm // halo) - 1, 0), 0)
    const2 = lambda b, i: (0, 0)
    d_mix = o_mla.shape[-1]
    resident = dict(pipeline_mode=pl.Buffered(1))
    return pl.pallas_call(
        _final_kernel,
        grid=(bsz, seq // tm),
        in_specs=[
            pl.BlockSpec((1, tm, D_MODEL), tok),
            pl.BlockSpec((1, halo, D_MODEL), prev),
            pl.BlockSpec((1, tm, d_mix), tok),
            pl.BlockSpec((1, halo, d_mix), prev),
            pl.BlockSpec((1, tm, d_mix), tok),
            pl.BlockSpec((1, halo, d_mix), prev),
            pl.BlockSpec((1, N_MOD, D_MODEL), lambda b, i: (b, 0, 0)),
            pl.BlockSpec((1, D_MODEL), const2),
            pl.BlockSpec((2 * d_mix, D_MODEL), const2, **resident),
            pl.BlockSpec((D_MODEL, 2 * D_FF), const2, **resident),
            pl.BlockSpec((3, 2 * D_FF), const2),
            pl.BlockSpec((1, 2 * D_FF), const2),
            pl.BlockSpec((D_FF, D_MODEL), const2, **resident),
        ],
        out_specs=pl.BlockSpec((1, tm, D_MODEL), tok),
        out_shape=jax.ShapeDtypeStruct((bsz, seq, D_MODEL), F32),
        scratch_shapes=[
            pltpu.VMEM((halo + tm, D_MODEL), BF16),
            pltpu.VMEM((halo + tm, FF_CHUNK), F32),
            pltpu.VMEM((halo + tm, FF_CHUNK), F32),
            pltpu.VMEM((halo + tm, FF_CHUNK), F32),
            pltpu.VMEM((halo + tm, FF_CHUNK), F32),
            pltpu.VMEM((tm, FF_CHUNK), BF16),
            pltpu.VMEM((tm, D_MODEL), F32),
        ],
        compiler_params=_compiler_params(("parallel", "arbitrary")),
        name="final",
    )(x, x, o_mla, o_mla, o_ca, o_ca, mod, g_mlp, w_out, w_up, conv_w, conv_b, w_down)


def _prep_layer(w_in, w_q_up, w_kv_up, g_mla_q, g_mla_k, g_ca_q, g_ca_k):
    half = MLA_ROPE // 2
    c0 = MLA_Q_RANK + MLA_KV_RANK
    k1 = w_in[:, c0:c0 + half]
    k2 = w_in[:, c0 + half:c0 + MLA_ROPE]
    z64 = jnp.zeros((D_MODEL, MLA_NOPE), w_in.dtype)
    c_cv = c0 + MLA_ROPE + 2 * CA_HEADS * CA_HEAD_DIM
    w_in_ext = jnp.concatenate(
        [w_in[:, :c0], z64, k1, k2, k1, k2, z64, k2, k1, k2, k1, w_in[:, c0 + MLA_ROPE:c_cv]], axis=1).astype(BF16)
    w_cvt = w_in[:, c_cv:].T.astype(BF16)

    wq = w_q_up.reshape(MLA_Q_RANK, MLA_HEADS, MLA_QK)
    x1 = wq[..., MLA_NOPE:MLA_NOPE + half]
    x2 = wq[..., MLA_NOPE + half:]
    w_q_ext = jnp.concatenate([wq, x2, x1], axis=-1).reshape(MLA_Q_RANK, MLA_HEADS * LANES).astype(BF16)

    wkv = w_kv_up.reshape(MLA_KV_RANK, MLA_HEADS, MLA_NOPE + MLA_V)
    wk = jnp.concatenate([wkv[..., :MLA_NOPE], jnp.zeros((MLA_KV_RANK, MLA_HEADS, LANES - MLA_NOPE), wkv.dtype)], axis=-1)
    w_kv_ext = jnp.concatenate(
        [wk.reshape(MLA_KV_RANK, MLA_HEADS * LANES), wkv[..., MLA_NOPE:].reshape(MLA_KV_RANK, MLA_HEADS * MLA_V)],
        axis=1).astype(BF16)

    gq1, gq2 = g_mla_q[MLA_NOPE:MLA_NOPE + half], g_mla_q[MLA_NOPE + half:]
    gk1, gk2 = g_mla_k[MLA_NOPE:MLA_NOPE + half], g_mla_k[MLA_NOPE + half:]
    z = jnp.zeros((MLA_NOPE,), F32)
    rows = [
        jnp.concatenate([g_mla_q, gq2, gq1]) * (MLA_QK ** -0.5 * LOG2E),
        jnp.concatenate([g_mla_k[:MLA_NOPE], z]),
        jnp.concatenate([z, gk1, gk2, gk1, gk2]),
        jnp.concatenate([z, gk2, gk1, gk2, gk1]),
        jnp.concatenate([g_ca_q, g_ca_q]) * (CA_HEAD_DIM ** -0.5 * LOG2E),
        jnp.concatenate([g_ca_k, g_ca_k]),
        jnp.zeros((LANES,), F32),
        jnp.zeros((LANES,), F32),
    ]
    gvec = jnp.stack(rows)
    return w_in_ext, w_q_ext, w_kv_ext, w_cvt, gvec


def kernel(x, c, positions, w_ada, b_ada, g_attn_norm, w_in, g_q_latent, g_kv_latent, w_q_up, w_kv_up, g_mla_q, g_mla_k, g_ca_q, g_ca_k, rel_bias, w_out, g_mlp_norm, w_up, conv_w, conv_b, w_down):
    bsz, seq, _ = x.shape
    depth = w_ada.shape[0]

    tc, ts = _rope_tables(positions)

    for l in range(depth):
        w_in_ext, w_q_ext, w_kv_ext, w_cvt, gvec = _prep_layer(
            w_in[l], w_q_up[l], w_kv_up[l], g_mla_q[l], g_mla_k[l], g_ca_q[l], g_ca_k[l])
        mod = _ada(c, w_ada[l], b_ada[l]).reshape(bsz, N_MOD, D_MODEL)
        bias_mask = _bias_tiles(rel_bias[l])
        q, k, v, cq, ck, cvt = _proj(
            x, mod, tc, ts, g_attn_norm[l].reshape(1, -1), g_q_latent[l].reshape(1, -1),
            g_kv_latent[l].reshape(1, -1), gvec, w_in_ext, w_q_ext, w_kv_ext, w_cvt)
        o_mla, o_ca = _attn(q, k, v, cq, ck, cvt, bias_mask)
        x = _final(x, o_mla, o_ca, mod, g_mlp_norm[l].reshape(1, -1), w_out[l].astype(BF16),
                   w_up[l].astype(BF16), conv_w[l], conv_b[l].reshape(1, -1), w_down[l].astype(BF16))
    return x
```

```python
import math

import numpy as np
import jax
import jax.numpy as jnp
from jax import lax
from jax.experimental import pallas as pl
from jax.experimental.pallas import tpu as pltpu

D_MODEL = 1024
CHUNK = 64
LEFT_CHUNKS = 8
MLA_HEADS = 8
MLA_Q_RANK = 256
MLA_KV_RANK = 128
MLA_NOPE = 64
MLA_ROPE = 32
MLA_QK = MLA_NOPE + MLA_ROPE
MLA_V = 64
ROPE_THETA = 10000.0
CA_HEADS = 8
CA_HEAD_DIM = 64
REL_CLIP = 128
D_FF = 2816
N_MOD = 6
EPS = 1e-6
NEG_INF = -1e30
LOG2E = math.log2(math.e)

LANES = 128
BF16_SUBLANES = 16
VMEM_LIMIT_BYTES = 56 * 1024 * 1024

PROJ_TM = 1024
MLA_TQ = 256
CA_TQ = 256
CA_WIN = CA_TQ + LEFT_CHUNKS * CHUNK
CA_ROLL_W = 1024
REDUCE_ROWS = 64
ATTN_LOOKAHEAD = 6
FIN_TM = 512
FIN_HALO = BF16_SUBLANES
FIN_ROW_BLOCK = 128
FF_CHUNK = 256
N_FF_CHUNKS = D_FF // FF_CHUNK

_C_QLAT = 0
_C_KVLAT = _C_QLAT + MLA_Q_RANK
_C_KA = _C_KVLAT + MLA_KV_RANK
_C_KB = _C_KA + LANES
_C_CQ = _C_KB + LANES
_C_CK = _C_CQ + CA_HEADS * CA_HEAD_DIM
D_IN_EXT = _C_CK + CA_HEADS * CA_HEAD_DIM

F32 = jnp.float32
BF16 = jnp.bfloat16


def _dot(a, b):
    return jnp.dot(a, b, preferred_element_type=F32)


def _dot_nt(a, b):
    return lax.dot_general(a, b, (((1,), (1,)), ((), ())), preferred_element_type=F32)


def _compiler_params(semantics):
    return pltpu.CompilerParams(dimension_semantics=semantics, vmem_limit_bytes=VMEM_LIMIT_BYTES)


def _ada_kernel(c_ref, w_ref, b_ref, o_ref):
    c = c_ref[...]
    s = c / (1.0 + jnp.exp(-c))
    o_ref[...] = _dot(s.astype(BF16), w_ref[...].astype(BF16)) + b_ref[...]


def _ada(c, w_ada, b_ada):
    bsz = c.shape[0]
    n_out = w_ada.shape[1]
    tn = D_MODEL
    return pl.pallas_call(
        _ada_kernel,
        grid=(n_out // tn,),
        in_specs=[
            pl.BlockSpec((bsz, D_MODEL), lambda j: (0, 0)),
            pl.BlockSpec((D_MODEL, tn), lambda j: (0, j)),
            pl.BlockSpec((1, tn), lambda j: (0, j)),
        ],
        out_specs=pl.BlockSpec((bsz, tn), lambda j: (0, j)),
        out_shape=jax.ShapeDtypeStruct((bsz, n_out), F32),
        compiler_params=_compiler_params(("arbitrary",)),
        name="ada",
    )(c, w_ada, b_ada.reshape(1, n_out))


def _expand(d, e_ref):
    hi = d.astype(BF16)
    r1 = d - hi.astype(F32)
    mid = r1.astype(BF16)
    lo = (r1 - mid.astype(F32)).astype(BF16)
    e = e_ref[...]
    return _dot(hi, e) + _dot(mid, e) + _dot(lo, e)


def _rope_kernel(pos_ref, rep_ref, inv_ref, ec_ref, es_ref, base_ref, tc_ref, ts_ref):
    rows = pos_ref.shape[0]
    tok_per_row = ec_ref.shape[1] // LANES
    pos = _expand(pos_ref[...].astype(F32), rep_ref)
    ang = pos * inv_ref[...]
    tc_wide = _expand(jnp.cos(ang), ec_ref) + base_ref[...]
    ts_wide = _expand(jnp.sin(ang), es_ref)
    for t in range(tok_per_row):
        tc_ref[pl.ds(t, rows, stride=tok_per_row), :] = tc_wide[:, t * LANES:(t + 1) * LANES]
        ts_ref[pl.ds(t, rows, stride=tok_per_row), :] = ts_wide[:, t * LANES:(t + 1) * LANES]


def _rope_tables(positions):
    half = MLA_ROPE // 2
    bsz, seq = positions.shape
    n_tok = bsz * seq
    tok_per_row = LANES // half
    rows = n_tok // tok_per_row
    inv = jnp.power(ROPE_THETA, -jnp.arange(half, dtype=F32) / half)
    inv_t = jnp.tile(inv, tok_per_row).reshape(1, LANES)
    spread = np.repeat(np.eye(tok_per_row, dtype=np.float32), half, axis=1)

    src = np.arange(LANES)
    sel_c = np.zeros((LANES, tok_per_row * LANES), np.float32)
    sel_s = np.zeros((LANES, tok_per_row * LANES), np.float32)
    for rep, sign in enumerate((-1.0, 1.0, -1.0, 1.0)):
        dst = (src // half) * LANES + MLA_NOPE + rep * half + src % half
        sel_c[src, dst] = 1.0
        sel_s[src, dst] = sign
    base = np.tile((np.arange(LANES) < MLA_NOPE).astype(np.float32), tok_per_row).reshape(1, -1)

    tr = 512
    wide = tok_per_row * LANES
    const = lambda i: (0, 0)
    tc, ts = pl.pallas_call(
        _rope_kernel,
        grid=(rows // tr,),
        in_specs=[
            pl.BlockSpec((tr, tok_per_row), lambda i: (i, 0)),
            pl.BlockSpec((tok_per_row, LANES), const),
            pl.BlockSpec((1, LANES), const),
            pl.BlockSpec((LANES, wide), const),
            pl.BlockSpec((LANES, wide), const),
            pl.BlockSpec((1, wide), const),
        ],
        out_specs=[pl.BlockSpec((tr * tok_per_row, LANES), lambda i: (i, 0))] * 2,
        out_shape=[jax.ShapeDtypeStruct((n_tok, LANES), F32)] * 2,
        compiler_params=_compiler_params(("arbitrary",)),
        name="rope",
    )(positions.reshape(rows, tok_per_row), jnp.asarray(spread, BF16), inv_t,
      jnp.asarray(sel_c, BF16), jnp.asarray(sel_s, BF16), jnp.asarray(base))
    return tc.reshape(bsz, seq, LANES), ts.reshape(bsz, seq, LANES)


def _bias_kernel(y_ref, o_ref):
    tq, win = CA_TQ, CA_WIN
    full = jnp.broadcast_to(y_ref[0], (tq, CA_ROLL_W))
    rolled = pltpu.roll(full, 0, 1, stride=1, stride_axis=0)
    q_chunk = lax.broadcasted_iota(jnp.int32, (tq, 1), 0) // CHUNK
    k_chunk = lax.broadcasted_iota(jnp.int32, (1, win), 1) // CHUNK
    valid = jnp.logical_and(k_chunk >= q_chunk, k_chunk <= q_chunk + LEFT_CHUNKS)
    o_ref[0] = jnp.where(valid, rolled[:, :win] * LOG2E, NEG_INF).T


def _bias_tiles(rel_bias):
    n_heads = rel_bias.shape[0]
    t = jnp.arange(CA_ROLL_W)
    d = jnp.where(t < CA_WIN, t, t - CA_ROLL_W)
    idx = jnp.clip(LEFT_CHUNKS * CHUNK - d, -REL_CLIP, REL_CLIP) + REL_CLIP
    y = rel_bias[:, idx].reshape(n_heads, 1, CA_ROLL_W)
    return pl.pallas_call(
        _bias_kernel,
        grid=(n_heads,),
        in_specs=[pl.BlockSpec((1, 1, CA_ROLL_W), lambda h: (h, 0, 0))],
        out_specs=pl.BlockSpec((1, CA_WIN, CA_TQ), lambda h: (h, 0, 0)),
        out_shape=jax.ShapeDtypeStruct((n_heads, CA_WIN, CA_TQ), F32),
        compiler_params=_compiler_params(("arbitrary",)),
        name="bias",
    )(y)


def _proj_kernel(x_ref, mod_ref, tc_ref, ts_ref, gattn_ref, gql_ref, gkvl_ref, gv_ref,
                 win_ref, wq_ref, wkv_ref, wcvt_ref,
                 q_ref, k_ref, v_ref, cq_ref, ck_ref, cvt_ref):
    x = x_ref[0]
    mod = mod_ref[0]
    sh, sc = mod[0:1, :], mod[1:2, :]
    y = x * lax.rsqrt(jnp.mean(x * x, axis=-1, keepdims=True) + EPS) * gattn_ref[...]
    h = (y * (1.0 + sc) + sh).astype(BF16)
    proj = _dot(h, win_ref[...])

    lane = lax.broadcasted_iota(jnp.int32, (1, LANES), 1)
    gv = gv_ref[...]
    g_q, g_kn, g_ka, g_kb, g_cq, g_ck = (gv[r:r + 1, :] for r in range(6))

    tc = tc_ref[0]
    ts = ts_ref[0]
    tq = jnp.where(lane < MLA_QK, tc, ts)

    ql = proj[:, _C_QLAT:_C_QLAT + MLA_Q_RANK]
    qn = (ql * lax.rsqrt(jnp.mean(ql * ql, axis=-1, keepdims=True) + EPS) * gql_ref[...]).astype(BF16)
    qp = _dot(qn, wq_ref[...])
    qmul = g_q * tq
    for hd in range(MLA_HEADS):
        qh = qp[:, hd * LANES:(hd + 1) * LANES]
        ss = jnp.sum(jnp.where(lane < MLA_QK, qh * qh, 0.0), axis=-1, keepdims=True)
        r = lax.rsqrt(ss * (1.0 / MLA_QK) + EPS)
        q_ref[0, :, hd * LANES:(hd + 1) * LANES] = (qh * r * qmul).astype(BF16)

    kvl = proj[:, _C_KVLAT:_C_KVLAT + MLA_KV_RANK]
    kvn = (kvl * lax.rsqrt(jnp.mean(kvl * kvl, axis=-1, keepdims=True) + EPS) * gkvl_ref[...]).astype(BF16)
    kvp = _dot(kvn, wkv_ref[...])
    ka = proj[:, _C_KA:_C_KA + LANES]
    kb = proj[:, _C_KB:_C_KB + LANES]
    krot = ka * g_ka * tc + kb * g_kb * ts
    ss_rope = jnp.sum(jnp.where(lane < MLA_QK, ka * ka, 0.0), axis=-1, keepdims=True)
    for hd in range(MLA_HEADS):
        kh = kvp[:, hd * LANES:(hd + 1) * LANES]
        ss = jnp.sum(kh * kh, axis=-1, keepdims=True) + ss_rope
        r = lax.rsqrt(ss * (1.0 / MLA_QK) + EPS)
        k_ref[0, :, hd * LANES:(hd + 1) * LANES] = ((kh * g_kn + krot) * r).astype(BF16)
    v_ref[0] = kvp[:, MLA_HEADS * LANES:].astype(BF16)

    lo = lane < CA_HEAD_DIM
    for src, gain, dst in ((_C_CQ, g_cq, cq_ref), (_C_CK, g_ck, ck_ref)):
        for p in range(CA_HEADS // 2):
            xx = proj[:, src + p * LANES:src + (p + 1) * LANES]
            x2 = xx * xx
            s_all = jnp.sum(x2, axis=-1, keepdims=True)
            s_lo = jnp.sum(jnp.where(lo, x2, 0.0), axis=-1, keepdims=True)
            r_lo = lax.rsqrt(s_lo * (1.0 / CA_HEAD_DIM) + EPS)
            r_hi = lax.rsqrt((s_all - s_lo) * (1.0 / CA_HEAD_DIM) + EPS)
            dst[0, :, p * LANES:(p + 1) * LANES] = (xx * jnp.where(lo, r_lo, r_hi) * gain).astype(BF16)
    cvt_ref[0] = _dot_nt(wcvt_ref[...], h).astype(BF16)


def _proj(x, mod, tc, ts, g_attn, g_ql, g_kvl, gvec, w_in_ext, w_q_ext, w_kv_ext, w_cvt):
    bsz, seq, _ = x.shape
    tm = PROJ_TM
    tok = lambda b, i: (b, i, 0)
    tok_t = lambda b, i: (b, 0, i)
    const2 = lambda b, i: (0, 0)
    wide = MLA_HEADS * LANES
    narrow = CA_HEADS * CA_HEAD_DIM
    row_major = lambda w: (jax.ShapeDtypeStruct((bsz, seq, w), BF16), pl.BlockSpec((1, tm, w), tok))
    feat_major = lambda w: (jax.ShapeDtypeStruct((bsz, w, seq), BF16), pl.BlockSpec((1, w, tm), tok_t))
    outs = [row_major(wide), row_major(wide), row_major(MLA_HEADS * MLA_V),
            row_major(narrow), row_major(narrow), feat_major(narrow)]
    out_shapes = [o[0] for o in outs]
    out_specs = [o[1] for o in outs]
    return pl.pallas_call(
        _proj_kernel,
        grid=(bsz, seq // tm),
        in_specs=[
            pl.BlockSpec((1, tm, D_MODEL), tok),
            pl.BlockSpec((1, N_MOD, D_MODEL), lambda b, i: (b, 0, 0)),
            pl.BlockSpec((1, tm, LANES), tok),
            pl.BlockSpec((1, tm, LANES), tok),
            pl.BlockSpec((1, D_MODEL), const2),
            pl.BlockSpec((1, MLA_Q_RANK), const2),
            pl.BlockSpec((1, MLA_KV_RANK), const2),
            pl.BlockSpec((8, LANES), const2),
            pl.BlockSpec((D_MODEL, D_IN_EXT), const2),
            pl.BlockSpec((MLA_Q_RANK, wide), const2),
            pl.BlockSpec((MLA_KV_RANK, wide + MLA_HEADS * MLA_V), const2),
            pl.BlockSpec((narrow, D_MODEL), const2),
        ],
        out_specs=out_specs,
        out_shape=out_shapes,
        compiler_params=_compiler_params(("parallel", "parallel")),
        name="proj",
    )(x, mod, tc, ts, g_attn, g_ql, g_kvl, gvec, w_in_ext, w_q_ext, w_kv_ext, w_cvt)


def _softmax_pv_t(score_parts, value_parts):
    def fold(x, op):
        return op(x.reshape(x.shape[0] // REDUCE_ROWS, REDUCE_ROWS, x.shape[1]), axis=0)

    m = fold(score_parts[0], jnp.max)
    for s in score_parts[1:]:
        m = jnp.maximum(m, fold(s, jnp.max))
    m = m.max(axis=0, keepdims=True)
    l = None
    acc = None
    for s, vt in zip(score_parts, value_parts):
        p = jnp.exp2(s - m)
        ps = fold(p, jnp.sum).sum(axis=0, keepdims=True)
        pv = _dot(vt, p.astype(BF16))
        l = ps if l is None else l + ps
        acc = pv if acc is None else acc + pv
    return acc / l


def _attention_pipeline(units, lookahead):
    pending = [scores() for scores, _ in units[:lookahead]]
    for n, (_, finish) in enumerate(units):
        if n + lookahead < len(units):
            pending.append(units[n + lookahead][0]())
        finish(pending.pop(0))


def _pair_units(n_tiles, scores, finish, store):
    units = []
    outs = []

    def make(i, hh):
        def finish_unit(parts):
            outs.append(finish(i, parts))
            if hh == 1:
                store(i, list(outs))
                outs.clear()
        return (lambda: scores(i, hh)), finish_unit

    for i in range(n_tiles):
        for hh in range(2):
            units.append(make(i, hh))
    return units


def _store_pair(o_ref, lo, hi, outs, head_rows):
    row = lax.broadcasted_iota(jnp.int32, (LANES, 1), 0)
    o_t = jnp.where(row < head_rows, outs[0], outs[1])
    o_ref[0, lo:hi, :] = o_t.T.astype(BF16)


def _softmax_pv(score_parts, value_parts):
    m = score_parts[0].max(axis=-1, keepdims=True)
    for s in score_parts[1:]:
        m = jnp.maximum(m, s.max(axis=-1, keepdims=True))
    l = None
    acc = None
    for s, vb in zip(score_parts, value_parts):
        p = jnp.exp2(s - m)
        ps = jnp.sum(p, axis=-1, keepdims=True)
        pv = _dot(p.astype(BF16), vb)
        l = ps if l is None else l + ps
        acc = pv if acc is None else acc + pv
    return acc / l


def _mla_units(q_ref, k_ref, v_ref, o_ref):
    t = MLA_TQ
    seq = q_ref.shape[1]
    lane = lax.broadcasted_iota(jnp.int32, (1, LANES), 1)
    qry_chunk = lax.broadcasted_iota(jnp.int32, (t, 1), 0) // CHUNK
    key_chunk = lax.broadcasted_iota(jnp.int32, (1, t), 1) // CHUNK
    diag_mask = key_chunk <= qry_chunk

    def scores(i, hh):
        lo = i * t
        hs = slice(hh * LANES, (hh + 1) * LANES)
        q = q_ref[0, lo:lo + t, hs]
        parts = [jnp.where(diag_mask, _dot_nt(q, k_ref[0, lo:lo + t, hs]), NEG_INF)]
        if i > 0:
            parts.insert(0, _dot_nt(q, k_ref[0, 0:lo, hs]))
        return parts

    def finish(i, parts):
        lo = i * t
        vals = [v_ref[0, lo:lo + t, :]]
        if i > 0:
            vals.insert(0, v_ref[0, 0:lo, :])
        return _softmax_pv(parts, vals)

    def store(i, outs):
        o_ref[0, i * t:(i + 1) * t, :] = jnp.where(lane < MLA_V, outs[0], outs[1]).astype(BF16)

    return _pair_units(seq // t, scores, finish, store)


def _ca_units(q_ref, k_ref, vt_ref, bm_ref, o_ref):
    tq, win = CA_TQ, CA_WIN
    seq = q_ref.shape[1]
    lane = lax.broadcasted_iota(jnp.int32, (1, LANES), 1)

    def window(i):
        hi = (i + 1) * tq
        k_lo = max(hi - win, 0)
        return k_lo, hi, k_lo - (hi - win)

    def scores(i, hh):
        k_lo, hi, c_lo = window(i)
        q = q_ref[0, i * tq:hi, :]
        head_lanes = (lane < CA_HEAD_DIM) if hh == 0 else (lane >= CA_HEAD_DIM)
        qm = jnp.where(head_lanes, q, jnp.zeros_like(q))
        return [_dot_nt(k_ref[0, k_lo:hi, :], qm) + bm_ref[hh, c_lo:, :]]

    def finish(i, parts):
        k_lo, hi, _ = window(i)
        return _softmax_pv_t(parts, [vt_ref[0, :, k_lo:hi]])

    def store(i, outs):
        _store_pair(o_ref, i * tq, (i + 1) * tq, outs, CA_HEAD_DIM)

    return _pair_units(seq // tq, scores, finish, store)


def _attn_kernel(q_ref, k_ref, v_ref, cq_ref, ck_ref, cvt_ref, bm_ref, om_ref, oc_ref):
    mla = _mla_units(q_ref, k_ref, v_ref, om_ref)
    ca = _ca_units(cq_ref, ck_ref, cvt_ref, bm_ref, oc_ref)
    assert len(mla) == len(ca)
    units = [u for pair in zip(mla, ca) for u in pair]
    _attention_pipeline(units, ATTN_LOOKAHEAD)


def _attn(q, k, v, cq, ck, cvt, bias_mask):
    bsz, seq, _ = q.shape
    pairs = MLA_HEADS // 2
    assert CA_HEADS // 2 == pairs
    tok = lambda b, p: (b, 0, p)
    wide = pl.BlockSpec((1, seq, 2 * LANES), tok)
    narrow = pl.BlockSpec((1, seq, LANES), tok)
    return pl.pallas_call(
        _attn_kernel,
        grid=(bsz, pairs),
        in_specs=[
            wide, wide, narrow,
            narrow, narrow,
            pl.BlockSpec((1, LANES, seq), lambda b, p: (b, p, 0)),
            pl.BlockSpec((2, CA_WIN, CA_TQ), lambda b, p: (p, 0, 0)),
        ],
        out_specs=[narrow, narrow],
        out_shape=[jax.ShapeDtypeStruct((bsz, seq, MLA_HEADS * MLA_V), BF16),
                   jax.ShapeDtypeStruct((bsz, seq, CA_HEADS * CA_HEAD_DIM), BF16)],
        compiler_params=_compiler_params(("parallel", "parallel")),
        name="attn",
    )(q, k, v, cq, ck, cvt, bias_mask)


def _final_kernel(x_ref, xh_ref, om_ref, omh_ref, oc_ref, och_ref, mod_ref, gmlp_ref,
                  wo_ref, wu_ref, cw_ref, cb_ref, wd_ref,
                  out_ref, h2_sc, ug0_sc, ug1_sc, uv0_sc, uv1_sc, act_sc, acc_sc):
    i = pl.program_id(1)
    tm = FIN_TM
    halo = FIN_HALO
    mod = mod_ref[0]
    g_a, sh_m, sc_m, g_m = mod[2:3, :], mod[3:4, :], mod[4:5, :], mod[5:6, :]

    u_bufs = ((ug0_sc, uv0_sc), (ug1_sc, uv1_sc))
    rb = FIN_ROW_BLOCK
    n_rb = tm // rb

    def up_rows(r):
        return (0, halo + rb) if r == 0 else (halo + r * rb, halo + (r + 1) * rb)

    x_ext = jnp.concatenate([xh_ref[0], x_ref[0]], axis=0)
    o_ext = jnp.concatenate(
        [jnp.concatenate([omh_ref[0], och_ref[0]], axis=-1), jnp.concatenate([om_ref[0], oc_ref[0]], axis=-1)], axis=0)
    x1 = x_ext + g_a * _dot(o_ext, wo_ref[...])
    out_ref[0] = x1[halo:]
    y = x1 * lax.rsqrt(jnp.mean(x1 * x1, axis=-1, keepdims=True) + EPS) * gmlp_ref[...]
    h2 = y * (1.0 + sc_m) + sh_m
    h2_sc[:halo] = jnp.where(i > 0, h2[:halo], 0.0).astype(BF16)
    h2_sc[halo:] = h2[halo:].astype(BF16)

    def up_project(c, r):
        ug_sc, uv_sc = u_bufs[c % 2]
        lo, hi = up_rows(r)
        h2b = h2_sc[lo:hi, :]
        ug_sc[lo:hi, :] = _dot(h2b, wu_ref[:, c * FF_CHUNK:(c + 1) * FF_CHUNK])
        uv_sc[lo:hi, :] = _dot(h2b, wu_ref[:, D_FF + c * FF_CHUNK:D_FF + (c + 1) * FF_CHUNK])

    def conv(u_sc, col, r):
        cw = cw_ref[:, col:col + FF_CHUNK]
        lo = halo + r * rb
        return (u_sc[lo - 2:lo - 2 + rb, :] * cw[0:1, :] + u_sc[lo - 1:lo - 1 + rb, :] * cw[1:2, :]
                + u_sc[lo:lo + rb, :] * cw[2:3, :] + cb_ref[:, col:col + FF_CHUNK])

    def activate(c, r):
        ug_sc, uv_sc = u_bufs[c % 2]
        gcol, vcol = c * FF_CHUNK, D_FF + c * FF_CHUNK
        gate = conv(ug_sc, gcol, r)
        val = conv(uv_sc, vcol, r)
        half = c % 2
        act_sc[r * rb:(r + 1) * rb, half * FF_CHUNK:(half + 1) * FF_CHUNK] = (
            gate / (1.0 + jnp.exp(-gate)) * val).astype(BF16)

    def down_project(c0, c1):
        width = (c1 - c0 + 1) * FF_CHUNK
        down = _dot(act_sc[:, :width], wd_ref[c0 * FF_CHUNK:(c1 + 1) * FF_CHUNK, :])
        if c0 == 0:
            acc_sc[...] = down
        elif c1 < N_FF_CHUNKS - 1:
            acc_sc[...] += down
        else:
            out_ref[0] += g_m * (acc_sc[...] + down)

    for r in range(n_rb):
        up_project(0, r)
    for c in range(N_FF_CHUNKS):
        for r in range(n_rb):
            if c + 1 < N_FF_CHUNKS:
                up_project(c + 1, r)
            activate(c, r)
        if c % 2 == 1 or c == N_FF_CHUNKS - 1:
            down_project(c - c % 2, c)


def _final(x, o_mla, o_ca, mod, g_mlp, w_out, w_up, conv_w, conv_b, w_down):
    bsz, seq, _ = x.shape
    tm, halo = FIN_TM, FIN_HALO
    tok = lambda b, i: (b, i, 0)
    prev = lambda b, i: (b, jnp.maximum(i * (tm // halo) - 1, 0), 0)
    const2 = lambda b, i: (0, 0)
    d_mix = o_mla.shape[-1]
    resident = dict(pipeline_mode=pl.Buffered(1))
    return pl.pallas_call(
        _final_kernel,
        grid=(bsz, seq // tm),
        in_specs=[
            pl.BlockSpec((1, tm, D_MODEL), tok),
            pl.BlockSpec((1, halo, D_MODEL), prev),
            pl.BlockSpec((1, tm, d_mix), tok),
            pl.BlockSpec((1, halo, d_mix), prev),
            pl.BlockSpec((1, tm, d_mix), tok),
            pl.BlockSpec((1, halo, d_mix), prev),
            pl.BlockSpec((1, N_MOD, D_MODEL), lambda b, i: (b, 0, 0)),
            pl.BlockSpec((1, D_MODEL), const2),
            pl.BlockSpec((2 * d_mix, D_MODEL), const2, **resident),
            pl.BlockSpec((D_MODEL, 2 * D_FF), const2, **resident),
            pl.BlockSpec((3, 2 * D_FF), const2),
            pl.BlockSpec((1, 2 * D_FF), const2),
            pl.BlockSpec((D_FF, D_MODEL), const2, **resident),
        ],
        out_specs=pl.BlockSpec((1, tm, D_MODEL), tok),
        out_shape=jax.ShapeDtypeStruct((bsz, seq, D_MODEL), F32),
        scratch_shapes=[
            pltpu.VMEM((halo + tm, D_MODEL), BF16),
            pltpu.VMEM((halo + tm, FF_CHUNK), F32),
            pltpu.VMEM((halo + tm, FF_CHUNK), F32),
            pltpu.VMEM((halo + tm, FF_CHUNK), F32),
            pltpu.VMEM((halo + tm, FF_CHUNK), F32),
            pltpu.VMEM((tm, 2 * FF_CHUNK), BF16),
            pltpu.VMEM((tm, D_MODEL), F32),
        ],
        compiler_params=_compiler_params(("parallel", "arbitrary")),
        name="final",
    )(x, x, o_mla, o_mla, o_ca, o_ca, mod, g_mlp, w_out, w_up, conv_w, conv_b, w_down)


def _prep_layer(w_in, w_q_up, w_kv_up, g_mla_q, g_mla_k, g_ca_q, g_ca_k):
    half = MLA_ROPE // 2
    c0 = MLA_Q_RANK + MLA_KV_RANK
    k1 = w_in[:, c0:c0 + half]
    k2 = w_in[:, c0 + half:c0 + MLA_ROPE]
    z64 = jnp.zeros((D_MODEL, MLA_NOPE), w_in.dtype)
    c_cv = c0 + MLA_ROPE + 2 * CA_HEADS * CA_HEAD_DIM
    w_in_ext = jnp.concatenate(
        [w_in[:, :c0], z64, k1, k2, k1, k2, z64, k2, k1, k2, k1, w_in[:, c0 + MLA_ROPE:c_cv]], axis=1).astype(BF16)
    w_cvt = w_in[:, c_cv:].T.astype(BF16)

    wq = w_q_up.reshape(MLA_Q_RANK, MLA_HEADS, MLA_QK)
    x1 = wq[..., MLA_NOPE:MLA_NOPE + half]
    x2 = wq[..., MLA_NOPE + half:]
    w_q_ext = jnp.concatenate([wq, x2, x1], axis=-1).reshape(MLA_Q_RANK, MLA_HEADS * LANES).astype(BF16)

    wkv = w_kv_up.reshape(MLA_KV_RANK, MLA_HEADS, MLA_NOPE + MLA_V)
    wk = jnp.concatenate([wkv[..., :MLA_NOPE], jnp.zeros((MLA_KV_RANK, MLA_HEADS, LANES - MLA_NOPE), wkv.dtype)], axis=-1)
    w_kv_ext = jnp.concatenate(
        [wk.reshape(MLA_KV_RANK, MLA_HEADS * LANES), wkv[..., MLA_NOPE:].reshape(MLA_KV_RANK, MLA_HEADS * MLA_V)],
        axis=1).astype(BF16)

    gq1, gq2 = g_mla_q[MLA_NOPE:MLA_NOPE + half], g_mla_q[MLA_NOPE + half:]
    gk1, gk2 = g_mla_k[MLA_NOPE:MLA_NOPE + half], g_mla_k[MLA_NOPE + half:]
    z = jnp.zeros((MLA_NOPE,), F32)
    rows = [
        jnp.concatenate([g_mla_q, gq2, gq1]) * (MLA_QK ** -0.5 * LOG2E),
        jnp.concatenate([g_mla_k[:MLA_NOPE], z]),
        jnp.concatenate([z, gk1, gk2, gk1, gk2]),
        jnp.concatenate([z, gk2, gk1, gk2, gk1]),
        jnp.concatenate([g_ca_q, g_ca_q]) * (CA_HEAD_DIM ** -0.5 * LOG2E),
        jnp.concatenate([g_ca_k, g_ca_k]),
        jnp.zeros((LANES,), F32),
        jnp.zeros((LANES,), F32),
    ]
    gvec = jnp.stack(rows)
    return w_in_ext, w_q_ext, w_kv_ext, w_cvt, gvec


def kernel(x, c, positions, w_ada, b_ada, g_attn_norm, w_in, g_q_latent, g_kv_latent, w_q_up, w_kv_up, g_mla_q, g_mla_k, g_ca_q, g_ca_k, rel_bias, w_out, g_mlp_norm, w_up, conv_w, conv_b, w_down):
    bsz, seq, _ = x.shape
    depth = w_ada.shape[0]

    tc, ts = _rope_tables(positions)

    for l in range(depth):
        w_in_ext, w_q_ext, w_kv_ext, w_cvt, gvec = _prep_layer(
            w_in[l], w_q_up[l], w_kv_up[l], g_mla_q[l], g_mla_k[l], g_ca_q[l], g_ca_k[l])
        mod = _ada(c, w_ada[l], b_ada[l]).reshape(bsz, N_MOD, D_MODEL)
        bias_mask = _bias_tiles(rel_bias[l])
        q, k, v, cq, ck, cvt = _proj(
            x, mod, tc, ts, g_attn_norm[l].reshape(1, -1), g_q_latent[l].reshape(1, -1),
            g_kv_latent[l].reshape(1, -1), gvec, w_in_ext, w_q_ext, w_kv_ext, w_cvt)
        o_mla, o_ca = _attn(q, k, v, cq, ck, cvt, bias_mask)
        x = _final(x, o_mla, o_ca, mod, g_mlp_norm[l].reshape(1, -1), w_out[l].astype(BF16),
                   w_up[l].astype(BF16), conv_w[l], conv_b[l].reshape(1, -1), w_down[l].astype(BF16))
    return x
```
